```python
import math
import jax, jax.numpy as jnp
from jax import lax
import numpy as np

D_MODEL = 1024
BATCH = 32
SEQ = 256
DEPTH = 1
DEC_BATCH = 4
DEC_SEQ = 4096
PAST_LEN = 512

GRID_W = 64
H_DIFF = 4
DH_DIFF = 64
H_GQA = 8
H_KV = 2
DH_GQA = 64
G_GQA = H_GQA // H_KV
D_FF = 2816
QBLK = 128
ROPE_THETA = 10000.0
EPS = 1e-6
N_MOD = 9

DQ_DIFF = H_DIFF * 2 * DH_DIFF
DQ_GQA = H_GQA * DH_GQA
DKV_GQA = H_KV * DH_GQA
D_IN = 3 * DQ_DIFF + DQ_GQA + 2 * DKV_GQA
D_MIX = DQ_DIFF + DQ_GQA
SPLITS = (DQ_DIFF, 2 * DQ_DIFF, 3 * DQ_DIFF, 3 * DQ_DIFF + DQ_GQA, 3 * DQ_DIFF + DQ_GQA + DKV_GQA)

kernel_name = "hybrid_diffattn_gqa_prefix_dit_step"


def lambda_init(layer_idx):
    return 0.8 - 0.6 * math.exp(-0.3 * layer_idx)


def rmsnorm(x, g):
    xf = x.astype(jnp.float32)
    y = xf * lax.rsqrt(jnp.mean(xf * xf, axis=-1, keepdims=True) + EPS)
    return (y * g.astype(jnp.float32)).astype(x.dtype)


def ada_mod(cvec, w, b):
    m = jax.nn.silu(cvec) @ w + b
    return m.reshape(*cvec.shape[:-1], N_MOD, D_MODEL)


def modulate(x, g, shift, scale):
    return rmsnorm(x, g) * (1 + scale[..., None, :]) + shift[..., None, :]


def swiglu(h, w_gu, w_down):
    g, u = jnp.split(h @ w_gu, 2, axis=-1)
    return (jax.nn.silu(g) * u) @ w_down


def ffn_sublayer(x, mod, i, g, w_gu, w_down):
    shift, scale, gate = mod[..., 3 * i, :], mod[..., 3 * i + 1, :], mod[..., 3 * i + 2, :]
    return x + 0.5 * gate[..., None, :] * swiglu(modulate(x, g, shift, scale), w_gu, w_down)


def axial_rope(n_tokens, head_dim):
    rows = n_tokens // GRID_W
    row = jnp.repeat(jnp.arange(rows), GRID_W)
    col = jnp.tile(jnp.arange(GRID_W), rows)
    half = head_dim // 2
    inv = ROPE_THETA ** (-jnp.arange(0, half, 2, dtype=jnp.float32) / half)
    ang = jnp.stack([row[:, None] * inv, col[:, None] * inv], axis=1)
    return jnp.cos(ang), jnp.sin(ang)


def apply_rope(x, cos, sin):
    B, T, H, d = x.shape
    xs = x.astype(jnp.float32).reshape(B, T, H, 2, 2, d // 4)
    x1, x2 = xs[..., 0, :], xs[..., 1, :]
    c = cos[None, :, None]
    s = sin[None, :, None]
    out = jnp.stack([x1 * c - x2 * s, x1 * s + x2 * c], axis=-2)
    return out.reshape(B, T, H, d).astype(x.dtype)


def sweep_query_blocks(fn, qs):
    T = qs[0].shape[1]
    nb = T // QBLK
    blocks = tuple(jnp.moveaxis(q.reshape(q.shape[0], nb, QBLK, *q.shape[2:]), 1, 0) for q in qs)
    out = lax.map(lambda qb: fn(*qb), blocks)
    return jnp.moveaxis(out, 0, 1).reshape(out.shape[1], T, *out.shape[3:])


def diff_attn_block(q, k, v, lam):
    s = jnp.einsum('bqhmd,bkhmd->bhmqk', q, k).astype(jnp.float32) * (DH_DIFF ** -0.5)
    p = jax.nn.softmax(s, axis=-1)
    a = p[:, :, 0] - lam * p[:, :, 1]
    return jnp.einsum('bhqk,bkhe->bqhe', a.astype(v.dtype), v)


def gqa_block(q, k, v):
    s = jnp.einsum('bqngd,bknd->bngqk', q, k).astype(jnp.float32) * (DH_GQA ** -0.5)
    p = jax.nn.softmax(s, axis=-1)
    return jnp.einsum('bngqk,bknd->bqngd', p.astype(v.dtype), v)


def mixer_projections(h, w_in, q_norm, k_norm):
    B, T, _ = h.shape
    dq, dk, dv, gq, gk, gv = jnp.split(h @ w_in, list(SPLITS), axis=-1)
    dq = dq.reshape(B, T, H_DIFF, 2, DH_DIFF)
    dk = dk.reshape(B, T, H_DIFF, 2, DH_DIFF)
    dv = dv.reshape(B, T, H_DIFF, 2 * DH_DIFF)
    gq = rmsnorm(gq.reshape(B, T, H_GQA, DH_GQA), q_norm)
    gk = rmsnorm(gk.reshape(B, T, H_KV, DH_GQA), k_norm)
    gv = gv.reshape(B, T, H_KV, DH_GQA)
    return dq, dk, dv, gq, gk, gv


def rope_pairs(x, cos, sin):
    B, T, H, M, d = x.shape
    return apply_rope(x.reshape(B, T, H * M, d), cos, sin).reshape(B, T, H, M, d)


def diff_lambda(lq1, lk1, lq2, lk2, lam_init):
    f = jnp.float32
    return (jnp.exp(jnp.sum(lq1.astype(f) * lk1.astype(f)))
            - jnp.exp(jnp.sum(lq2.astype(f) * lk2.astype(f))) + lam_init)


def mixer_output(o_diff, o_gqa, subln_g, lam_init, w_out):
    B, T = o_diff.shape[:2]
    o_diff = rmsnorm(o_diff, subln_g) * (1.0 - lam_init)
    o = jnp.concatenate([o_diff.reshape(B, T, DQ_DIFF), o_gqa.reshape(B, T, DQ_GQA)], axis=-1)
    return o @ w_out


def setup_inputs(seed: int = 0) -> dict:
    key = jax.random.key(seed)
    ks = jax.random.split(key, 32)
    f = jnp.float32

    def nrm(k, shape, scale=1.0):
        return jax.random.normal(k, shape, f) * scale

    def gain(k, shape):
        return 1.0 + 0.05 * jax.random.normal(k, shape, f)

    return {
        "x_prompt": nrm(ks[0], (BATCH, SEQ, D_MODEL)),
        "x_sample": nrm(ks[1], (DEC_BATCH, DEC_SEQ, D_MODEL)),
        "c": nrm(ks[2], (DEC_BATCH, D_MODEL)),
        "cache_diff_k": nrm(ks[3], (DEC_BATCH, DEPTH, PAST_LEN, H_DIFF, 2, DH_DIFF)),
        "cache_diff_v": nrm(ks[4], (DEC_BATCH, DEPTH, PAST_LEN, H_DIFF, 2 * DH_DIFF)),
        "cache_gqa_k": nrm(ks[5], (DEC_BATCH, DEPTH, PAST_LEN, H_KV, DH_GQA)),
        "cache_gqa_v": nrm(ks[6], (DEC_BATCH, DEPTH, PAST_LEN, H_KV, DH_GQA)),
        "c_ctx": nrm(ks[7], (D_MODEL,)),
        "w_ada": nrm(ks[8], (DEPTH, D_MODEL, N_MOD * D_MODEL), 0.5 * D_MODEL ** -0.5),
        "b_ada": nrm(ks[9], (DEPTH, N_MOD * D_MODEL), 0.02),
        "norm_ff1": gain(ks[10], (DEPTH, D_MODEL)),
        "w_ff1_gu": nrm(ks[11], (DEPTH, D_MODEL, 2 * D_FF), D_MODEL ** -0.5),
        "w_ff1_down": nrm(ks[12], (DEPTH, D_FF, D_MODEL), D_FF ** -0.5),
        "norm_mix": gain(ks[13], (DEPTH, D_MODEL)),
        "w_in": nrm(ks[14], (DEPTH, D_MODEL, D_IN), D_MODEL ** -0.5),
        "q_norm": gain(ks[15], (DEPTH, DH_GQA)),
        "k_norm": gain(ks[16], (DEPTH, DH_GQA)),
        "lambda_q1": nrm(ks[17], (DEPTH, DH_DIFF), 0.1),
        "lambda_k1": nrm(ks[18], (DEPTH, DH_DIFF), 0.1),
        "lambda_q2": nrm(ks[19], (DEPTH, DH_DIFF), 0.1),
        "lambda_k2": nrm(ks[20], (DEPTH, DH_DIFF), 0.1),
        "subln": gain(ks[21], (DEPTH, 2 * DH_DIFF)),
        "w_out": nrm(ks[22], (DEPTH, D_MIX, D_MODEL), D_MIX ** -0.5),
        "norm_ff2": gain(ks[23], (DEPTH, D_MODEL)),
        "w_ff2_gu": nrm(ks[24], (DEPTH, D_MODEL, 2 * D_FF), D_MODEL ** -0.5),
        "w_ff2_down": nrm(ks[25], (DEPTH, D_FF, D_MODEL), D_FF ** -0.5),
        "final_norm": gain(ks[26], (D_MODEL,)),
    }


def reference(x_prompt, x_sample, c, cache_diff_k, cache_diff_v, cache_gqa_k, cache_gqa_v,
              c_ctx, w_ada, b_ada, norm_ff1, w_ff1_gu, w_ff1_down, norm_mix, w_in,
              q_norm, k_norm, lambda_q1, lambda_k1, lambda_q2, lambda_k2, subln, w_out,
              norm_ff2, w_ff2_gu, w_ff2_down, final_norm):
    xc = x_prompt
    Bc, Tc = xc.shape[:2]
    new_dk, new_dv, new_gk, new_gv = [], [], [], []
    for l in range(DEPTH):
        lam_init = lambda_init(l)
        lam = diff_lambda(lambda_q1[l], lambda_k1[l], lambda_q2[l], lambda_k2[l], lam_init)
        mod = ada_mod(c_ctx, w_ada[l], b_ada[l])
        xc = ffn_sublayer(xc, mod, 0, norm_ff1[l], w_ff1_gu[l], w_ff1_down[l])
        h = modulate(xc, norm_mix[l], mod[..., 3, :], mod[..., 4, :])
        dq, dk, dv, gq, gk, gv = mixer_projections(h, w_in[l], q_norm[l], k_norm[l])
        o_diff = sweep_query_blocks(lambda q, dk=dk, dv=dv: diff_attn_block(q, dk, dv, lam), (dq,))
        o_gqa = sweep_query_blocks(lambda q, gk=gk, gv=gv: gqa_block(q, gk, gv),
                                   (gq.reshape(Bc, Tc, H_KV, G_GQA, DH_GQA),))
        xc = xc + mod[..., 5, :][..., None, :] * mixer_output(o_diff, o_gqa, subln[l], lam_init, w_out[l])
        xc = ffn_sublayer(xc, mod, 2, norm_ff2[l], w_ff2_gu[l], w_ff2_down[l])
        new_dk.append(dk)
        new_dv.append(dv)
        new_gk.append(gk)
        new_gv.append(gv)
    y_prompt = rmsnorm(xc, final_norm)
    new_diff_k = jnp.stack(new_dk, axis=1)
    new_diff_v = jnp.stack(new_dv, axis=1)
    new_gqa_k = jnp.stack(new_gk, axis=1)
    new_gqa_v = jnp.stack(new_gv, axis=1)

    xs = x_sample
    Bs, Ts = xs.shape[:2]
    cos_d, sin_d = axial_rope(Ts, DH_DIFF)
    cos_g, sin_g = axial_rope(Ts, DH_GQA)
    for l in range(DEPTH):
        lam_init = lambda_init(l)
        lam = diff_lambda(lambda_q1[l], lambda_k1[l], lambda_q2[l], lambda_k2[l], lam_init)
        mod = ada_mod(c, w_ada[l], b_ada[l])
        xs = ffn_sublayer(xs, mod, 0, norm_ff1[l], w_ff1_gu[l], w_ff1_down[l])
        h = modulate(xs, norm_mix[l], mod[..., 3, :], mod[..., 4, :])
        dq, dk, dv, gq, gk, gv = mixer_projections(h, w_in[l], q_norm[l], k_norm[l])
        dq = rope_pairs(dq, cos_d, sin_d)
        dk = rope_pairs(dk, cos_d, sin_d)
        gq = apply_rope(gq, cos_g, sin_g)
        gk = apply_rope(gk, cos_g, sin_g)
        dk_all = jnp.concatenate([cache_diff_k[:, l], dk], axis=1)
        dv_all = jnp.concatenate([cache_diff_v[:, l], dv], axis=1)
        gk_all = jnp.concatenate([cache_gqa_k[:, l], gk], axis=1)
        gv_all = jnp.concatenate([cache_gqa_v[:, l], gv], axis=1)
        o_diff = sweep_query_blocks(lambda q, k=dk_all, v=dv_all: diff_attn_block(q, k, v, lam), (dq,))
        o_gqa = sweep_query_blocks(lambda q, k=gk_all, v=gv_all: gqa_block(q, k, v),
                                   (gq.reshape(Bs, Ts, H_KV, G_GQA, DH_GQA),))
        xs = xs + mod[..., 5, :][..., None, :] * mixer_output(o_diff, o_gqa, subln[l], lam_init, w_out[l])
        xs = ffn_sublayer(xs, mod, 2, norm_ff2[l], w_ff2_gu[l], w_ff2_down[l])
    y_sample = rmsnorm(xs, final_norm)

    return (y_prompt, y_sample, new_diff_k, new_diff_v, new_gqa_k, new_gqa_v)
```

```python
import functools
import math

import jax
import jax.numpy as jnp
from jax import lax
from jax.experimental import pallas as pl
from jax.experimental.pallas import tpu as pltpu

F32 = jnp.float32
BF16 = jnp.bfloat16

D_MODEL = 1024
N_MOD = 9
D_FF = 2816
H_DIFF = 4
DH = 64
H_GQA = 8
H_KV = 2
G_GQA = H_GQA // H_KV
GRID_W = 64
ROPE_THETA = 10000.0
EPS = 1e-6
LAMBDA_INIT = 0.8 - 0.6 * math.exp(-0.3 * 0)
DQ_DIFF = H_DIFF * 2 * DH
DQ_GQA = H_GQA * DH
DKV_GQA = H_KV * DH
D_IN = 3 * DQ_DIFF + DQ_GQA + 2 * DKV_GQA
D_MIX = DQ_DIFF + DQ_GQA
D_KV_ALL = DQ_DIFF + 2 * DKV_GQA
D_NORMED = DQ_GQA + 2 * DKV_GQA
SCORE_SCALE = DH ** -0.5

LANES = 128
VMEM_LIMIT_BYTES = 60000 * 1024

MOD_ROWS = 8
MOD_TILE_N = 1152
FFN_TILE = 256
PROJ_TILE = 512
ATTN_TILE_Q = 256


def _sigmoid(x):
    return 1.0 / (1.0 + jnp.exp(-x))


def _rms(x):
    return x * lax.rsqrt(jnp.mean(x * x, axis=-1, keepdims=True) + EPS)


def _dot(a, b):
    return jnp.dot(a, b, preferred_element_type=F32)


def _dot_nt(a, b):
    return lax.dot_general(a, b, (((1,), (1,)), ((), ())), preferred_element_type=F32)


def _resident(shape):
    return pl.BlockSpec(shape, lambda *_: (0,) * len(shape), pipeline_mode=pl.Buffered(1))


def _mod_kernel(c_ref, w_ref, b_ref, o_ref):
    c = c_ref[...]
    s = c * _sigmoid(c)
    o_ref[...] = _dot(s.astype(BF16), w_ref[...].astype(BF16)) + b_ref[...]


def _mod_call(cvecs, w_ada, b_ada):
    n = w_ada.shape[1]
    return pl.pallas_call(
        _mod_kernel,
        grid=(n // MOD_TILE_N,),
        in_specs=[
            pl.BlockSpec((MOD_ROWS, D_MODEL), lambda j: (0, 0)),
            pl.BlockSpec((D_MODEL, MOD_TILE_N), lambda j: (0, j)),
            pl.BlockSpec((1, MOD_TILE_N), lambda j: (0, j)),
        ],
        out_specs=pl.BlockSpec((MOD_ROWS, MOD_TILE_N), lambda j: (0, j)),
        out_shape=jax.ShapeDtypeStruct((MOD_ROWS, n), F32),
        name="mod",
    )(cvecs, w_ada, b_ada)


def _ffn_kernel(*refs, sub, pre, final):
    refs = list(refs)
    x_ref, mod_ref = refs[:2]
    pos = 2
    if pre:
        od_ref, og_ref, wout_ref = refs[pos:pos + 3]
        pos += 3
    g_ref, wgu_ref, wd_ref = refs[pos:pos + 3]
    pos += 3
    if final:
        fg_ref = refs[pos]
        pos += 1
    o_ref = refs[pos]

    x = x_ref[...]
    mod = mod_ref[0]
    if pre:
        o = jnp.concatenate([od_ref[...], og_ref[...]], axis=1)
        x = x + mod[5:6] * _dot(o, wout_ref[...])
    shift = mod[3 * sub:3 * sub + 1]
    scale = mod[3 * sub + 1:3 * sub + 2]
    gate = mod[3 * sub + 2:3 * sub + 3]
    h = (_rms(x) * g_ref[...]) * (1.0 + scale) + shift
    gu = _dot(h.astype(BF16), wgu_ref[...])
    g = gu[:, :D_FF]
    u = gu[:, D_FF:]
    act = (g * _sigmoid(g)) * u
    y = _dot(act.astype(BF16), wd_ref[...])
    x = x + (0.5 * gate) * y
    if final:
        x = _rms(x) * fg_ref[...]
    o_ref[...] = x


def _ffn_call(x, mod, row_map, gain, wgu, wd, *, sub, pre=None, final_gain=None, name):
    t = x.shape[0]
    tm = FFN_TILE
    row_spec = lambda w: pl.BlockSpec((tm, w), lambda i: (i, 0))
    in_specs = [row_spec(D_MODEL), pl.BlockSpec((1, N_MOD, D_MODEL), row_map)]
    args = [x, mod]
    if pre is not None:
        od, og, wout = pre
        in_specs += [row_spec(DQ_DIFF), row_spec(DQ_GQA), _resident((D_MIX, D_MODEL))]
        args += [od, og, wout]
    in_specs += [_resident((1, D_MODEL)), _resident((D_MODEL, 2 * D_FF)), _resident((D_FF, D_MODEL))]
    args += [gain, wgu, wd]
    if final_gain is not None:
        in_specs.append(_resident((1, D_MODEL)))
        args.append(final_gain)
    return pl.pallas_call(
        functools.partial(_ffn_kernel, sub=sub, pre=pre is not None, final=final_gain is not None),
        grid=(t // tm,),
        in_specs=in_specs,
        out_specs=row_spec(D_MODEL),
        out_shape=jax.ShapeDtypeStruct((t, D_MODEL), F32),
        compiler_params=pltpu.CompilerParams(vmem_limit_bytes=VMEM_LIMIT_BYTES),
        name=name,
    )(*args)


def _rope(x, cos, sin, first_of_pair):
    w = x.shape[1]
    partner = jnp.where(first_of_pair, pltpu.roll(x, w - 16, 1), pltpu.roll(x, 16, 1))
    reps = w // LANES
    cos_w = jnp.concatenate([cos] * reps, axis=1) if reps > 1 else cos
    sin_w = jnp.concatenate([sin] * reps, axis=1) if reps > 1 else sin
    return x * cos_w + partner * sin_w


def _proj_kernel(*refs, rope, emit_f32):
    refs = list(refs)
    x_ref, mod_ref, g_ref, w_ref, seg_ref, qkn_ref = refs[:6]
    pos = 6
    if rope:
        cos_ref, sin_ref = refs[pos:pos + 2]
        pos += 2
    q_ref, k_ref, v_ref = refs[pos:pos + 3]
    pos += 3

    x = x_ref[...]
    mod = mod_ref[0]
    h = (_rms(x) * g_ref[...]) * (1.0 + mod[4:5]) + mod[3:4]
    qkv = _dot(h.astype(BF16), w_ref[...])
    dq = qkv[:, 0:DQ_DIFF]
    dk = qkv[:, DQ_DIFF:2 * DQ_DIFF]
    dv = qkv[:, 2 * DQ_DIFF:3 * DQ_DIFF]
    n0 = 3 * DQ_DIFF
    raw = qkv[:, n0:n0 + D_NORMED]
    gv = qkv[:, n0 + D_NORMED:]

    sq = raw * raw
    hi = sq.astype(BF16)
    lo = (sq - hi.astype(F32)).astype(BF16)
    seg = seg_ref[...]
    ss = jnp.concatenate(
        [_dot(jnp.concatenate([hi[:, c:c + LANES], lo[:, c:c + LANES]], axis=1), seg)
         for c in range(0, D_NORMED, LANES)], axis=1)
    normed = (raw * lax.rsqrt(ss * (1.0 / DH) + EPS)) * qkn_ref[...]
    gq = normed[:, :DQ_GQA]
    gk = normed[:, DQ_GQA:]

    if rope:
        cos = cos_ref[...]
        sin = sin_ref[...]
        lane = lax.broadcasted_iota(jnp.int32, (x.shape[0], LANES), 1)
        first = (lane % 32) < 16
        first4 = jnp.concatenate([first] * 4, axis=1)
        first2 = jnp.concatenate([first] * 2, axis=1)
        dq = _rope(dq, cos, sin, first4)
        dk = _rope(dk, cos, sin, first4)
        gq = _rope(gq, cos, sin, first4)
        gk = _rope(gk, cos, sin, first2)

    q_ref[...] = jnp.concatenate([dq * SCORE_SCALE, gq * SCORE_SCALE], axis=1).astype(BF16)
    k_ref[...] = jnp.concatenate([dk, gk], axis=1).astype(BF16)
    v_ref[...] = jnp.concatenate([dv, gv], axis=1).astype(BF16)

    if emit_f32:
        dk32_ref, dv32_ref, gk32_ref, gv32_ref = refs[pos:pos + 4]
        lane = lax.broadcasted_iota(jnp.int32, (x.shape[0], LANES), 1)
        low = lane < DH
        dk32_ref[...] = dk
        dv32_ref[...] = dv
        gk32_ref[...] = jnp.where(low, gk[:, :LANES], gk[:, LANES:])
        gv32_ref[...] = jnp.where(low, gv[:, :LANES], gv[:, LANES:])


def _proj_call(x, mod, row_map, gain, w_ext, seg, qkn, *, rope_tables, emit_f32, name):
    t = x.shape[0]
    tm = PROJ_TILE
    row_spec = lambda w: pl.BlockSpec((tm, w), lambda i: (i, 0))
    n_ext = w_ext.shape[1]
    in_specs = [row_spec(D_MODEL), pl.BlockSpec((1, N_MOD, D_MODEL), row_map),
                _resident((1, D_MODEL)), _resident((D_MODEL, n_ext)),
                _resident((2 * LANES, LANES)), _resident((1, D_NORMED))]
    args = [x, mod, gain, w_ext, seg, qkn]
    if rope_tables is not None:
        cos, sin = rope_tables
        tiles_per_seq = cos.shape[0] // tm
        tab_spec = pl.BlockSpec((tm, LANES), lambda i: (i % tiles_per_seq, 0))
        in_specs += [tab_spec, tab_spec]
        args += [cos, sin]
    out_specs = [row_spec(D_MIX), row_spec(D_KV_ALL), row_spec(D_KV_ALL)]
    out_shape = [jax.ShapeDtypeStruct((t, D_MIX), BF16),
                 jax.ShapeDtypeStruct((t, D_KV_ALL), BF16),
                 jax.ShapeDtypeStruct((t, D_KV_ALL), BF16)]
    if emit_f32:
        out_specs += [row_spec(DQ_DIFF), row_spec(DQ_DIFF), row_spec(DKV_GQA), row_spec(DKV_GQA)]
        out_shape += [jax.ShapeDtypeStruct((t, DQ_DIFF), F32), jax.ShapeDtypeStruct((t, DQ_DIFF), F32),
                      jax.ShapeDtypeStruct((t, DKV_GQA), F32), jax.ShapeDtypeStruct((t, DKV_GQA), F32)]
    return pl.pallas_call(
        functools.partial(_proj_kernel, rope=rope_tables is not None, emit_f32=emit_f32),
        grid=(t // tm,),
        in_specs=in_specs,
        out_specs=out_specs,
        out_shape=out_shape,
        compiler_params=pltpu.CompilerParams(vmem_limit_bytes=VMEM_LIMIT_BYTES),
        name=name,
    )(*args)


def _attn_kernel(*refs, items, has_cache, n_out):
    refs = list(refs)
    q_ref, kn_ref, vn_ref = refs[:3]
    pos = 3
    segs = [(kn_ref, vn_ref)]
    if has_cache:
        segs.append((refs[pos], refs[pos + 1]))
        pos += 2
    lamv_ref, subln_ref = refs[pos:pos + 2]
    pos += 2
    out_refs = refs[pos:pos + n_out]

    tq = q_ref.shape[1]
    lane = lax.broadcasted_iota(jnp.int32, (tq, LANES), 1)
    low = lane < DH

    def softmax_pv(qm, kvc):
        cs = slice(kvc * LANES, (kvc + 1) * LANES)
        scores = [_dot_nt(qm, k_ref[0, :, cs]) for k_ref, _ in segs]
        m = functools.reduce(jnp.maximum, [s.max(axis=-1, keepdims=True) for s in scores])
        acc = None
        tot = None
        for s, (_, v_ref) in zip(scores, segs):
            e = jnp.exp(s - m)
            part = e.sum(axis=-1, keepdims=True)
            pv = _dot(e.astype(BF16), v_ref[0, :, cs])
            acc = pv if acc is None else acc + pv
            tot = part if tot is None else tot + part
        return acc * (1.0 / tot)

    lamv = lamv_ref[...]
    lam = (jnp.exp(jnp.sum(lamv[0:1] * lamv[1:2], axis=-1, keepdims=True))
           - jnp.exp(jnp.sum(lamv[2:3] * lamv[3:4], axis=-1, keepdims=True)) + LAMBDA_INIT)

    for is_diff, qc, kvc, oi, oc in items:
        q = q_ref[0, :, qc * LANES:(qc + 1) * LANES]
        zero = jnp.zeros_like(q)
        o_a = softmax_pv(jnp.where(low, q, zero), kvc)
        o_b = softmax_pv(jnp.where(low, zero, q), kvc)
        if is_diff:
            o = o_a - lam * o_b
            o = (_rms(o) * subln_ref[...]) * (1.0 - LAMBDA_INIT)
        else:
            o = jnp.where(low, o_a, o_b)
        out_refs[oi][0, :, oc * LANES:(oc + 1) * LANES] = o.astype(BF16)


def _attn_sample_call(q, k, v, kc, vc, lamv, subln, *, is_diff, name):
    b, t, _ = q.shape
    tc = kc.shape[1]
    tq = ATTN_TILE_Q
    ncol = DQ_DIFF // LANES
    if is_diff:
        qcol = lambda j: j
        kvcol = lambda j: j
    else:
        qcol = lambda j: ncol + j
        kvcol = lambda j: ncol + j // (LANES // DH)
    in_specs = [
        pl.BlockSpec((1, tq, LANES), lambda bi, j, i: (bi, i, qcol(j))),
        pl.BlockSpec((1, t, LANES), lambda bi, j, i: (bi, 0, kvcol(j))),
        pl.BlockSpec((1, t, LANES), lambda bi, j, i: (bi, 0, kvcol(j))),
        pl.BlockSpec((1, tc, LANES), lambda bi, j, i: (bi, 0, kvcol(j))),
        pl.BlockSpec((1, tc, LANES), lambda bi, j, i: (bi, 0, kvcol(j))),
        pl.BlockSpec((4, DH), lambda bi, j, i: (0, 0)),
        pl.BlockSpec((1, LANES), lambda bi, j, i: (0, 0)),
    ]
    return pl.pallas_call(
        functools.partial(_attn_kernel, items=((is_diff, 0, 0, 0, 0),), has_cache=True, n_out=1),
        grid=(b, ncol, t // tq),
        in_specs=in_specs,
        out_specs=pl.BlockSpec((1, tq, LANES), lambda bi, j, i: (bi, i, j)),
        out_shape=jax.ShapeDtypeStruct((b, t, ncol * LANES), BF16),
        compiler_params=pltpu.CompilerParams(vmem_limit_bytes=VMEM_LIMIT_BYTES),
        name=name,
    )(q, k, v, kc, vc, lamv, subln)


def _attn_ctx_call(q, k, v, lamv, subln, *, name):
    b, t, _ = q.shape
    ncol = DQ_DIFF // LANES
    items = tuple((True, j, j, 0, j) for j in range(ncol)) + tuple(
        (False, ncol + j, ncol + j // (LANES // DH), 1, j) for j in range(ncol))
    whole = lambda w: pl.BlockSpec((1, t, w), lambda bi: (bi, 0, 0))
    return pl.pallas_call(
        functools.partial(_attn_kernel, items=items, has_cache=False, n_out=2),
        grid=(b,),
        in_specs=[whole(D_MIX), whole(D_KV_ALL), whole(D_KV_ALL),
                  pl.BlockSpec((4, DH), lambda bi: (0, 0)),
                  pl.BlockSpec((1, LANES), lambda bi: (0, 0))],
        out_specs=[whole(DQ_DIFF), whole(DQ_GQA)],
        out_shape=[jax.ShapeDtypeStruct((b, t, DQ_DIFF), BF16),
                   jax.ShapeDtypeStruct((b, t, DQ_GQA), BF16)],
        name=name,
    )(q, k, v, lamv, subln)


def _rope_tables(n_tokens):
    t = jnp.arange(n_tokens)
    row = t // GRID_W
    col = t % GRID_W
    half = DH // 2
    inv = ROPE_THETA ** (-jnp.arange(0, half, 2, dtype=F32) / half)
    ang_r = row[:, None] * inv
    ang_c = col[:, None] * inv
    cos = jnp.concatenate([jnp.cos(ang_r)] * 2 + [jnp.cos(ang_c)] * 2, axis=1)
    sin = jnp.concatenate([-jnp.sin(ang_r), jnp.sin(ang_r), -jnp.sin(ang_c), jnp.sin(ang_c)], axis=1)
    reps = LANES // DH
    return jnp.concatenate([cos] * reps, axis=1), jnp.concatenate([sin] * reps, axis=1)


def _dup_heads(a):
    parts = []
    for n in range(H_KV):
        head = a[..., n * DH:(n + 1) * DH]
        parts += [head, head]
    return jnp.concatenate(parts, axis=-1)


def kernel(x_prompt, x_sample, c, cache_diff_k, cache_diff_v, cache_gqa_k, cache_gqa_v, c_ctx, w_ada, b_ada, norm_ff1, w_ff1_gu, w_ff1_down, norm_mix, w_in, q_norm, k_norm, lambda_q1, lambda_k1, lambda_q2, lambda_k2, subln, w_out, norm_ff2, w_ff2_gu, w_ff2_down, final_norm):
    assert w_ada.shape[0] == 1, "single trunk layer"
    bc, tc_, _ = x_prompt.shape
    bs, ts, _ = x_sample.shape
    tpast = cache_diff_k.shape[2]
    assert bs + 1 <= MOD_ROWS
    ctx_row = bs

    cvecs = jnp.concatenate([c, c_ctx[None], jnp.zeros((MOD_ROWS - bs - 1, D_MODEL), F32)], axis=0)
    wgu1 = w_ff1_gu[0].astype(BF16)
    wd1 = w_ff1_down[0].astype(BF16)
    wgu2 = w_ff2_gu[0].astype(BF16)
    wd2 = w_ff2_down[0].astype(BF16)
    wout = w_out[0].astype(BF16)
    wi = w_in[0]
    n_qkvq = 3 * DQ_DIFF + DQ_GQA
    w_ext = jnp.concatenate(
        [wi[:, :n_qkvq], _dup_heads(wi[:, n_qkvq:n_qkvq + DKV_GQA]), _dup_heads(wi[:, n_qkvq + DKV_GQA:])],
        axis=1).astype(BF16)
    seg128 = (jnp.arange(LANES)[:, None] // DH == jnp.arange(LANES)[None, :] // DH).astype(BF16)
    seg = jnp.concatenate([seg128, seg128], axis=0)
    qkn = jnp.concatenate([jnp.tile(q_norm[0], H_GQA), jnp.tile(k_norm[0], 2 * H_KV)])[None]
    lamv = jnp.stack([lambda_q1[0], lambda_k1[0], lambda_q2[0], lambda_k2[0]])
    g1 = norm_ff1
    gm = norm_mix
    g2 = norm_ff2
    fg = final_norm[None]
    cos, sin = _rope_tables(ts)
    kcache = jnp.concatenate([cache_diff_k[:, 0].reshape(bs, tpast, DQ_DIFF),
                              _dup_heads(cache_gqa_k[:, 0].reshape(bs, tpast, DKV_GQA))], axis=-1).astype(BF16)
    vcache = jnp.concatenate([cache_diff_v[:, 0].reshape(bs, tpast, DQ_DIFF),
                              _dup_heads(cache_gqa_v[:, 0].reshape(bs, tpast, DKV_GQA))], axis=-1).astype(BF16)

    mod = _mod_call(cvecs, w_ada[0], b_ada).reshape(MOD_ROWS, N_MOD, D_MODEL)

    ctx_map = lambda i: (ctx_row, 0, 0)
    xc = x_prompt.reshape(bc * tc_, D_MODEL)
    xc = _ffn_call(xc, mod, ctx_map, g1, wgu1, wd1, sub=0, name="ffn1_ctx")
    qc, kc, vc, dk32, dv32, gk32, gv32 = _proj_call(
        xc, mod, ctx_map, gm, w_ext, seg, qkn, rope_tables=None, emit_f32=True, name="proj_ctx")
    odc, ogc = _attn_ctx_call(qc.reshape(bc, tc_, D_MIX), kc.reshape(bc, tc_, D_KV_ALL),
                              vc.reshape(bc, tc_, D_KV_ALL), lamv, subln, name="attn_ctx")
    yc = _ffn_call(xc, mod, ctx_map, g2, wgu2, wd2, sub=2,
                   pre=(odc.reshape(bc * tc_, DQ_DIFF), ogc.reshape(bc * tc_, DQ_GQA), wout),
                   final_gain=fg, name="ffn2_ctx")

    ffn_tiles = ts // FFN_TILE
    proj_tiles = ts // PROJ_TILE
    xs = x_sample.reshape(bs * ts, D_MODEL)
    xs = _ffn_call(xs, mod, lambda i: (i // ffn_tiles, 0, 0), g1, wgu1, wd1, sub=0, name="ffn1_smp")
    qs, ks, vs = _proj_call(xs, mod, lambda i: (i // proj_tiles, 0, 0), gm, w_ext, seg, qkn,
                            rope_tables=(cos, sin), emit_f32=False, name="proj_smp")
    qs = qs.reshape(bs, ts, D_MIX)
    ks = ks.reshape(bs, ts, D_KV_ALL)
    vs = vs.reshape(bs, ts, D_KV_ALL)
    ods = _attn_sample_call(qs, ks, vs, kcache, vcache, lamv, subln, is_diff=True, name="attn_diff_smp")
    ogs = _attn_sample_call(qs, ks, vs, kcache, vcache, lamv, subln, is_diff=False, name="attn_gqa_smp")
    ys = _ffn_call(xs, mod, lambda i: (i // ffn_tiles, 0, 0), g2, wgu2, wd2, sub=2,
                   pre=(ods.reshape(bs * ts, DQ_DIFF), ogs.reshape(bs * ts, DQ_GQA), wout),
                   final_gain=fg, name="ffn2_smp")

    return (yc.reshape(bc, tc_, D_MODEL),
            ys.reshape(bs, ts, D_MODEL),
            dk32.reshape(bc, 1, tc_, H_DIFF, 2, DH),
            dv32.reshape(bc, 1, tc_, H_DIFF, 2 * DH),
            gk32.reshape(bc, 1, tc_, H_KV, DH),
            gv32.reshape(bc, 1, tc_, H_KV, DH))
```

```python
import functools
import math

import jax
import jax.numpy as jnp
from jax import lax
from jax.experimental import pallas as pl
from jax.experimental.pallas import tpu as pltpu

F32 = jnp.float32
BF16 = jnp.bfloat16

D_MODEL = 1024
N_MOD = 9
D_FF = 2816
H_DIFF = 4
DH = 64
H_GQA = 8
H_KV = 2
G_GQA = H_GQA // H_KV
GRID_W = 64
ROPE_THETA = 10000.0
EPS = 1e-6
LAMBDA_INIT = 0.8 - 0.6 * math.exp(-0.3 * 0)
DQ_DIFF = H_DIFF * 2 * DH
DQ_GQA = H_GQA * DH
DKV_GQA = H_KV * DH
D_IN = 3 * DQ_DIFF + DQ_GQA + 2 * DKV_GQA
D_MIX = DQ_DIFF + DQ_GQA
D_KV_ALL = DQ_DIFF + 2 * DKV_GQA
D_NORMED = DQ_GQA + 2 * DKV_GQA
SCORE_SCALE = DH ** -0.5 * math.log2(math.e)

LANES = 128
SUBLANES = 8
VMEM_LIMIT_BYTES = 60000 * 1024

MOD_ROWS = 8
MOD_TILE_N = 1152
FFN_TILE = 256
PROJ_TILE = 512
ATTN_TILE_Q = 256
ATTN_CHUNK_K = 1024
ATTN_COLS_PER_STEP = 2
ATTN_MAPS_PER_GROUP = 2


def _sigmoid(x):
    return 1.0 / (1.0 + jnp.exp(-x))


def _rms(x):
    return x * lax.rsqrt(jnp.mean(x * x, axis=-1, keepdims=True) + EPS)


def _dot(a, b):
    return jnp.dot(a, b, preferred_element_type=F32)


def _dot_nt(a, b):
    return lax.dot_general(a, b, (((1,), (1,)), ((), ())), preferred_element_type=F32)


def _resident(shape):
    return pl.BlockSpec(shape, lambda *_: (0,) * len(shape), pipeline_mode=pl.Buffered(1))


def _mod_kernel(c_ref, w_ref, b_ref, o_ref):
    c = c_ref[...]
    s = c * _sigmoid(c)
    o_ref[...] = _dot(s.astype(BF16), w_ref[...].astype(BF16)) + b_ref[...]


def _mod_call(cvecs, w_ada, b_ada):
    n = w_ada.shape[1]
    return pl.pallas_call(
        _mod_kernel,
        grid=(n // MOD_TILE_N,),
        in_specs=[
            pl.BlockSpec((MOD_ROWS, D_MODEL), lambda j: (0, 0)),
            pl.BlockSpec((D_MODEL, MOD_TILE_N), lambda j: (0, j)),
            pl.BlockSpec((1, MOD_TILE_N), lambda j: (0, j)),
        ],
        out_specs=pl.BlockSpec((MOD_ROWS, MOD_TILE_N), lambda j: (0, j)),
        out_shape=jax.ShapeDtypeStruct((MOD_ROWS, n), F32),
        name="mod",
    )(cvecs, w_ada, b_ada)


def _ffn_kernel(*refs, sub, pre, final):
    refs = list(refs)
    x_ref, mod_ref = refs[:2]
    pos = 2
    if pre:
        od_ref, og_ref, wout_ref = refs[pos:pos + 3]
        pos += 3
    g_ref, wgu_ref, wd_ref = refs[pos:pos + 3]
    pos += 3
    if final:
        fg_ref = refs[pos]
        pos += 1
    o_ref = refs[pos]

    x = x_ref[...]
    mod = mod_ref[0]
    if pre:
        o = jnp.concatenate([od_ref[...], og_ref[...]], axis=1)
        x = x + mod[5:6] * _dot(o, wout_ref[...])
    shift = mod[3 * sub:3 * sub + 1]
    scale = mod[3 * sub + 1:3 * sub + 2]
    gate = mod[3 * sub + 2:3 * sub + 3]
    h = (_rms(x) * g_ref[...]) * (1.0 + scale) + shift
    gu = _dot(h.astype(BF16), wgu_ref[...])
    g = gu[:, :D_FF]
    u = gu[:, D_FF:]
    act = (g * _sigmoid(g)) * u
    y = _dot(act.astype(BF16), wd_ref[...])
    x = x + (0.5 * gate) * y
    if final:
        x = _rms(x) * fg_ref[...]
    o_ref[...] = x


def _ffn_call(x, mod, row_map, gain, wgu, wd, *, sub, pre=None, final_gain=None, name):
    t = x.shape[0]
    tm = FFN_TILE
    row_spec = lambda w: pl.BlockSpec((tm, w), lambda i: (i, 0))
    in_specs = [row_spec(D_MODEL), pl.BlockSpec((1, N_MOD, D_MODEL), row_map)]
    args = [x, mod]
    if pre is not None:
        od, og, wout = pre
        in_specs += [row_spec(DQ_DIFF), row_spec(DQ_GQA), _resident((D_MIX, D_MODEL))]
        args += [od, og, wout]
    in_specs += [_resident((1, D_MODEL)), _resident((D_MODEL, 2 * D_FF)), _resident((D_FF, D_MODEL))]
    args += [gain, wgu, wd]
    if final_gain is not None:
        in_specs.append(_resident((1, D_MODEL)))
        args.append(final_gain)
    return pl.pallas_call(
        functools.partial(_ffn_kernel, sub=sub, pre=pre is not None, final=final_gain is not None),
        grid=(t // tm,),
        in_specs=in_specs,
        out_specs=row_spec(D_MODEL),
        out_shape=jax.ShapeDtypeStruct((t, D_MODEL), F32),
        compiler_params=pltpu.CompilerParams(vmem_limit_bytes=VMEM_LIMIT_BYTES),
        name=name,
    )(*args)


def _rope(x, cos, sin, first_of_pair):
    w = x.shape[1]
    partner = jnp.where(first_of_pair, pltpu.roll(x, w - 16, 1), pltpu.roll(x, 16, 1))
    reps = w // LANES
    cos_w = jnp.concatenate([cos] * reps, axis=1) if reps > 1 else cos
    sin_w = jnp.concatenate([sin] * reps, axis=1) if reps > 1 else sin
    return x * cos_w + partner * sin_w


def _proj_kernel(*refs, rope, emit_f32):
    refs = list(refs)
    x_ref, mod_ref, g_ref, w_ref, seg_ref, qkn_ref = refs[:6]
    pos = 6
    if rope:
        cos_ref, sin_ref = refs[pos:pos + 2]
        pos += 2
    q_ref, k_ref, vt_ref = refs[pos:pos + 3]
    pos += 3

    x = x_ref[...]
    mod = mod_ref[0]
    h = (_rms(x) * g_ref[...]) * (1.0 + mod[4:5]) + mod[3:4]
    qkv = _dot(h.astype(BF16), w_ref[...])
    dq = qkv[:, 0:DQ_DIFF]
    dk = qkv[:, DQ_DIFF:2 * DQ_DIFF]
    dv = qkv[:, 2 * DQ_DIFF:3 * DQ_DIFF]
    n0 = 3 * DQ_DIFF
    raw = qkv[:, n0:n0 + D_NORMED]
    gv = qkv[:, n0 + D_NORMED:]

    sq = raw * raw
    hi = sq.astype(BF16)
    lo = (sq - hi.astype(F32)).astype(BF16)
    seg = seg_ref[...]
    ss = jnp.concatenate(
        [_dot(jnp.concatenate([hi[:, c:c + LANES], lo[:, c:c + LANES]], axis=1), seg)
         for c in range(0, D_NORMED, LANES)], axis=1)
    normed = (raw * lax.rsqrt(ss * (1.0 / DH) + EPS)) * qkn_ref[...]
    gq = normed[:, :DQ_GQA]
    gk = normed[:, DQ_GQA:]

    if rope:
        cos = cos_ref[...]
        sin = sin_ref[...]
        lane = lax.broadcasted_iota(jnp.int32, (x.shape[0], LANES), 1)
        first = (lane % 32) < 16
        first4 = jnp.concatenate([first] * 4, axis=1)
        first2 = jnp.concatenate([first] * 2, axis=1)
        dq = _rope(dq, cos, sin, first4)
        dk = _rope(dk, cos, sin, first4)
        gq = _rope(gq, cos, sin, first4)
        gk = _rope(gk, cos, sin, first2)

    q_ref[...] = jnp.concatenate([dq * SCORE_SCALE, gq * SCORE_SCALE], axis=1).astype(BF16)
    k_ref[...] = jnp.concatenate([dk, gk], axis=1).astype(BF16)
    vt_ref[...] = jnp.concatenate([dv, gv], axis=1).T.astype(BF16)

    if emit_f32:
        dk32_ref, dv32_ref, gk32_ref, gv32_ref = refs[pos:pos + 4]
        lane = lax.broadcasted_iota(jnp.int32, (x.shape[0], LANES), 1)
        low = lane < DH
        dk32_ref[...] = dk
        dv32_ref[...] = dv
        gk32_ref[...] = jnp.where(low, gk[:, :LANES], gk[:, LANES:])
        gv32_ref[...] = jnp.where(low, gv[:, :LANES], gv[:, LANES:])


def _proj_call(x, mod, row_map, gain, w_ext, seg, qkn, *, rope_tables, emit_f32, name):
    t = x.shape[0]
    tm = PROJ_TILE
    row_spec = lambda w: pl.BlockSpec((tm, w), lambda i: (i, 0))
    n_ext = w_ext.shape[1]
    in_specs = [row_spec(D_MODEL), pl.BlockSpec((1, N_MOD, D_MODEL), row_map),
                _resident((1, D_MODEL)), _resident((D_MODEL, n_ext)),
                _resident((2 * LANES, LANES)), _resident((1, D_NORMED))]
    args = [x, mod, gain, w_ext, seg, qkn]
    if rope_tables is not None:
        cos, sin = rope_tables
        tiles_per_seq = cos.shape[0] // tm
        tab_spec = pl.BlockSpec((tm, LANES), lambda i: (i % tiles_per_seq, 0))
        in_specs += [tab_spec, tab_spec]
        args += [cos, sin]
    out_specs = [row_spec(D_MIX), row_spec(D_KV_ALL), pl.BlockSpec((D_KV_ALL, tm), lambda i: (0, i))]
    out_shape = [jax.ShapeDtypeStruct((t, D_MIX), BF16),
                 jax.ShapeDtypeStruct((t, D_KV_ALL), BF16),
                 jax.ShapeDtypeStruct((D_KV_ALL, t), BF16)]
    if emit_f32:
        out_specs += [row_spec(DQ_DIFF), row_spec(DQ_DIFF), row_spec(DKV_GQA), row_spec(DKV_GQA)]
        out_shape += [jax.ShapeDtypeStruct((t, DQ_DIFF), F32), jax.ShapeDtypeStruct((t, DQ_DIFF), F32),
                      jax.ShapeDtypeStruct((t, DKV_GQA), F32), jax.ShapeDtypeStruct((t, DKV_GQA), F32)]
    return pl.pallas_call(
        functools.partial(_proj_kernel, rope=rope_tables is not None, emit_f32=emit_f32),
        grid=(t // tm,),
        in_specs=in_specs,
        out_specs=out_specs,
        out_shape=out_shape,
        compiler_params=pltpu.CompilerParams(vmem_limit_bytes=VMEM_LIMIT_BYTES),
        name=name,
    )(*args)


def _attn_kernel(*refs, items, has_cache, n_out):
    refs = list(refs)
    q_ref, kn_ref, vtn_ref = refs[:3]
    pos = 3
    segs = [(kn_ref, vtn_ref)]
    if has_cache:
        segs.append((refs[pos], refs[pos + 1]))
        pos += 2
    lamv_ref, subln_ref = refs[pos:pos + 2]
    pos += 2
    out_refs = refs[pos:pos + n_out]

    tq = q_ref.shape[1]
    low = lax.broadcasted_iota(jnp.int32, (tq, LANES), 1) < DH
    top = lax.broadcasted_iota(jnp.int32, (LANES, tq), 0) < DH

    chunks = []
    for si, (k_ref, _) in enumerate(segs):
        n = k_ref.shape[1]
        chunks += [(si, c0, min(n, c0 + ATTN_CHUNK_K)) for c0 in range(0, n, ATTN_CHUNK_K)]

    def fold_rows(x, op):
        return op(x.reshape(x.shape[0] // SUBLANES, SUBLANES, x.shape[1]), axis=0)

    class SoftmaxMap:
        def __init__(self, qm, kvc):
            self.qm = qm
            self.cs = slice(kvc * LANES, (kvc + 1) * LANES)
            self.scores = []
            self.mpart = None
            self.acc = None
            self.lpart = None

        def score_chunk(self, i):
            si, c0, c1 = chunks[i]
            s = _dot_nt(segs[si][0][0, c0:c1, self.cs], self.qm)
            part = fold_rows(s, jnp.max)
            self.mpart = part if self.mpart is None else jnp.maximum(self.mpart, part)
            self.scores.append(s)

        def finish_scores(self):
            self.m = self.mpart.max(axis=0, keepdims=True)

        def pv_chunk(self, i):
            si, c0, c1 = chunks[i]
            e = jnp.exp2(self.scores[i] - self.m)
            part = fold_rows(e, jnp.sum)
            pv = _dot(segs[si][1][self.cs, c0:c1], e.astype(BF16))
            self.lpart = part if self.lpart is None else self.lpart + part
            self.acc = pv if self.acc is None else self.acc + pv

        def result(self):
            return self.acc * (1.0 / self.lpart.sum(axis=0, keepdims=True))

    maps = []
    for _, qc, kvc, _, _ in items:
        q = q_ref[0, :, qc * LANES:(qc + 1) * LANES]
        zero = jnp.zeros_like(q)
        maps += [SoftmaxMap(jnp.where(low, q, zero), kvc), SoftmaxMap(jnp.where(low, zero, q), kvc)]

    groups = [maps[g:g + ATTN_MAPS_PER_GROUP] for g in range(0, len(maps), ATTN_MAPS_PER_GROUP)]
    for stage in range(len(groups) + 1):
        for i in range(len(chunks)):
            if stage > 0:
                for mp in groups[stage - 1]:
                    mp.pv_chunk(i)
            if stage < len(groups):
                for mp in groups[stage]:
                    mp.score_chunk(i)
        if stage < len(groups):
            for mp in groups[stage]:
                mp.finish_scores()

    lamv = lamv_ref[...]
    lam = (jnp.exp(jnp.sum(lamv[0:1] * lamv[1:2], axis=-1, keepdims=True))
           - jnp.exp(jnp.sum(lamv[2:3] * lamv[3:4], axis=-1, keepdims=True)) + LAMBDA_INIT)

    for n, (is_diff, _, _, oi, oc) in enumerate(items):
        ot_a = maps[2 * n].result()
        ot_b = maps[2 * n + 1].result()
        if is_diff:
            o = (ot_a - lam * ot_b).T
            o = (_rms(o) * subln_ref[...]) * (1.0 - LAMBDA_INIT)
        else:
            o = jnp.where(top, ot_a, ot_b).T
        out_refs[oi][0, :, oc * LANES:(oc + 1) * LANES] = o.astype(BF16)


def _attn_loop_kernel(q_ref, kn_ref, vtn_ref, kc_ref, vtc_ref, lamv_ref, subln_ref, out_ref, s_ref, m_ref,
                      *, is_diff, kv_cols):
    tq = ATTN_TILE_Q
    n_tiles = q_ref.shape[1] // tq
    n_groups = len(kv_cols)
    t_new = kn_ref.shape[1]
    segs = [(kn_ref, vtn_ref, 0), (kc_ref, vtc_ref, t_new)]
    chunks = []
    for si, (k_ref, _, base) in enumerate(segs):
        n = k_ref.shape[1]
        chunks += [(si, c0, min(n, c0 + ATTN_CHUNK_K), base + c0) for c0 in range(0, n, ATTN_CHUNK_K)]

    low = lax.broadcasted_iota(jnp.int32, (tq, LANES), 1) < DH
    top = lax.broadcasted_iota(jnp.int32, (LANES, tq), 0) < DH
    lamv = lamv_ref[...]
    lam = (jnp.exp(jnp.sum(lamv[0:1] * lamv[1:2], axis=-1, keepdims=True))
           - jnp.exp(jnp.sum(lamv[2:3] * lamv[3:4], axis=-1, keepdims=True)) + LAMBDA_INIT)

    def fold_rows(x, op):
        return op(x.reshape(x.shape[0] // SUBLANES, SUBLANES, x.shape[1]), axis=0)

    def rows(tile):
        return pl.ds(pl.multiple_of(tile * tq, tq), tq)

    def stage(b_g, a_g, a_tile):
        if a_g is not None:
            q = q_ref[0, rows(a_tile), a_g * LANES:(a_g + 1) * LANES]
            zero = jnp.zeros_like(q)
            qms = [jnp.where(low, q, zero), jnp.where(low, zero, q)]
            a_cs = slice(kv_cols[a_g] * LANES, (kv_cols[a_g] + 1) * LANES)
            mparts = [None, None]
        if b_g is not None:
            ms = [m_ref[2 * b_g + mi] for mi in range(2)]
            b_cs = slice(kv_cols[b_g] * LANES, (kv_cols[b_g] + 1) * LANES)
            accs = [None, None]
            lparts = [None, None]
        for si, c0, c1, r0 in chunks:
            if b_g is not None:
                for mi in range(2):
                    e = jnp.exp2(s_ref[2 * b_g + mi, r0:r0 + c1 - c0, :] - ms[mi])
                    part = fold_rows(e, jnp.sum)
                    pv = _dot(segs[si][1][b_cs, c0:c1], e.astype(BF16))
                    lparts[mi] = part if lparts[mi] is None else lparts[mi] + part
                    accs[mi] = pv if accs[mi] is None else accs[mi] + pv
            if a_g is not None:
                for mi in range(2):
                    s = _dot_nt(segs[si][0][0, c0:c1, a_cs], qms[mi])
                    s_ref[2 * a_g + mi, r0:r0 + c1 - c0, :] = s
                    part = fold_rows(s, jnp.max)
                    mparts[mi] = part if mparts[mi] is None else jnp.maximum(mparts[mi], part)
        if a_g is not None:
            for mi in range(2):
                m_ref[2 * a_g + mi] = mparts[mi].max(axis=0, keepdims=True)
        if b_g is not None:
            return [accs[mi] * (1.0 / lparts[mi].sum(axis=0, keepdims=True)) for mi in range(2)]
        return None

    def emit(g, tile, ot_a, ot_b):
        if is_diff:
            o = (ot_a - lam * ot_b).T
            o = (_rms(o) * subln_ref[...]) * (1.0 - LAMBDA_INIT)
        else:
            o = jnp.where(top, ot_a, ot_b).T
        out_ref[0, rows(tile), g * LANES:(g + 1) * LANES] = o.astype(BF16)

    stage(None, 0, 0)

    def body(t, carry):
        for g in range(n_groups):
            if g + 1 < n_groups:
                ot_a, ot_b = stage(g, g + 1, t)
            else:
                ot_a, ot_b = stage(g, 0, jnp.minimum(t + 1, n_tiles - 1))
            emit(g, t, ot_a, ot_b)
        return carry

    lax.fori_loop(0, n_tiles, body, 0)


def _attn_sample_call(q, k, vt, kc, vtc, lamv, subln, *, is_diff, name):
    b, t, _ = q.shape
    tc = kc.shape[1]
    qw = ATTN_COLS_PER_STEP * LANES
    nblk = DQ_DIFF // qw
    if is_diff:
        kw = qw
        q0 = 0
        kv0 = 0
        kv_cols = tuple(range(ATTN_COLS_PER_STEP))
    else:
        assert ATTN_COLS_PER_STEP * (LANES // DH) == G_GQA
        kw = LANES
        q0 = DQ_DIFF // qw
        kv0 = DQ_DIFF // kw
        kv_cols = (0,) * ATTN_COLS_PER_STEP
    n_maps = 2 * ATTN_COLS_PER_STEP
    in_specs = [
        pl.BlockSpec((1, t, qw), lambda bi, j: (bi, 0, q0 + j)),
        pl.BlockSpec((1, t, kw), lambda bi, j: (bi, 0, kv0 + j)),
        pl.BlockSpec((kw, t), lambda bi, j: (kv0 + j, bi)),
        pl.BlockSpec((1, tc, kw), lambda bi, j: (bi, 0, kv0 + j)),
        pl.BlockSpec((kw, tc), lambda bi, j: (kv0 + j, bi)),
        pl.BlockSpec((4, DH), lambda bi, j: (0, 0)),
        pl.BlockSpec((1, LANES), lambda bi, j: (0, 0)),
    ]
    return pl.pallas_call(
        functools.partial(_attn_loop_kernel, is_diff=is_diff, kv_cols=kv_cols),
        grid=(b, nblk),
        in_specs=in_specs,
        out_specs=pl.BlockSpec((1, t, qw), lambda bi, j: (bi, 0, j)),
        out_shape=jax.ShapeDtypeStruct((b, t, nblk * qw), BF16),
        scratch_shapes=[pltpu.VMEM((n_maps, t + tc, ATTN_TILE_Q), F32),
                        pltpu.VMEM((n_maps, 1, ATTN_TILE_Q), F32)],
        compiler_params=pltpu.CompilerParams(vmem_limit_bytes=VMEM_LIMIT_BYTES),
        name=name,
    )(q, k, vt, kc, vtc, lamv, subln)


def _attn_ctx_call(q, k, vt, lamv, subln, *, name):
    b, t, _ = q.shape
    ncol = DQ_DIFF // LANES
    items = tuple((True, j, j, 0, j) for j in range(ncol)) + tuple(
        (False, ncol + j, ncol + j // (LANES // DH), 1, j) for j in range(ncol))
    whole = lambda w: pl.BlockSpec((1, t, w), lambda bi: (bi, 0, 0))
    return pl.pallas_call(
        functools.partial(_attn_kernel, items=items, has_cache=False, n_out=2),
        grid=(b,),
        in_specs=[whole(D_MIX), whole(D_KV_ALL),
                  pl.BlockSpec((D_KV_ALL, t), lambda bi: (0, bi)),
                  pl.BlockSpec((4, DH), lambda bi: (0, 0)),
                  pl.BlockSpec((1, LANES), lambda bi: (0, 0))],
        out_specs=[whole(DQ_DIFF), whole(DQ_GQA)],
        out_shape=[jax.ShapeDtypeStruct((b, t, DQ_DIFF), BF16),
                   jax.ShapeDtypeStruct((b, t, DQ_GQA), BF16)],
        compiler_params=pltpu.CompilerParams(vmem_limit_bytes=VMEM_LIMIT_BYTES),
        name=name,
    )(q, k, vt, lamv, subln)


def _rope_tables(n_tokens):
    t = jnp.arange(n_tokens)
    row = t // GRID_W
    col = t % GRID_W
    half = DH // 2
    inv = ROPE_THETA ** (-jnp.arange(0, half, 2, dtype=F32) / half)
    ang_r = row[:, None] * inv
    ang_c = col[:, None] * inv
    cos = jnp.concatenate([jnp.cos(ang_r)] * 2 + [jnp.cos(ang_c)] * 2, axis=1)
    sin = jnp.concatenate([-jnp.sin(ang_r), jnp.sin(ang_r), -jnp.sin(ang_c), jnp.sin(ang_c)], axis=1)
    reps = LANES // DH
    return jnp.concatenate([cos] * reps, axis=1), jnp.concatenate([sin] * reps, axis=1)


def _dup_heads(a):
    parts = []
    for n in range(H_KV):
        head = a[..., n * DH:(n + 1) * DH]
        parts += [head, head]
    return jnp.concatenate(parts, axis=-1)


def kernel(x_prompt, x_sample, c, cache_diff_k, cache_diff_v, cache_gqa_k, cache_gqa_v, c_ctx, w_ada, b_ada, norm_ff1, w_ff1_gu, w_ff1_down, norm_mix, w_in, q_norm, k_norm, lambda_q1, lambda_k1, lambda_q2, lambda_k2, subln, w_out, norm_ff2, w_ff2_gu, w_ff2_down, final_norm):
    assert w_ada.shape[0] == 1, "single trunk layer"
    bc, tc_, _ = x_prompt.shape
    bs, ts, _ = x_sample.shape
    tpast = cache_diff_k.shape[2]
    assert bs + 1 <= MOD_ROWS
    ctx_row = bs

    cvecs = jnp.concatenate([c, c_ctx[None], jnp.zeros((MOD_ROWS - bs - 1, D_MODEL), F32)], axis=0)
    wgu1 = w_ff1_gu[0].astype(BF16)
    wd1 = w_ff1_down[0].astype(BF16)
    wgu2 = w_ff2_gu[0].astype(BF16)
    wd2 = w_ff2_down[0].astype(BF16)
    wout = w_out[0].astype(BF16)
    wi = w_in[0]
    n_qkvq = 3 * DQ_DIFF + DQ_GQA
    w_ext = jnp.concatenate(
        [wi[:, :n_qkvq], _dup_heads(wi[:, n_qkvq:n_qkvq + DKV_GQA]), _dup_heads(wi[:, n_qkvq + DKV_GQA:])],
        axis=1).astype(BF16)
    seg128 = (jnp.arange(LANES)[:, None] // DH == jnp.arange(LANES)[None, :] // DH).astype(BF16)
    seg = jnp.concatenate([seg128, seg128], axis=0)
    qkn = jnp.concatenate([jnp.tile(q_norm[0], H_GQA), jnp.tile(k_norm[0], 2 * H_KV)])[None]
    lamv = jnp.stack([lambda_q1[0], lambda_k1[0], lambda_q2[0], lambda_k2[0]])
    g1 = norm_ff1
    gm = norm_mix
    g2 = norm_ff2
    fg = final_norm[None]
    cos, sin = _rope_tables(ts)
    kcache = jnp.concatenate([cache_diff_k[:, 0].reshape(bs, tpast, DQ_DIFF),
                              _dup_heads(cache_gqa_k[:, 0].reshape(bs, tpast, DKV_GQA))], axis=-1).astype(BF16)
    vcache = jnp.concatenate([cache_diff_v[:, 0].reshape(bs, tpast, DQ_DIFF),
                              _dup_heads(cache_gqa_v[:, 0].reshape(bs, tpast, DKV_GQA))], axis=-1).astype(BF16)
    vcache_t = vcache.transpose(2, 0, 1).reshape(D_KV_ALL, bs * tpast)

    mod = _mod_call(cvecs, w_ada[0], b_ada).reshape(MOD_ROWS, N_MOD, D_MODEL)

    ctx_map = lambda i: (ctx_row, 0, 0)
    xc = x_prompt.reshape(bc * tc_, D_MODEL)
    xc = _ffn_call(xc, mod, ctx_map, g1, wgu1, wd1, sub=0, name="ffn1_ctx")
    qc, kc, vtc, dk32, dv32, gk32, gv32 = _proj_call(
        xc, mod, ctx_map, gm, w_ext, seg, qkn, rope_tables=None, emit_f32=True, name="proj_ctx")
    odc, ogc = _attn_ctx_call(qc.reshape(bc, tc_, D_MIX), kc.reshape(bc, tc_, D_KV_ALL), vtc,
                              lamv, subln, name="attn_ctx")
    yc = _ffn_call(xc, mod, ctx_map, g2, wgu2, wd2, sub=2,
                   pre=(odc.reshape(bc * tc_, DQ_DIFF), ogc.reshape(bc * tc_, DQ_GQA), wout),
                   final_gain=fg, name="ffn2_ctx")

    ffn_tiles = ts // FFN_TILE
    proj_tiles = ts // PROJ_TILE
    xs = x_sample.reshape(bs * ts, D_MODEL)
    xs = _ffn_call(xs, mod, lambda i: (i // ffn_tiles, 0, 0), g1, wgu1, wd1, sub=0, name="ffn1_smp")
    qs, ks, vts = _proj_call(xs, mod, lambda i: (i // proj_tiles, 0, 0), gm, w_ext, seg, qkn,
                             rope_tables=(cos, sin), emit_f32=False, name="proj_smp")
    qs = qs.reshape(bs, ts, D_MIX)
    ks = ks.reshape(bs, ts, D_KV_ALL)
    ods = _attn_sample_call(qs, ks, vts, kcache, vcache_t, lamv, subln, is_diff=True, name="attn_diff_smp")
    ogs = _attn_sample_call(qs, ks, vts, kcache, vcache_t, lamv, subln, is_diff=False, name="attn_gqa_smp")
    ys = _ffn_call(xs, mod, lambda i: (i // ffn_tiles, 0, 0), g2, wgu2, wd2, sub=2,
                   pre=(ods.reshape(bs * ts, DQ_DIFF), ogs.reshape(bs * ts, DQ_GQA), wout),
                   final_gain=fg, name="ffn2_smp")

    return (yc.reshape(bc, tc_, D_MODEL),
            ys.reshape(bs, ts, D_MODEL),
            dk32.reshape(bc, 1, tc_, H_DIFF, 2, DH),
            dv32.reshape(bc, 1, tc_, H_DIFF, 2 * DH),
            gk32.reshape(bc, 1, tc_, H_KV, DH),
            gv32.reshape(bc, 1, tc_, H_KV, DH))
```

```python
import functools
import math

import jax
import jax.numpy as jnp
import numpy as np
from jax import lax
from jax.experimental import pallas as pl
from jax.experimental.pallas import tpu as pltpu

F32 = jnp.float32
BF16 = jnp.bfloat16

D_MODEL = 1024
N_MOD = 9
D_FF = 2816
H_DIFF = 4
DH = 64
H_GQA = 8
H_KV = 2
G_GQA = H_GQA // H_KV
GRID_W = 64
ROPE_THETA = 10000.0
EPS = 1e-6
LAMBDA_INIT = 0.8 - 0.6 * math.exp(-0.3 * 0)
DQ_DIFF = H_DIFF * 2 * DH
DQ_GQA = H_GQA * DH
DKV_GQA = H_KV * DH
D_IN = 3 * DQ_DIFF + DQ_GQA + 2 * DKV_GQA
D_MIX = DQ_DIFF + DQ_GQA
D_KV_ALL = DQ_DIFF + 2 * DKV_GQA
D_NORMED = DQ_GQA + 2 * DKV_GQA
SCORE_SCALE = DH ** -0.5 * math.log2(math.e)

LANES = 128
SUBLANES = 8
VMEM_LIMIT_BYTES = 60000 * 1024

MOD_ROWS = 8
MOD_TILE_N = 1152
FFN_TILE = 512
PROJ_TILE = 512
ATTN_TILE_Q = 256
ATTN_CHUNK_K = 512
ATTN_COLS_PER_STEP = 2
ATTN_CTX_MAPS_PER_GROUP = 8


def _sigmoid(x):
    return 1.0 / (1.0 + jnp.exp(-x))


def _rms(x):
    return x * lax.rsqrt(jnp.mean(x * x, axis=-1, keepdims=True) + EPS)


def _dot(a, b):
    return jnp.dot(a, b, preferred_element_type=F32)


def _dot_nt(a, b):
    return lax.dot_general(a, b, (((1,), (1,)), ((), ())), preferred_element_type=F32)


def _resident(shape):
    return pl.BlockSpec(shape, lambda *_: (0,) * len(shape), pipeline_mode=pl.Buffered(1))


def _mod_kernel(c_ref, w_ref, b_ref, o_ref):
    c = c_ref[...]
    s = c * _sigmoid(c)
    o_ref[...] = _dot(s.astype(BF16), w_ref[...].astype(BF16)) + b_ref[...]


def _mod_call(cvecs, w_ada, b_ada):
    n = w_ada.shape[1]
    return pl.pallas_call(
        _mod_kernel,
        grid=(n // MOD_TILE_N,),
        in_specs=[
            pl.BlockSpec((MOD_ROWS, D_MODEL), lambda j: (0, 0)),
            pl.BlockSpec((D_MODEL, MOD_TILE_N), lambda j: (0, j)),
            pl.BlockSpec((1, MOD_TILE_N), lambda j: (0, j)),
        ],
        out_specs=pl.BlockSpec((MOD_ROWS, MOD_TILE_N), lambda j: (0, j)),
        out_shape=jax.ShapeDtypeStruct((MOD_ROWS, n), F32),
        name="mod",
    )(cvecs, w_ada, b_ada)


def _ffn_kernel(*refs, sub, pre, final):
    refs = list(refs)
    x_ref, mod_ref = refs[:2]
    pos = 2
    if pre:
        od_ref, og_ref, wout_ref = refs[pos:pos + 3]
        pos += 3
    g_ref, wgu_ref, wd_ref = refs[pos:pos + 3]
    pos += 3
    if final:
        fg_ref = refs[pos]
        pos += 1
    o_ref = refs[pos]

    x = x_ref[...]
    mod = mod_ref[0]
    if pre:
        o = jnp.concatenate([od_ref[...], og_ref[...]], axis=1)
        x = x + mod[5:6] * _dot(o, wout_ref[...])
    shift = mod[3 * sub:3 * sub + 1]
    scale = mod[3 * sub + 1:3 * sub + 2]
    gate = mod[3 * sub + 2:3 * sub + 3]
    h = (_rms(x) * g_ref[...]) * (1.0 + scale) + shift
    gu = _dot(h.astype(BF16), wgu_ref[...])
    g = gu[:, :D_FF]
    u = gu[:, D_FF:]
    act = (g * _sigmoid(g)) * u
    y = _dot(act.astype(BF16), wd_ref[...])
    x = x + (0.5 * gate) * y
    if final:
        x = _rms(x) * fg_ref[...]
    o_ref[...] = x


def _ffn_call(x, mod, row_map, gain, wgu, wd, *, sub, pre=None, final_gain=None, name):
    t = x.shape[0]
    tm = FFN_TILE
    row_spec = lambda w: pl.BlockSpec((tm, w), lambda i: (i, 0))
    in_specs = [row_spec(D_MODEL), pl.BlockSpec((1, N_MOD, D_MODEL), row_map)]
    args = [x, mod]
    if pre is not None:
        od, og, wout = pre
        in_specs += [row_spec(DQ_DIFF), row_spec(DQ_GQA), _resident((D_MIX, D_MODEL))]
        args += [od, og, wout]
    in_specs += [_resident((1, D_MODEL)), _resident((D_MODEL, 2 * D_FF)), _resident((D_FF, D_MODEL))]
    args += [gain, wgu, wd]
    if final_gain is not None:
        in_specs.append(_resident((1, D_MODEL)))
        args.append(final_gain)
    return pl.pallas_call(
        functools.partial(_ffn_kernel, sub=sub, pre=pre is not None, final=final_gain is not None),
        grid=(t // tm,),
        in_specs=in_specs,
        out_specs=row_spec(D_MODEL),
        out_shape=jax.ShapeDtypeStruct((t, D_MODEL), F32),
        compiler_params=pltpu.CompilerParams(vmem_limit_bytes=VMEM_LIMIT_BYTES),
        name=name,
    )(*args)


def _rope(x, cos, sin, first_of_pair):
    w = x.shape[1]
    partner = jnp.where(first_of_pair, pltpu.roll(x, w - 16, 1), pltpu.roll(x, 16, 1))
    reps = w // LANES
    cos_w = jnp.concatenate([cos] * reps, axis=1) if reps > 1 else cos
    sin_w = jnp.concatenate([sin] * reps, axis=1) if reps > 1 else sin
    return x * cos_w + partner * sin_w


def _proj_kernel(*refs, rope, emit_f32):
    refs = list(refs)
    x_ref, mod_ref, g_ref, w_ref, seg_ref, qkn_ref = refs[:6]
    pos = 6
    if rope:
        cos_ref, sin_ref = refs[pos:pos + 2]
        pos += 2
    q_ref, k_ref, vt_ref = refs[pos:pos + 3]
    pos += 3

    x = x_ref[...]
    mod = mod_ref[0]
    h = (_rms(x) * g_ref[...]) * (1.0 + mod[4:5]) + mod[3:4]
    qkv = _dot(h.astype(BF16), w_ref[...])
    dq = qkv[:, 0:DQ_DIFF]
    dk = qkv[:, DQ_DIFF:2 * DQ_DIFF]
    dv = qkv[:, 2 * DQ_DIFF:3 * DQ_DIFF]
    n0 = 3 * DQ_DIFF
    raw = qkv[:, n0:n0 + D_NORMED]
    gv = qkv[:, n0 + D_NORMED:]

    sq = raw * raw
    hi = sq.astype(BF16)
    lo = (sq - hi.astype(F32)).astype(BF16)
    seg = seg_ref[...]
    ss = jnp.concatenate(
        [_dot(jnp.concatenate([hi[:, c:c + LANES], lo[:, c:c + LANES]], axis=1), seg)
         for c in range(0, D_NORMED, LANES)], axis=1)
    normed = (raw * lax.rsqrt(ss * (1.0 / DH) + EPS)) * qkn_ref[...]
    gq = normed[:, :DQ_GQA]
    gk = normed[:, DQ_GQA:]

    if rope:
        cos = cos_ref[...]
        sin = sin_ref[...]
        lane = lax.broadcasted_iota(jnp.int32, (x.shape[0], LANES), 1)
        first = (lane % 32) < 16
        first4 = jnp.concatenate([first] * 4, axis=1)
        first2 = jnp.concatenate([first] * 2, axis=1)
        dq = _rope(dq, cos, sin, first4)
        dk = _rope(dk, cos, sin, first4)
        gq = _rope(gq, cos, sin, first4)
        gk = _rope(gk, cos, sin, first2)

    q_ref[...] = jnp.concatenate([dq * SCORE_SCALE, gq * SCORE_SCALE], axis=1).astype(BF16)
    k_ref[...] = jnp.concatenate([dk, gk], axis=1).astype(BF16)
    vt_ref[...] = jnp.concatenate([dv, gv], axis=1).T.astype(BF16)

    if emit_f32:
        dk32_ref, dv32_ref, gk32_ref, gv32_ref = refs[pos:pos + 4]
        lane = lax.broadcasted_iota(jnp.int32, (x.shape[0], LANES), 1)
        low = lane < DH
        dk32_ref[...] = dk
        dv32_ref[...] = dv
        gk32_ref[...] = jnp.where(low, gk[:, :LANES], gk[:, LANES:])
        gv32_ref[...] = jnp.where(low, gv[:, :LANES], gv[:, LANES:])


def _proj_call(x, mod, row_map, gain, w_ext, seg, qkn, *, rope_tables, emit_f32, name):
    t = x.shape[0]
    tm = PROJ_TILE
    row_spec = lambda w: pl.BlockSpec((tm, w), lambda i: (i, 0))
    n_ext = w_ext.shape[1]
    in_specs = [row_spec(D_MODEL), pl.BlockSpec((1, N_MOD, D_MODEL), row_map),
                _resident((1, D_MODEL)), _resident((D_MODEL, n_ext)),
                _resident((2 * LANES, LANES)), _resident((1, D_NORMED))]
    args = [x, mod, gain, w_ext, seg, qkn]
    if rope_tables is not None:
        cos, sin = rope_tables
        tiles_per_seq = cos.shape[0] // tm
        tab_spec = pl.BlockSpec((tm, LANES), lambda i: (i % tiles_per_seq, 0))
        in_specs += [tab_spec, tab_spec]
        args += [cos, sin]
    out_specs = [row_spec(D_MIX), row_spec(D_KV_ALL), pl.BlockSpec((D_KV_ALL, tm), lambda i: (0, i))]
    out_shape = [jax.ShapeDtypeStruct((t, D_MIX), BF16),
                 jax.ShapeDtypeStruct((t, D_KV_ALL), BF16),
                 jax.ShapeDtypeStruct((D_KV_ALL, t), BF16)]
    if emit_f32:
        out_specs += [row_spec(DQ_DIFF), row_spec(DQ_DIFF), row_spec(DKV_GQA), row_spec(DKV_GQA)]
        out_shape += [jax.ShapeDtypeStruct((t, DQ_DIFF), F32), jax.ShapeDtypeStruct((t, DQ_DIFF), F32),
                      jax.ShapeDtypeStruct((t, DKV_GQA), F32), jax.ShapeDtypeStruct((t, DKV_GQA), F32)]
    return pl.pallas_call(
        functools.partial(_proj_kernel, rope=rope_tables is not None, emit_f32=emit_f32),
        grid=(t // tm,),
        in_specs=in_specs,
        out_specs=out_specs,
        out_shape=out_shape,
        compiler_params=pltpu.CompilerParams(vmem_limit_bytes=VMEM_LIMIT_BYTES),
        name=name,
    )(*args)


def _lambda(lamv_ref):
    lamv = lamv_ref[...]
    return (jnp.exp(jnp.sum(lamv[0:1] * lamv[1:2], axis=-1, keepdims=True))
            - jnp.exp(jnp.sum(lamv[2:3] * lamv[3:4], axis=-1, keepdims=True)) + LAMBDA_INIT)


def _fold_rows(x, op):
    return op(x.reshape(x.shape[0] // SUBLANES, SUBLANES, x.shape[1]), axis=0)


def _value_rows(is_diff, kvc):
    return slice(kvc * LANES, kvc * LANES + (LANES if is_diff else DH))


def _combine_heads(is_diff, ot_a, ot_b, lam, subln):
    if is_diff:
        o = (ot_a - lam * ot_b).T
        return (_rms(o) * subln) * (1.0 - LAMBDA_INIT)
    return jnp.concatenate([ot_a, ot_b], axis=0).T


def _attn_kernel(q_ref, k_ref, vt_ref, lamv_ref, subln_ref, od_ref, og_ref, *, items, maps_per_group):
    tq = q_ref.shape[1]
    low = lax.broadcasted_iota(jnp.int32, (tq, LANES), 1) < DH

    class SoftmaxMap:
        def __init__(self, qm, is_diff, kvc):
            self.qm = qm
            self.cs = slice(kvc * LANES, (kvc + 1) * LANES)
            self.vrows = _value_rows(is_diff, kvc)

        def score_pass(self):
            self.s = _dot_nt(k_ref[0, :, self.cs], self.qm)
            self.m = _fold_rows(self.s, jnp.max).max(axis=0, keepdims=True)

        def pv_pass(self):
            e = jnp.exp2(self.s - self.m)
            tot = _fold_rows(e, jnp.sum).sum(axis=0, keepdims=True)
            self.ot = _dot(vt_ref[self.vrows, :], e.astype(BF16)) * (1.0 / tot)

    maps = []
    for is_diff, qc, kvc, _ in items:
        q = q_ref[0, :, qc * LANES:(qc + 1) * LANES]
        zero = jnp.zeros_like(q)
        maps += [SoftmaxMap(jnp.where(low, q, zero), is_diff, kvc),
                 SoftmaxMap(jnp.where(low, zero, q), is_diff, kvc)]

    groups = [maps[g:g + maps_per_group] for g in range(0, len(maps), maps_per_group)]
    for stage in range(len(groups) + 1):
        if stage > 0:
            for mp in groups[stage - 1]:
                mp.pv_pass()
        if stage < len(groups):
            for mp in groups[stage]:
                mp.score_pass()

    lam = _lambda(lamv_ref)
    for n, (is_diff, _, _, oc) in enumerate(items):
        o = _combine_heads(is_diff, maps[2 * n].ot, maps[2 * n + 1].ot, lam, subln_ref[...])
        out_ref = od_ref if is_diff else og_ref
        out_ref[0, :, oc * LANES:(oc + 1) * LANES] = o.astype(BF16)


def _attn_loop_kernel(q_ref, kn_ref, vtn_ref, kc_ref, vtc_ref, lamv_ref, subln_ref, out_ref, *scratch,
                      is_diff, kv_cols):
    tq = ATTN_TILE_Q
    n_tiles = q_ref.shape[1] // tq
    n_groups = len(kv_cols)
    s_refs = scratch[:n_groups]
    m_refs = scratch[n_groups:]
    t_new = kn_ref.shape[1]
    segs = [(kn_ref, vtn_ref, 0), (kc_ref, vtc_ref, t_new)]
    chunks = []
    for si, (k_ref, _, base) in enumerate(segs):
        n = k_ref.shape[1]
        chunks += [(si, c0, min(n, c0 + ATTN_CHUNK_K), base + c0) for c0 in range(0, n, ATTN_CHUNK_K)]

    low = lax.broadcasted_iota(jnp.int32, (tq, LANES), 1) < DH
    lam = _lambda(lamv_ref)

    def rows(tile):
        return pl.ds(pl.multiple_of(tile * tq, tq), tq)

    def stage(b_g, a_g, a_tile):
        if a_g is not None:
            q = q_ref[0, rows(a_tile), a_g * LANES:(a_g + 1) * LANES]
            zero = jnp.zeros_like(q)
            qm = jnp.concatenate([jnp.where(low, q, zero), jnp.where(low, zero, q)], axis=0)
            a_cs = slice(kv_cols[a_g] * LANES, (kv_cols[a_g] + 1) * LANES)
            mpart = None
        if b_g is not None:
            m = m_refs[b_g][...]
            b_rows = _value_rows(is_diff, kv_cols[b_g])
            acc = None
            lpart = None
        for si, c0, c1, r0 in chunks:
            if a_g is not None:
                s = _dot_nt(segs[si][0][0, c0:c1, a_cs], qm)
                s_refs[a_g][r0:r0 + c1 - c0, :] = s
                part = _fold_rows(s, jnp.max)
                mpart = part if mpart is None else jnp.maximum(mpart, part)
            if b_g is not None:
                e = jnp.exp2(s_refs[b_g][r0:r0 + c1 - c0, :] - m)
                part = _fold_rows(e, jnp.sum)
                pv = _dot(segs[si][1][b_rows, c0:c1], e.astype(BF16))
                lpart = part if lpart is None else lpart + part
                acc = pv if acc is None else acc + pv
        if a_g is not None:
            m_refs[a_g][...] = mpart.max(axis=0, keepdims=True)
        if b_g is not None:
            ot = acc * (1.0 / lpart.sum(axis=0, keepdims=True))
            return ot[:, :tq], ot[:, tq:]
        return None

    def emit(g, tile, ot_a, ot_b):
        o = _combine_heads(is_diff, ot_a, ot_b, lam, subln_ref[...])
        out_ref[0, rows(tile), g * LANES:(g + 1) * LANES] = o.astype(BF16)

    stage(None, 0, 0)

    def body(t, carry):
        for g in range(n_groups):
            if g + 1 < n_groups:
                ot_a, ot_b = stage(g, g + 1, t)
            else:
                ot_a, ot_b = stage(g, 0, jnp.minimum(t + 1, n_tiles - 1))
            emit(g, t, ot_a, ot_b)
        return carry

    lax.fori_loop(0, n_tiles, body, 0)


def _attn_sample_call(q, k, vt, kc, vtc, lamv, subln, *, is_diff, name):
    b, t, _ = q.shape
    tc = kc.shape[1]
    qw = ATTN_COLS_PER_STEP * LANES
    nblk = DQ_DIFF // qw
    if is_diff:
        kw = qw
        q0 = 0
        kv0 = 0
        kv_cols = tuple(range(ATTN_COLS_PER_STEP))
    else:
        assert ATTN_COLS_PER_STEP * (LANES // DH) == G_GQA
        kw = LANES
        q0 = DQ_DIFF // qw
        kv0 = DQ_DIFF // kw
        kv_cols = (0,) * ATTN_COLS_PER_STEP
    in_specs = [
        pl.BlockSpec((1, t, qw), lambda bi, j: (bi, 0, q0 + j)),
        pl.BlockSpec((1, t, kw), lambda bi, j: (bi, 0, kv0 + j)),
        pl.BlockSpec((kw, t), lambda bi, j: (kv0 + j, bi)),
        pl.BlockSpec((1, tc, kw), lambda bi, j: (bi, 0, kv0 + j)),
        pl.BlockSpec((kw, tc), lambda bi, j: (kv0 + j, bi)),
        pl.BlockSpec((4, DH), lambda bi, j: (0, 0)),
        pl.BlockSpec((1, LANES), lambda bi, j: (0, 0)),
    ]
    return pl.pallas_call(
        functools.partial(_attn_loop_kernel, is_diff=is_diff, kv_cols=kv_cols),
        grid=(b, nblk),
        in_specs=in_specs,
        out_specs=pl.BlockSpec((1, t, qw), lambda bi, j: (bi, 0, j)),
        out_shape=jax.ShapeDtypeStruct((b, t, nblk * qw), BF16),
        scratch_shapes=([pltpu.VMEM((t + tc, 2 * ATTN_TILE_Q), F32)] * ATTN_COLS_PER_STEP
                        + [pltpu.VMEM((1, 2 * ATTN_TILE_Q), F32)] * ATTN_COLS_PER_STEP),
        compiler_params=pltpu.CompilerParams(vmem_limit_bytes=VMEM_LIMIT_BYTES),
        name=name,
    )(q, k, vt, kc, vtc, lamv, subln)


def _attn_ctx_call(q, k, vt, lamv, subln, *, name):
    b, t, _ = q.shape
    ncol = DQ_DIFF // LANES
    items = tuple((True, j, j, j) for j in range(ncol)) + tuple(
        (False, ncol + j, ncol + j // (LANES // DH), j) for j in range(ncol))
    whole = lambda w: pl.BlockSpec((1, t, w), lambda bi: (bi, 0, 0))
    return pl.pallas_call(
        functools.partial(_attn_kernel, items=items, maps_per_group=ATTN_CTX_MAPS_PER_GROUP),
        grid=(b,),
        in_specs=[whole(D_MIX), whole(D_KV_ALL),
                  pl.BlockSpec((D_KV_ALL, t), lambda bi: (0, bi)),
                  pl.BlockSpec((4, DH), lambda bi: (0, 0)),
                  pl.BlockSpec((1, LANES), lambda bi: (0, 0))],
        out_specs=[whole(DQ_DIFF), whole(DQ_GQA)],
        out_shape=[jax.ShapeDtypeStruct((b, t, DQ_DIFF), BF16),
                   jax.ShapeDtypeStruct((b, t, DQ_GQA), BF16)],
        compiler_params=pltpu.CompilerParams(vmem_limit_bytes=VMEM_LIMIT_BYTES),
        name=name,
    )(q, k, vt, lamv, subln)


def _rope_tables(n_tokens):
    t = np.arange(n_tokens)
    row = (t // GRID_W).astype(np.float32)
    col = (t % GRID_W).astype(np.float32)
    half = DH // 2
    inv = np.float32(ROPE_THETA) ** (-(np.arange(0, half, 2, dtype=np.float32) / np.float32(half)))
    ang_r = row[:, None] * inv
    ang_c = col[:, None] * inv
    cos = np.concatenate([np.cos(ang_r)] * 2 + [np.cos(ang_c)] * 2, axis=1)
    sin = np.concatenate([-np.sin(ang_r), np.sin(ang_r), -np.sin(ang_c), np.sin(ang_c)], axis=1)
    reps = LANES // DH
    return (jnp.asarray(np.concatenate([cos] * reps, axis=1), F32),
            jnp.asarray(np.concatenate([sin] * reps, axis=1), F32))


def _dup_heads(a):
    parts = []
    for n in range(H_KV):
        head = a[..., n * DH:(n + 1) * DH]
        parts += [head, head]
    return jnp.concatenate(parts, axis=-1)


def kernel(x_prompt, x_sample, c, cache_diff_k, cache_diff_v, cache_gqa_k, cache_gqa_v, c_ctx, w_ada, b_ada, norm_ff1, w_ff1_gu, w_ff1_down, norm_mix, w_in, q_norm, k_norm, lambda_q1, lambda_k1, lambda_q2, lambda_k2, subln, w_out, norm_ff2, w_ff2_gu, w_ff2_down, final_norm):
    assert w_ada.shape[0] == 1, "single trunk layer"
    bc, tc_, _ = x_prompt.shape
    bs, ts, _ = x_sample.shape
    tpast = cache_diff_k.shape[2]
    assert bs + 1 <= MOD_ROWS
    ctx_row = bs

    cvecs = jnp.concatenate([c, c_ctx[None], jnp.zeros((MOD_ROWS - bs - 1, D_MODEL), F32)], axis=0)
    wgu1 = w_ff1_gu[0].astype(BF16)
    wd1 = w_ff1_down[0].astype(BF16)
    wgu2 = w_ff2_gu[0].astype(BF16)
    wd2 = w_ff2_down[0].astype(BF16)
    wout = w_out[0].astype(BF16)
    wi = w_in[0]
    n_qkvq = 3 * DQ_DIFF + DQ_GQA
    w_ext = jnp.concatenate(
        [wi[:, :n_qkvq], _dup_heads(wi[:, n_qkvq:n_qkvq + DKV_GQA]), _dup_heads(wi[:, n_qkvq + DKV_GQA:])],
        axis=1).astype(BF16)
    seg128 = (jnp.arange(LANES)[:, None] // DH == jnp.arange(LANES)[None, :] // DH).astype(BF16)
    seg = jnp.concatenate([seg128, seg128], axis=0)
    qkn = jnp.concatenate([jnp.tile(q_norm[0], H_GQA), jnp.tile(k_norm[0], 2 * H_KV)])[None]
    lamv = jnp.stack([lambda_q1[0], lambda_k1[0], lambda_q2[0], lambda_k2[0]])
    g1 = norm_ff1
    gm = norm_mix
    g2 = norm_ff2
    fg = final_norm[None]
    cos, sin = _rope_tables(ts)
    kcache = jnp.concatenate([cache_diff_k[:, 0].reshape(bs, tpast, DQ_DIFF),
                              _dup_heads(cache_gqa_k[:, 0].reshape(bs, tpast, DKV_GQA))], axis=-1).astype(BF16)
    vcache = jnp.concatenate([cache_diff_v[:, 0].reshape(bs, tpast, DQ_DIFF),
                              _dup_heads(cache_gqa_v[:, 0].reshape(bs, tpast, DKV_GQA))], axis=-1).astype(BF16)
    vcache_t = vcache.transpose(2, 0, 1).reshape(D_KV_ALL, bs * tpast)

    mod = _mod_call(cvecs, w_ada[0], b_ada).reshape(MOD_ROWS, N_MOD, D_MODEL)

    ctx_map = lambda i: (ctx_row, 0, 0)
    xc = x_prompt.reshape(bc * tc_, D_MODEL)
    xc = _ffn_call(xc, mod, ctx_map, g1, wgu1, wd1, sub=0, name="ffn1_ctx")
    qc, kc, vtc, dk32, dv32, gk32, gv32 = _proj_call(
        xc, mod, ctx_map, gm, w_ext, seg, qkn, rope_tables=None, emit_f32=True, name="proj_ctx")
    odc, ogc = _attn_ctx_call(qc.reshape(bc, tc_, D_MIX), kc.reshape(bc, tc_, D_KV_ALL), vtc,
                              lamv, subln, name="attn_ctx")
    yc = _ffn_call(xc, mod, ctx_map, g2, wgu2, wd2, sub=2,
                   pre=(odc.reshape(bc * tc_, DQ_DIFF), ogc.reshape(bc * tc_, DQ_GQA), wout),
                   final_gain=fg, name="ffn2_ctx")

    ffn_tiles = ts // FFN_TILE
    proj_tiles = ts // PROJ_TILE
    xs = x_sample.reshape(bs * ts, D_MODEL)
    xs = _ffn_call(xs, mod, lambda i: (i // ffn_tiles, 0, 0), g1, wgu1, wd1, sub=0, name="ffn1_smp")
    qs, ks, vts = _proj_call(xs, mod, lambda i: (i // proj_tiles, 0, 0), gm, w_ext, seg, qkn,
                             rope_tables=(cos, sin), emit_f32=False, name="proj_smp")
    qs = qs.reshape(bs, ts, D_MIX)
    ks = ks.reshape(bs, ts, D_KV_ALL)
    ods = _attn_sample_call(qs, ks, vts, kcache, vcache_t, lamv, subln, is_diff=True, name="attn_diff_smp")
    ogs = _attn_sample_call(qs, ks, vts, kcache, vcache_t, lamv, subln, is_diff=False, name="attn_gqa_smp")
    ys = _ffn_call(xs, mod, lambda i: (i // ffn_tiles, 0, 0), g2, wgu2, wd2, sub=2,
                   pre=(ods.reshape(bs * ts, DQ_DIFF), ogs.reshape(bs * ts, DQ_GQA), wout),
                   final_gain=fg, name="ffn2_smp")

    return (yc.reshape(bc, tc_, D_MODEL),
            ys.reshape(bs, ts, D_MODEL),
            dk32.reshape(bc, 1, tc_, H_DIFF, 2, DH),
            dv32.reshape(bc, 1, tc_, H_DIFF, 2 * DH),
            gk32.reshape(bc, 1, tc_, H_KV, DH),
            gv32.reshape(bc, 1, tc_, H_KV, DH))
```

```python
import functools
import math

import jax
import jax.numpy as jnp
import numpy as np
from jax import lax
from jax.experimental import pallas as pl
from jax.experimental.pallas import tpu as pltpu

F32 = jnp.float32
BF16 = jnp.bfloat16

D_MODEL = 1024
N_MOD = 9
D_FF = 2816
H_DIFF = 4
DH = 64
H_GQA = 8
H_KV = 2
G_GQA = H_GQA // H_KV
GRID_W = 64
ROPE_THETA = 10000.0
EPS = 1e-6
LAMBDA_INIT = 0.8 - 0.6 * math.exp(-0.3 * 0)
DQ_DIFF = H_DIFF * 2 * DH
DQ_GQA = H_GQA * DH
DKV_GQA = H_KV * DH
D_IN = 3 * DQ_DIFF + DQ_GQA + 2 * DKV_GQA
D_MIX = DQ_DIFF + DQ_GQA
D_KV_ALL = DQ_DIFF + 2 * DKV_GQA
D_NORMED = DQ_GQA + 2 * DKV_GQA
SCORE_SCALE = DH ** -0.5 * math.log2(math.e)

LANES = 128
SUBLANES = 8
VMEM_LIMIT_BYTES = 60000 * 1024

MOD_ROWS = 8
MOD_TILE_N = 1152
FFN_TILE = 512
PROJ_TILE = 512
ATTN_TILE_Q = 256
ATTN_CHUNK_K = 512
ATTN_COLS_PER_STEP = 2
ATTN_CTX_MAPS_PER_GROUP = 8


def _sigmoid(x):
    return 1.0 / (1.0 + jnp.exp(-x))


def _rms(x):
    return x * lax.rsqrt(jnp.mean(x * x, axis=-1, keepdims=True) + EPS)


def _dot(a, b):
    return jnp.dot(a, b, preferred_element_type=F32)


def _dot_nt(a, b):
    return lax.dot_general(a, b, (((1,), (1,)), ((), ())), preferred_element_type=F32)


def _resident(shape):
    return pl.BlockSpec(shape, lambda *_: (0,) * len(shape), pipeline_mode=pl.Buffered(1))


def _mod_kernel(c_ref, w_ref, b_ref, o_ref):
    c = c_ref[...]
    s = c * _sigmoid(c)
    o_ref[...] = _dot(s.astype(BF16), w_ref[...].astype(BF16)) + b_ref[...]


def _mod_call(cvecs, w_ada, b_ada):
    n = w_ada.shape[1]
    return pl.pallas_call(
        _mod_kernel,
        grid=(n // MOD_TILE_N,),
        in_specs=[
            pl.BlockSpec((MOD_ROWS, D_MODEL), lambda j: (0, 0)),
            pl.BlockSpec((D_MODEL, MOD_TILE_N), lambda j: (0, j)),
            pl.BlockSpec((1, MOD_TILE_N), lambda j: (0, j)),
        ],
        out_specs=pl.BlockSpec((MOD_ROWS, MOD_TILE_N), lambda j: (0, j)),
        out_shape=jax.ShapeDtypeStruct((MOD_ROWS, n), F32),
        name="mod",
    )(cvecs, w_ada, b_ada)


def _ffn_kernel(*refs, sub, pre, final):
    refs = list(refs)
    x_ref, mod_ref = refs[:2]
    pos = 2
    if pre:
        od_ref, og_ref, wout_ref = refs[pos:pos + 3]
        pos += 3
    g_ref, wgu_ref, wd_ref = refs[pos:pos + 3]
    pos += 3
    if final:
        fg_ref = refs[pos]
        pos += 1
    o_ref = refs[pos]

    x = x_ref[...]
    mod = mod_ref[0]
    if pre:
        o = jnp.concatenate([od_ref[...], og_ref[...]], axis=1)
        x = x + mod[5:6] * _dot(o, wout_ref[...])
    shift = mod[3 * sub:3 * sub + 1]
    scale = mod[3 * sub + 1:3 * sub + 2]
    gate = mod[3 * sub + 2:3 * sub + 3]
    h = (_rms(x) * g_ref[...]) * (1.0 + scale) + shift
    gu = _dot(h.astype(BF16), wgu_ref[...])
    g = gu[:, :D_FF]
    u = gu[:, D_FF:]
    act = (g * _sigmoid(g)) * u
    y = _dot(act.astype(BF16), wd_ref[...])
    x = x + (0.5 * gate) * y
    if final:
        x = _rms(x) * fg_ref[...]
    o_ref[...] = x


def _ffn_call(x, mod, row_map, gain, wgu, wd, *, sub, pre=None, final_gain=None, name):
    t = x.shape[0]
    tm = FFN_TILE
    row_spec = lambda w: pl.BlockSpec((tm, w), lambda i: (i, 0))
    in_specs = [row_spec(D_MODEL), pl.BlockSpec((1, N_MOD, D_MODEL), row_map)]
    args = [x, mod]
    if pre is not None:
        od, og, wout = pre
        in_specs += [row_spec(DQ_DIFF), row_spec(DQ_GQA), _resident((D_MIX, D_MODEL))]
        args += [od, og, wout]
    in_specs += [_resident((1, D_MODEL)), _resident((D_MODEL, 2 * D_FF)), _resident((D_FF, D_MODEL))]
    args += [gain, wgu, wd]
    if final_gain is not None:
        in_specs.append(_resident((1, D_MODEL)))
        args.append(final_gain)
    return pl.pallas_call(
        functools.partial(_ffn_kernel, sub=sub, pre=pre is not None, final=final_gain is not None),
        grid=(t // tm,),
        in_specs=in_specs,
        out_specs=row_spec(D_MODEL),
        out_shape=jax.ShapeDtypeStruct((t, D_MODEL), F32),
        compiler_params=pltpu.CompilerParams(vmem_limit_bytes=VMEM_LIMIT_BYTES),
        name=name,
    )(*args)


def _rope(x, cos, sin, first_of_pair):
    w = x.shape[1]
    partner = jnp.where(first_of_pair, pltpu.roll(x, w - 16, 1), pltpu.roll(x, 16, 1))
    reps = w // LANES
    cos_w = jnp.concatenate([cos] * reps, axis=1) if reps > 1 else cos
    sin_w = jnp.concatenate([sin] * reps, axis=1) if reps > 1 else sin
    return x * cos_w + partner * sin_w


def _proj_kernel(*refs, rope, emit_f32):
    refs = list(refs)
    x_ref, mod_ref, g_ref, w_ref, seg_ref, qkn_ref = refs[:6]
    pos = 6
    if rope:
        cos_ref, sin_ref = refs[pos:pos + 2]
        pos += 2
    q_ref, k_ref, vt_ref = refs[pos:pos + 3]
    pos += 3

    x = x_ref[...]
    mod = mod_ref[0]
    h = (_rms(x) * g_ref[...]) * (1.0 + mod[4:5]) + mod[3:4]
    qkv = _dot(h.astype(BF16), w_ref[...])
    dq = qkv[:, 0:DQ_DIFF]
    dk = qkv[:, DQ_DIFF:2 * DQ_DIFF]
    dv = qkv[:, 2 * DQ_DIFF:3 * DQ_DIFF]
    n0 = 3 * DQ_DIFF
    raw = qkv[:, n0:n0 + D_NORMED]
    gv = qkv[:, n0 + D_NORMED:]

    sq = raw * raw
    hi = sq.astype(BF16)
    lo = (sq - hi.astype(F32)).astype(BF16)
    seg = seg_ref[...]
    ss = jnp.concatenate(
        [_dot(jnp.concatenate([hi[:, c:c + LANES], lo[:, c:c + LANES]], axis=1), seg)
         for c in range(0, D_NORMED, LANES)], axis=1)
    normed = (raw * lax.rsqrt(ss * (1.0 / DH) + EPS)) * qkn_ref[...]
    gq = normed[:, :DQ_GQA]
    gk = normed[:, DQ_GQA:]

    if rope:
        cos = cos_ref[...]
        sin = sin_ref[...]
        lane = lax.broadcasted_iota(jnp.int32, (x.shape[0], LANES), 1)
        first = (lane % 32) < 16
        first4 = jnp.concatenate([first] * 4, axis=1)
        first2 = jnp.concatenate([first] * 2, axis=1)
        dq = _rope(dq, cos, sin, first4)
        dk = _rope(dk, cos, sin, first4)
        gq = _rope(gq, cos, sin, first4)
        gk = _rope(gk, cos, sin, first2)

    q_ref[...] = jnp.concatenate([dq * SCORE_SCALE, gq * SCORE_SCALE], axis=1).astype(BF16)
    k_ref[...] = jnp.concatenate([dk, gk], axis=1).astype(BF16)
    vt_ref[...] = jnp.concatenate([dv, gv], axis=1).T.astype(BF16)

    if emit_f32:
        dk32_ref, dv32_ref, gk32_ref, gv32_ref = refs[pos:pos + 4]
        lane = lax.broadcasted_iota(jnp.int32, (x.shape[0], LANES), 1)
        low = lane < DH
        dk32_ref[...] = dk
        dv32_ref[...] = dv
        gk32_ref[...] = jnp.where(low, gk[:, :LANES], gk[:, LANES:])
        gv32_ref[...] = jnp.where(low, gv[:, :LANES], gv[:, LANES:])


def _proj_call(x, mod, row_map, gain, w_ext, seg, qkn, *, rope_tables, emit_f32, name):
    t = x.shape[0]
    tm = PROJ_TILE
    row_spec = lambda w: pl.BlockSpec((tm, w), lambda i: (i, 0))
    n_ext = w_ext.shape[1]
    in_specs = [row_spec(D_MODEL), pl.BlockSpec((1, N_MOD, D_MODEL), row_map),
                _resident((1, D_MODEL)), _resident((D_MODEL, n_ext)),
                _resident((2 * LANES, LANES)), _resident((1, D_NORMED))]
    args = [x, mod, gain, w_ext, seg, qkn]
    if rope_tables is not None:
        cos, sin = rope_tables
        tiles_per_seq = cos.shape[0] // tm
        tab_spec = pl.BlockSpec((tm, LANES), lambda i: (i % tiles_per_seq, 0))
        in_specs += [tab_spec, tab_spec]
        args += [cos, sin]
    out_specs = [row_spec(D_MIX), row_spec(D_KV_ALL), pl.BlockSpec((D_KV_ALL, tm), lambda i: (0, i))]
    out_shape = [jax.ShapeDtypeStruct((t, D_MIX), BF16),
                 jax.ShapeDtypeStruct((t, D_KV_ALL), BF16),
                 jax.ShapeDtypeStruct((D_KV_ALL, t), BF16)]
    if emit_f32:
        out_specs += [row_spec(DQ_DIFF), row_spec(DQ_DIFF), row_spec(DKV_GQA), row_spec(DKV_GQA)]
        out_shape += [jax.ShapeDtypeStruct((t, DQ_DIFF), F32), jax.ShapeDtypeStruct((t, DQ_DIFF), F32),
                      jax.ShapeDtypeStruct((t, DKV_GQA), F32), jax.ShapeDtypeStruct((t, DKV_GQA), F32)]
    return pl.pallas_call(
        functools.partial(_proj_kernel, rope=rope_tables is not None, emit_f32=emit_f32),
        grid=(t // tm,),
        in_specs=in_specs,
        out_specs=out_specs,
        out_shape=out_shape,
        compiler_params=pltpu.CompilerParams(vmem_limit_bytes=VMEM_LIMIT_BYTES),
        name=name,
    )(*args)


def _lambda(lamv_ref):
    lamv = lamv_ref[...]
    return (jnp.exp(jnp.sum(lamv[0:1] * lamv[1:2], axis=-1, keepdims=True))
            - jnp.exp(jnp.sum(lamv[2:3] * lamv[3:4], axis=-1, keepdims=True)) + LAMBDA_INIT)


def _fold_rows(x, op):
    return op(x.reshape(x.shape[0] // SUBLANES, SUBLANES, x.shape[1]), axis=0)


def _value_rows(kvc):
    return slice(kvc * LANES, (kvc + 1) * LANES)


def _combine_heads(is_diff, ot_a, ot_b, lam, subln):
    if is_diff:
        o = (ot_a - lam * ot_b).T
        return (_rms(o) * subln) * (1.0 - LAMBDA_INIT)
    return jnp.concatenate([ot_a[:DH], ot_b[DH:]], axis=0).T


def _attn_kernel(q_ref, k_ref, vt_ref, lamv_ref, subln_ref, od_ref, og_ref, *, items, maps_per_group):
    tq = q_ref.shape[1]
    low = lax.broadcasted_iota(jnp.int32, (tq, LANES), 1) < DH

    class SoftmaxMap:
        def __init__(self, qm, is_diff, kvc):
            self.qm = qm
            self.cs = slice(kvc * LANES, (kvc + 1) * LANES)
            self.vrows = _value_rows(kvc)

        def score_pass(self):
            self.s = _dot_nt(k_ref[0, :, self.cs], self.qm)
            self.m = _fold_rows(self.s, jnp.max).max(axis=0, keepdims=True)

        def pv_pass(self):
            e = jnp.exp2(self.s - self.m)
            tot = _fold_rows(e, jnp.sum).sum(axis=0, keepdims=True)
            self.ot = _dot(vt_ref[self.vrows, :], e.astype(BF16)) * (1.0 / tot)

    maps = []
    for is_diff, qc, kvc, _ in items:
        q = q_ref[0, :, qc * LANES:(qc + 1) * LANES]
        zero = jnp.zeros_like(q)
        maps += [SoftmaxMap(jnp.where(low, q, zero), is_diff, kvc),
                 SoftmaxMap(jnp.where(low, zero, q), is_diff, kvc)]

    groups = [maps[g:g + maps_per_group] for g in range(0, len(maps), maps_per_group)]
    for stage in range(len(groups) + 1):
        if stage > 0:
            for mp in groups[stage - 1]:
                mp.pv_pass()
        if stage < len(groups):
            for mp in groups[stage]:
                mp.score_pass()

    lam = _lambda(lamv_ref)
    for n, (is_diff, _, _, oc) in enumerate(items):
        o = _combine_heads(is_diff, maps[2 * n].ot, maps[2 * n + 1].ot, lam, subln_ref[...])
        out_ref = od_ref if is_diff else og_ref
        out_ref[0, :, oc * LANES:(oc + 1) * LANES] = o.astype(BF16)


def _attn_loop_kernel(q_ref, kn_ref, vtn_ref, kc_ref, vtc_ref, lamv_ref, subln_ref, out_ref, *scratch,
                      is_diff, kv_cols):
    tq = ATTN_TILE_Q
    n_tiles = q_ref.shape[1] // tq
    n_groups = len(kv_cols)
    s_refs = scratch[:n_groups]
    m_refs = scratch[n_groups:]
    t_new = kn_ref.shape[1]
    segs = [(kn_ref, vtn_ref, 0), (kc_ref, vtc_ref, t_new)]
    chunks = []
    for si, (k_ref, _, base) in enumerate(segs):
        n = k_ref.shape[1]
        chunks += [(si, c0, min(n, c0 + ATTN_CHUNK_K), base + c0) for c0 in range(0, n, ATTN_CHUNK_K)]

    low = lax.broadcasted_iota(jnp.int32, (tq, LANES), 1) < DH
    lam = _lambda(lamv_ref)

    def rows(tile):
        return pl.ds(pl.multiple_of(tile * tq, tq), tq)

    def stage(b_g, a_g, a_tile):
        if a_g is not None:
            q = q_ref[0, rows(a_tile), a_g * LANES:(a_g + 1) * LANES]
            zero = jnp.zeros_like(q)
            qm = jnp.concatenate([jnp.where(low, q, zero), jnp.where(low, zero, q)], axis=0)
            a_cs = slice(kv_cols[a_g] * LANES, (kv_cols[a_g] + 1) * LANES)
            mpart = None
        if b_g is not None:
            m = m_refs[b_g][...]
            b_rows = _value_rows(kv_cols[b_g])
            acc = None
            lpart = None
        for si, c0, c1, r0 in chunks:
            if a_g is not None:
                s = _dot_nt(segs[si][0][0, c0:c1, a_cs], qm)
                s_refs[a_g][r0:r0 + c1 - c0, :] = s
                part = _fold_rows(s, jnp.max)
                mpart = part if mpart is None else jnp.maximum(mpart, part)
            if b_g is not None:
                e = jnp.exp2(s_refs[b_g][r0:r0 + c1 - c0, :] - m)
                part = _fold_rows(e, jnp.sum)
                pv = _dot(segs[si][1][b_rows, c0:c1], e.astype(BF16))
                lpart = part if lpart is None else lpart + part
                acc = pv if acc is None else acc + pv
        if a_g is not None:
            m_refs[a_g][...] = mpart.max(axis=0, keepdims=True)
        if b_g is not None:
            ot = acc * (1.0 / lpart.sum(axis=0, keepdims=True))
            return ot[:, :tq], ot[:, tq:]
        return None

    def emit(g, tile, ot_a, ot_b):
        o = _combine_heads(is_diff, ot_a, ot_b, lam, subln_ref[...])
        out_ref[0, rows(tile), g * LANES:(g + 1) * LANES] = o.astype(BF16)

    stage(None, 0, 0)

    def body(t, carry):
        for g in range(n_groups):
            if g + 1 < n_groups:
                ot_a, ot_b = stage(g, g + 1, t)
            else:
                ot_a, ot_b = stage(g, 0, jnp.minimum(t + 1, n_tiles - 1))
            emit(g, t, ot_a, ot_b)
        return carry

    lax.fori_loop(0, n_tiles, body, 0)


def _attn_sample_call(q, k, vt, kc, vtc, lamv, subln, *, is_diff, name):
    b, t, _ = q.shape
    tc = kc.shape[1]
    qw = ATTN_COLS_PER_STEP * LANES
    nblk = DQ_DIFF // qw
    if is_diff:
        kw = qw
        q0 = 0
        kv0 = 0
        kv_cols = tuple(range(ATTN_COLS_PER_STEP))
    else:
        assert ATTN_COLS_PER_STEP * (LANES // DH) == G_GQA
        kw = LANES
        q0 = DQ_DIFF // qw
        kv0 = DQ_DIFF // kw
        kv_cols = (0,) * ATTN_COLS_PER_STEP
    in_specs = [
        pl.BlockSpec((1, t, qw), lambda bi, j: (bi, 0, q0 + j)),
        pl.BlockSpec((1, t, kw), lambda bi, j: (bi, 0, kv0 + j)),
        pl.BlockSpec((kw, t), lambda bi, j: (kv0 + j, bi)),
        pl.BlockSpec((1, tc, kw), lambda bi, j: (bi, 0, kv0 + j)),
        pl.BlockSpec((kw, tc), lambda bi, j: (kv0 + j, bi)),
        pl.BlockSpec((4, DH), lambda bi, j: (0, 0)),
        pl.BlockSpec((1, LANES), lambda bi, j: (0, 0)),
    ]
    return pl.pallas_call(
        functools.partial(_attn_loop_kernel, is_diff=is_diff, kv_cols=kv_cols),
        grid=(b, nblk),
        in_specs=in_specs,
        out_specs=pl.BlockSpec((1, t, qw), lambda bi, j: (bi, 0, j)),
        out_shape=jax.ShapeDtypeStruct((b, t, nblk * qw), BF16),
        scratch_shapes=([pltpu.VMEM((t + tc, 2 * ATTN_TILE_Q), F32)] * ATTN_COLS_PER_STEP
                        + [pltpu.VMEM((1, 2 * ATTN_TILE_Q), F32)] * ATTN_COLS_PER_STEP),
        compiler_params=pltpu.CompilerParams(vmem_limit_bytes=VMEM_LIMIT_BYTES),
        name=name,
    )(q, k, vt, kc, vtc, lamv, subln)


def _attn_ctx_call(q, k, vt, lamv, subln, *, name):
    b, t, _ = q.shape
    ncol = DQ_DIFF // LANES
    items = tuple((True, j, j, j) for j in range(ncol)) + tuple(
        (False, ncol + j, ncol + j // (LANES // DH), j) for j in range(ncol))
    whole = lambda w: pl.BlockSpec((1, t, w), lambda bi: (bi, 0, 0))
    return pl.pallas_call(
        functools.partial(_attn_kernel, items=items, maps_per_group=ATTN_CTX_MAPS_PER_GROUP),
        grid=(b,),
        in_specs=[whole(D_MIX), whole(D_KV_ALL),
                  pl.BlockSpec((D_KV_ALL, t), lambda bi: (0, bi)),
                  pl.BlockSpec((4, DH), lambda bi: (0, 0)),
                  pl.BlockSpec((1, LANES), lambda bi: (0, 0))],
        out_specs=[whole(DQ_DIFF), whole(DQ_GQA)],
        out_shape=[jax.ShapeDtypeStruct((b, t, DQ_DIFF), BF16),
                   jax.ShapeDtypeStruct((b, t, DQ_GQA), BF16)],
        compiler_params=pltpu.CompilerParams(vmem_limit_bytes=VMEM_LIMIT_BYTES),
        name=name,
    )(q, k, vt, lamv, subln)


def _rope_tables(n_tokens):
    t = np.arange(n_tokens)
    row = (t // GRID_W).astype(np.float32)
    col = (t % GRID_W).astype(np.float32)
    half = DH // 2
    inv = np.float32(ROPE_THETA) ** (-(np.arange(0, half, 2, dtype=np.float32) / np.float32(half)))
    ang_r = row[:, None] * inv
    ang_c = col[:, None] * inv
    cos = np.concatenate([np.cos(ang_r)] * 2 + [np.cos(ang_c)] * 2, axis=1)
    sin = np.concatenate([-np.sin(ang_r), np.sin(ang_r), -np.sin(ang_c), np.sin(ang_c)], axis=1)
    reps = LANES // DH
    return (jnp.asarray(np.concatenate([cos] * reps, axis=1), F32),
            jnp.asarray(np.concatenate([sin] * reps, axis=1), F32))


def _dup_heads(a):
    parts = []
    for n in range(H_KV):
        head = a[..., n * DH:(n + 1) * DH]
        parts += [head, head]
    return jnp.concatenate(parts, axis=-1)


def kernel(x_prompt, x_sample, c, cache_diff_k, cache_diff_v, cache_gqa_k, cache_gqa_v, c_ctx, w_ada, b_ada, norm_ff1, w_ff1_gu, w_ff1_down, norm_mix, w_in, q_norm, k_norm, lambda_q1, lambda_k1, lambda_q2, lambda_k2, subln, w_out, norm_ff2, w_ff2_gu, w_ff2_down, final_norm):
    assert w_ada.shape[0] == 1, "single trunk layer"
    bc, tc_, _ = x_prompt.shape
    bs, ts, _ = x_sample.shape
    tpast = cache_diff_k.shape[2]
    assert bs + 1 <= MOD_ROWS
    ctx_row = bs

    cvecs = jnp.concatenate([c, c_ctx[None], jnp.zeros((MOD_ROWS - bs - 1, D_MODEL), F32)], axis=0)
    wgu1 = w_ff1_gu[0].astype(BF16)
    wd1 = w_ff1_down[0].astype(BF16)
    wgu2 = w_ff2_gu[0].astype(BF16)
    wd2 = w_ff2_down[0].astype(BF16)
    wout = w_out[0].astype(BF16)
    wi = w_in[0]
    n_qkvq = 3 * DQ_DIFF + DQ_GQA
    w_ext = jnp.concatenate(
        [wi[:, :n_qkvq], _dup_heads(wi[:, n_qkvq:n_qkvq + DKV_GQA]), _dup_heads(wi[:, n_qkvq + DKV_GQA:])],
        axis=1).astype(BF16)
    seg128 = (jnp.arange(LANES)[:, None] // DH == jnp.arange(LANES)[None, :] // DH).astype(BF16)
    seg = jnp.concatenate([seg128, seg128], axis=0)
    qkn = jnp.concatenate([jnp.tile(q_norm[0], H_GQA), jnp.tile(k_norm[0], 2 * H_KV)])[None]
    lamv = jnp.stack([lambda_q1[0], lambda_k1[0], lambda_q2[0], lambda_k2[0]])
    g1 = norm_ff1
    gm = norm_mix
    g2 = norm_ff2
    fg = final_norm[None]
    cos, sin = _rope_tables(ts)
    kcache = jnp.concatenate([cache_diff_k[:, 0].reshape(bs, tpast, DQ_DIFF),
                              _dup_heads(cache_gqa_k[:, 0].reshape(bs, tpast, DKV_GQA))], axis=-1).astype(BF16)
    vcache = jnp.concatenate([cache_diff_v[:, 0].reshape(bs, tpast, DQ_DIFF),
                              _dup_heads(cache_gqa_v[:, 0].reshape(bs, tpast, DKV_GQA))], axis=-1).astype(BF16)
    vcache_t = vcache.transpose(2, 0, 1).reshape(D_KV_ALL, bs * tpast)

    mod = _mod_call(cvecs, w_ada[0], b_ada).reshape(MOD_ROWS, N_MOD, D_MODEL)

    ctx_map = lambda i: (ctx_row, 0, 0)
    xc = x_prompt.reshape(bc * tc_, D_MODEL)
    xc = _ffn_call(xc, mod, ctx_map, g1, wgu1, wd1, sub=0, name="ffn1_ctx")
    qc, kc, vtc, dk32, dv32, gk32, gv32 = _proj_call(
        xc, mod, ctx_map, gm, w_ext, seg, qkn, rope_tables=None, emit_f32=True, name="proj_ctx")
    odc, ogc = _attn_ctx_call(qc.reshape(bc, tc_, D_MIX), kc.reshape(bc, tc_, D_KV_ALL), vtc,
                              lamv, subln, name="attn_ctx")
    yc = _ffn_call(xc, mod, ctx_map, g2, wgu2, wd2, sub=2,
                   pre=(odc.reshape(bc * tc_, DQ_DIFF), ogc.reshape(bc * tc_, DQ_GQA), wout),
                   final_gain=fg, name="ffn2_ctx")

    ffn_tiles = ts // FFN_TILE
    proj_tiles = ts // PROJ_TILE
    xs = x_sample.reshape(bs * ts, D_MODEL)
    xs = _ffn_call(xs, mod, lambda i: (i // ffn_tiles, 0, 0), g1, wgu1, wd1, sub=0, name="ffn1_smp")
    qs, ks, vts = _proj_call(xs, mod, lambda i: (i // proj_tiles, 0, 0), gm, w_ext, seg, qkn,
                             rope_tables=(cos, sin), emit_f32=False, name="proj_smp")
    qs = qs.reshape(bs, ts, D_MIX)
    ks = ks.reshape(bs, ts, D_KV_ALL)
    ods = _attn_sample_call(qs, ks, vts, kcache, vcache_t, lamv, subln, is_diff=True, name="attn_diff_smp")
    ogs = _attn_sample_call(qs, ks, vts, kcache, vcache_t, lamv, subln, is_diff=False, name="attn_gqa_smp")
    ys = _ffn_call(xs, mod, lambda i: (i // ffn_tiles, 0, 0), g2, wgu2, wd2, sub=2,
                   pre=(ods.reshape(bs * ts, DQ_DIFF), ogs.reshape(bs * ts, DQ_GQA), wout),
                   final_gain=fg, name="ffn2_smp")

    return (yc.reshape(bc, tc_, D_MODEL),
            ys.reshape(bs, ts, D_MODEL),
            dk32.reshape(bc, 1, tc_, H_DIFF, 2, DH),
            dv32.reshape(bc, 1, tc_, H_DIFF, 2 * DH),
            gk32.reshape(bc, 1, tc_, H_KV, DH),
            gv32.reshape(bc, 1, tc_, H_KV, DH))
```

```python
import functools
import math

import jax
import jax.numpy as jnp
import numpy as np
from jax import lax
from jax.experimental import pallas as pl
from jax.experimental.pallas import tpu as pltpu

F32 = jnp.float32
BF16 = jnp.bfloat16

D_MODEL = 1024
N_MOD = 9
D_FF = 2816
H_DIFF = 4
DH = 64
H_GQA = 8
H_KV = 2
G_GQA = H_GQA // H_KV
GRID_W = 64
ROPE_THETA = 10000.0
EPS = 1e-6
LAMBDA_INIT = 0.8 - 0.6 * math.exp(-0.3 * 0)
DQ_DIFF = H_DIFF * 2 * DH
DQ_GQA = H_GQA * DH
DKV_GQA = H_KV * DH
D_IN = 3 * DQ_DIFF + DQ_GQA + 2 * DKV_GQA
D_MIX = DQ_DIFF + DQ_GQA
D_KV_ALL = DQ_DIFF + 2 * DKV_GQA
D_NORMED = DQ_GQA + 2 * DKV_GQA
SCORE_SCALE = DH ** -0.5 * math.log2(math.e)

LANES = 128
SUBLANES = 8
VMEM_LIMIT_BYTES = 60000 * 1024

MOD_ROWS = 8
MOD_TILE_N = 1152
FFN_TILE = 512
PROJ_TILE = 512
ATTN_TILE_Q = 256
ATTN_CHUNK_K = 512
ATTN_COLS_PER_STEP = 2
ATTN_CTX_MAPS_PER_GROUP = 8


def _sigmoid(x):
    return 1.0 / (1.0 + jnp.exp(-x))


def _rms(x):
    return x * lax.rsqrt(jnp.mean(x * x, axis=-1, keepdims=True) + EPS)


def _dot(a, b):
    return jnp.dot(a, b, preferred_element_type=F32)


def _dot_nt(a, b):
    return lax.dot_general(a, b, (((1,), (1,)), ((), ())), preferred_element_type=F32)


def _resident(shape):
    return pl.BlockSpec(shape, lambda *_: (0,) * len(shape), pipeline_mode=pl.Buffered(1))


def _mod_kernel(c_ref, w_ref, b_ref, o_ref):
    c = c_ref[...]
    s = c * _sigmoid(c)
    o_ref[...] = _dot(s.astype(BF16), w_ref[...].astype(BF16)) + b_ref[...]


def _mod_call(cvecs, w_ada, b_ada):
    n = w_ada.shape[1]
    return pl.pallas_call(
        _mod_kernel,
        grid=(n // MOD_TILE_N,),
        in_specs=[
            pl.BlockSpec((MOD_ROWS, D_MODEL), lambda j: (0, 0)),
            pl.BlockSpec((D_MODEL, MOD_TILE_N), lambda j: (0, j)),
            pl.BlockSpec((1, MOD_TILE_N), lambda j: (0, j)),
        ],
        out_specs=pl.BlockSpec((MOD_ROWS, MOD_TILE_N), lambda j: (0, j)),
        out_shape=jax.ShapeDtypeStruct((MOD_ROWS, n), F32),
        name="mod",
    )(cvecs, w_ada, b_ada)


def _ffn_kernel(*refs, sub, pre, final):
    refs = list(refs)
    x_ref, mod_ref = refs[:2]
    pos = 2
    if pre:
        od_ref, og_ref, wout_ref = refs[pos:pos + 3]
        pos += 3
    g_ref, wgu_ref, wd_ref = refs[pos:pos + 3]
    pos += 3
    if final:
        fg_ref = refs[pos]
        pos += 1
    o_ref = refs[pos]

    x = x_ref[...]
    mod = mod_ref[0]
    if pre:
        o = jnp.concatenate([od_ref[...], og_ref[...]], axis=1)
        x = x + mod[5:6] * _dot(o, wout_ref[...])
    shift = mod[3 * sub:3 * sub + 1]
    scale = mod[3 * sub + 1:3 * sub + 2]
    gate = mod[3 * sub + 2:3 * sub + 3]
    h = (_rms(x) * g_ref[...]) * (1.0 + scale) + shift
    gu = _dot(h.astype(BF16), wgu_ref[...])
    g = gu[:, :D_FF]
    u = gu[:, D_FF:]
    act = (g * _sigmoid(g)) * u
    y = _dot(act.astype(BF16), wd_ref[...])
    x = x + (0.5 * gate) * y
    if final:
        x = _rms(x) * fg_ref[...]
    o_ref[...] = x


def _ffn_call(x, mod, row_map, gain, wgu, wd, *, sub, pre=None, final_gain=None, name):
    t = x.shape[0]
    tm = FFN_TILE
    row_spec = lambda w: pl.BlockSpec((tm, w), lambda i: (i, 0))
    in_specs = [row_spec(D_MODEL), pl.BlockSpec((1, N_MOD, D_MODEL), row_map)]
    args = [x, mod]
    if pre is not None:
        od, og, wout = pre
        in_specs += [row_spec(DQ_DIFF), row_spec(DQ_GQA), _resident((D_MIX, D_MODEL))]
        args += [od, og, wout]
    in_specs += [_resident((1, D_MODEL)), _resident((D_MODEL, 2 * D_FF)), _resident((D_FF, D_MODEL))]
    args += [gain, wgu, wd]
    if final_gain is not None:
        in_specs.append(_resident((1, D_MODEL)))
        args.append(final_gain)
    return pl.pallas_call(
        functools.partial(_ffn_kernel, sub=sub, pre=pre is not None, final=final_gain is not None),
        grid=(t // tm,),
        in_specs=in_specs,
        out_specs=row_spec(D_MODEL),
        out_shape=jax.ShapeDtypeStruct((t, D_MODEL), F32),
        compiler_params=pltpu.CompilerParams(vmem_limit_bytes=VMEM_LIMIT_BYTES),
        name=name,
    )(*args)


def _rope(x, cos, sin, first_of_pair):
    w = x.shape[1]
    partner = jnp.where(first_of_pair, pltpu.roll(x, w - 16, 1), pltpu.roll(x, 16, 1))
    reps = w // LANES
    cos_w = jnp.concatenate([cos] * reps, axis=1) if reps > 1 else cos
    sin_w = jnp.concatenate([sin] * reps, axis=1) if reps > 1 else sin
    return x * cos_w + partner * sin_w


def _proj_kernel(*refs, rope, emit_f32):
    refs = list(refs)
    x_ref, mod_ref, g_ref, w_ref, seg_ref, qkn_ref = refs[:6]
    pos = 6
    if rope:
        cos_ref, sin_ref = refs[pos:pos + 2]
        pos += 2
    q_ref, k_ref, vt_ref = refs[pos:pos + 3]
    pos += 3

    x = x_ref[...]
    mod = mod_ref[0]
    h = (_rms(x) * g_ref[...]) * (1.0 + mod[4:5]) + mod[3:4]
    qkv = _dot(h.astype(BF16), w_ref[...])
    dq = qkv[:, 0:DQ_DIFF]
    dk = qkv[:, DQ_DIFF:2 * DQ_DIFF]
    dv = qkv[:, 2 * DQ_DIFF:3 * DQ_DIFF]
    n0 = 3 * DQ_DIFF
    raw = qkv[:, n0:n0 + D_NORMED]
    gv = qkv[:, n0 + D_NORMED:]

    sq = raw * raw
    hi = sq.astype(BF16)
    lo = (sq - hi.astype(F32)).astype(BF16)
    seg = seg_ref[...]
    ss = jnp.concatenate(
        [_dot(jnp.concatenate([hi[:, c:c + LANES], lo[:, c:c + LANES]], axis=1), seg)
         for c in range(0, D_NORMED, LANES)], axis=1)
    normed = (raw * lax.rsqrt(ss * (1.0 / DH) + EPS)) * qkn_ref[...]
    gq = normed[:, :DQ_GQA]
    gk = normed[:, DQ_GQA:]

    if rope:
        cos = cos_ref[...]
        sin = sin_ref[...]
        lane = lax.broadcasted_iota(jnp.int32, (x.shape[0], LANES), 1)
        first = (lane % 32) < 16
        first4 = jnp.concatenate([first] * 4, axis=1)
        first2 = jnp.concatenate([first] * 2, axis=1)
        dq = _rope(dq, cos, sin, first4)
        dk = _rope(dk, cos, sin, first4)
        gq = _rope(gq, cos, sin, first4)
        gk = _rope(gk, cos, sin, first2)

    q_ref[...] = jnp.concatenate([dq * SCORE_SCALE, gq * SCORE_SCALE], axis=1).astype(BF16)
    k_ref[...] = jnp.concatenate([dk, gk], axis=1).astype(BF16)
    vt_ref[...] = jnp.concatenate([dv, gv], axis=1).T.astype(BF16)

    if emit_f32:
        dkt_ref, dv4_ref, gkt_ref, gvt_ref = refs[pos:pos + 4]
        tm = x.shape[0]
        seq = dkt_ref.shape[2]
        lane = lax.broadcasted_iota(jnp.int32, (tm, LANES), 1)
        low = lane < DH
        dkt = dk.T
        gkt = jnp.where(low, gk[:, :LANES], gk[:, LANES:]).T
        gvt = jnp.where(low, gv[:, :LANES], gv[:, LANES:]).T
        for b in range(tm // seq):
            dkt_ref[b] = dkt[:, b * seq:(b + 1) * seq]
            gkt_ref[b] = gkt[:, b * seq:(b + 1) * seq]
            gvt_ref[b] = gvt[:, b * seq:(b + 1) * seq]
        for hd in range(H_DIFF):
            dv4_ref[pl.ds(hd, tm, stride=H_DIFF), :] = dv[:, hd * LANES:(hd + 1) * LANES]


def _proj_call(x, mod, row_map, gain, w_ext, seg, qkn, *, rope_tables, cache_seq, name):
    t = x.shape[0]
    emit_f32 = cache_seq is not None
    tm = PROJ_TILE
    row_spec = lambda w: pl.BlockSpec((tm, w), lambda i: (i, 0))
    n_ext = w_ext.shape[1]
    in_specs = [row_spec(D_MODEL), pl.BlockSpec((1, N_MOD, D_MODEL), row_map),
                _resident((1, D_MODEL)), _resident((D_MODEL, n_ext)),
                _resident((2 * LANES, LANES)), _resident((1, D_NORMED))]
    args = [x, mod, gain, w_ext, seg, qkn]
    if rope_tables is not None:
        cos, sin = rope_tables
        tiles_per_seq = cos.shape[0] // tm
        tab_spec = pl.BlockSpec((tm, LANES), lambda i: (i % tiles_per_seq, 0))
        in_specs += [tab_spec, tab_spec]
        args += [cos, sin]
    out_specs = [row_spec(D_MIX), row_spec(D_KV_ALL), pl.BlockSpec((D_KV_ALL, tm), lambda i: (0, i))]
    out_shape = [jax.ShapeDtypeStruct((t, D_MIX), BF16),
                 jax.ShapeDtypeStruct((t, D_KV_ALL), BF16),
                 jax.ShapeDtypeStruct((D_KV_ALL, t), BF16)]
    if emit_f32:
        assert tm % cache_seq == 0
        nb = tm // cache_seq
        slab = lambda rows: pl.BlockSpec((nb, rows, cache_seq), lambda i: (i, 0, 0))
        out_specs += [slab(DQ_DIFF), pl.BlockSpec((tm * H_DIFF, LANES), lambda i: (i, 0)),
                      slab(DKV_GQA), slab(DKV_GQA)]
        out_shape += [jax.ShapeDtypeStruct((t // cache_seq, DQ_DIFF, cache_seq), F32),
                      jax.ShapeDtypeStruct((t * H_DIFF, LANES), F32),
                      jax.ShapeDtypeStruct((t // cache_seq, DKV_GQA, cache_seq), F32),
                      jax.ShapeDtypeStruct((t // cache_seq, DKV_GQA, cache_seq), F32)]
    return pl.pallas_call(
        functools.partial(_proj_kernel, rope=rope_tables is not None, emit_f32=emit_f32),
        grid=(t // tm,),
        in_specs=in_specs,
        out_specs=out_specs,
        out_shape=out_shape,
        compiler_params=pltpu.CompilerParams(vmem_limit_bytes=VMEM_LIMIT_BYTES),
        name=name,
    )(*args)


def _lambda(lamv_ref):
    lamv = lamv_ref[...]
    return (jnp.exp(jnp.sum(lamv[0:1] * lamv[1:2], axis=-1, keepdims=True))
            - jnp.exp(jnp.sum(lamv[2:3] * lamv[3:4], axis=-1, keepdims=True)) + LAMBDA_INIT)


def _fold_rows(x, op):
    return op(x.reshape(x.shape[0] // SUBLANES, SUBLANES, x.shape[1]), axis=0)


def _value_rows(kvc):
    return slice(kvc * LANES, (kvc + 1) * LANES)


def _combine_heads(is_diff, ot_a, ot_b, lam, subln):
    if is_diff:
        o = (ot_a - lam * ot_b).T
        return (_rms(o) * subln) * (1.0 - LAMBDA_INIT)
    return jnp.concatenate([ot_a[:DH], ot_b[DH:]], axis=0).T


def _attn_kernel(q_ref, k_ref, vt_ref, lamv_ref, subln_ref, od_ref, og_ref, *, items, maps_per_group):
    tq = q_ref.shape[1]
    low = lax.broadcasted_iota(jnp.int32, (tq, LANES), 1) < DH

    class SoftmaxMap:
        def __init__(self, qm, is_diff, kvc):
            self.qm = qm
            self.cs = slice(kvc * LANES, (kvc + 1) * LANES)
            self.vrows = _value_rows(kvc)

        def score_pass(self):
            self.s = _dot_nt(k_ref[0, :, self.cs], self.qm)
            self.m = _fold_rows(self.s, jnp.max).max(axis=0, keepdims=True)

        def pv_pass(self):
            e = jnp.exp2(self.s - self.m)
            tot = _fold_rows(e, jnp.sum).sum(axis=0, keepdims=True)
            self.ot = _dot(vt_ref[self.vrows, :], e.astype(BF16)) * (1.0 / tot)

    maps = []
    for is_diff, qc, kvc, _ in items:
        q = q_ref[0, :, qc * LANES:(qc + 1) * LANES]
        zero = jnp.zeros_like(q)
        maps += [SoftmaxMap(jnp.where(low, q, zero), is_diff, kvc),
                 SoftmaxMap(jnp.where(low, zero, q), is_diff, kvc)]

    groups = [maps[g:g + maps_per_group] for g in range(0, len(maps), maps_per_group)]
    for stage in range(len(groups) + 1):
        if stage > 0:
            for mp in groups[stage - 1]:
                mp.pv_pass()
        if stage < len(groups):
            for mp in groups[stage]:
                mp.score_pass()

    lam = _lambda(lamv_ref)
    for n, (is_diff, _, _, oc) in enumerate(items):
        o = _combine_heads(is_diff, maps[2 * n].ot, maps[2 * n + 1].ot, lam, subln_ref[...])
        out_ref = od_ref if is_diff else og_ref
        out_ref[0, :, oc * LANES:(oc + 1) * LANES] = o.astype(BF16)


def _attn_loop_kernel(q_ref, kn_ref, vtn_ref, kc_ref, vtc_ref, lamv_ref, subln_ref, out_ref, *scratch,
                      is_diff, kv_cols):
    tq = ATTN_TILE_Q
    n_tiles = q_ref.shape[1] // tq
    n_groups = len(kv_cols)
    s_refs = scratch[:n_groups]
    m_refs = scratch[n_groups:]
    t_new = kn_ref.shape[1]
    segs = [(kn_ref, vtn_ref, 0), (kc_ref, vtc_ref, t_new)]
    chunks = []
    for si, (k_ref, _, base) in enumerate(segs):
        n = k_ref.shape[1]
        chunks += [(si, c0, min(n, c0 + ATTN_CHUNK_K), base + c0) for c0 in range(0, n, ATTN_CHUNK_K)]

    low = lax.broadcasted_iota(jnp.int32, (tq, LANES), 1) < DH
    lam = _lambda(lamv_ref)

    def rows(tile):
        return pl.ds(pl.multiple_of(tile * tq, tq), tq)

    def stage(b_g, a_g, a_tile):
        if a_g is not None:
            q = q_ref[0, rows(a_tile), a_g * LANES:(a_g + 1) * LANES]
            zero = jnp.zeros_like(q)
            qm = jnp.concatenate([jnp.where(low, q, zero), jnp.where(low, zero, q)], axis=0)
            a_cs = slice(kv_cols[a_g] * LANES, (kv_cols[a_g] + 1) * LANES)
            mpart = None
        if b_g is not None:
            m = m_refs[b_g][...]
            b_rows = _value_rows(kv_cols[b_g])
            acc = None
            lpart = None
        for si, c0, c1, r0 in chunks:
            if a_g is not None:
                s = _dot_nt(segs[si][0][0, c0:c1, a_cs], qm)
                s_refs[a_g][r0:r0 + c1 - c0, :] = s
                part = _fold_rows(s, jnp.max)
                mpart = part if mpart is None else jnp.maximum(mpart, part)
            if b_g is not None:
                e = jnp.exp2(s_refs[b_g][r0:r0 + c1 - c0, :] - m)
                part = _fold_rows(e, jnp.sum)
                pv = _dot(segs[si][1][b_rows, c0:c1], e.astype(BF16))
                lpart = part if lpart is None else lpart + part
                acc = pv if acc is None else acc + pv
        if a_g is not None:
            m_refs[a_g][...] = mpart.max(axis=0, keepdims=True)
        if b_g is not None:
            ot = acc * (1.0 / lpart.sum(axis=0, keepdims=True))
            return ot[:, :tq], ot[:, tq:]
        return None

    def emit(g, tile, ot_a, ot_b):
        o = _combine_heads(is_diff, ot_a, ot_b, lam, subln_ref[...])
        out_ref[0, rows(tile), g * LANES:(g + 1) * LANES] = o.astype(BF16)

    stage(None, 0, 0)

    def body(t, carry):
        for g in range(n_groups):
            if g + 1 < n_groups:
                ot_a, ot_b = stage(g, g + 1, t)
            else:
                ot_a, ot_b = stage(g, 0, jnp.minimum(t + 1, n_tiles - 1))
            emit(g, t, ot_a, ot_b)
        return carry

    lax.fori_loop(0, n_tiles, body, 0)


def _attn_sample_call(q, k, vt, kc, vtc, lamv, subln, *, is_diff, name):
    b, t, _ = q.shape
    tc = kc.shape[1]
    qw = ATTN_COLS_PER_STEP * LANES
    nblk = DQ_DIFF // qw
    if is_diff:
        kw = qw
        q0 = 0
        kv0 = 0
        kv_cols = tuple(range(ATTN_COLS_PER_STEP))
    else:
        assert ATTN_COLS_PER_STEP * (LANES // DH) == G_GQA
        kw = LANES
        q0 = DQ_DIFF // qw
        kv0 = DQ_DIFF // kw
        kv_cols = (0,) * ATTN_COLS_PER_STEP
    in_specs = [
        pl.BlockSpec((1, t, qw), lambda bi, j: (bi, 0, q0 + j)),
        pl.BlockSpec((1, t, kw), lambda bi, j: (bi, 0, kv0 + j)),
        pl.BlockSpec((kw, t), lambda bi, j: (kv0 + j, bi)),
        pl.BlockSpec((1, tc, kw), lambda bi, j: (bi, 0, kv0 + j)),
        pl.BlockSpec((kw, tc), lambda bi, j: (kv0 + j, bi)),
        pl.BlockSpec((4, DH), lambda bi, j: (0, 0)),
        pl.BlockSpec((1, LANES), lambda bi, j: (0, 0)),
    ]
    return pl.pallas_call(
        functools.partial(_attn_loop_kernel, is_diff=is_diff, kv_cols=kv_cols),
        grid=(b, nblk),
        in_specs=in_specs,
        out_specs=pl.BlockSpec((1, t, qw), lambda bi, j: (bi, 0, j)),
        out_shape=jax.ShapeDtypeStruct((b, t, nblk * qw), BF16),
        scratch_shapes=([pltpu.VMEM((t + tc, 2 * ATTN_TILE_Q), F32)] * ATTN_COLS_PER_STEP
                        + [pltpu.VMEM((1, 2 * ATTN_TILE_Q), F32)] * ATTN_COLS_PER_STEP),
        compiler_params=pltpu.CompilerParams(vmem_limit_bytes=VMEM_LIMIT_BYTES),
        name=name,
    )(q, k, vt, kc, vtc, lamv, subln)


def _attn_ctx_call(q, k, vt, lamv, subln, *, name):
    b, t, _ = q.shape
    ncol = DQ_DIFF // LANES
    items = tuple((True, j, j, j) for j in range(ncol)) + tuple(
        (False, ncol + j, ncol + j // (LANES // DH), j) for j in range(ncol))
    whole = lambda w: pl.BlockSpec((1, t, w), lambda bi: (bi, 0, 0))
    return pl.pallas_call(
        functools.partial(_attn_kernel, items=items, maps_per_group=ATTN_CTX_MAPS_PER_GROUP),
        grid=(b,),
        in_specs=[whole(D_MIX), whole(D_KV_ALL),
                  pl.BlockSpec((D_KV_ALL, t), lambda bi: (0, bi)),
                  pl.BlockSpec((4, DH), lambda bi: (0, 0)),
                  pl.BlockSpec((1, LANES), lambda bi: (0, 0))],
        out_specs=[whole(DQ_DIFF), whole(DQ_GQA)],
        out_shape=[jax.ShapeDtypeStruct((b, t, DQ_DIFF), BF16),
                   jax.ShapeDtypeStruct((b, t, DQ_GQA), BF16)],
        compiler_params=pltpu.CompilerParams(vmem_limit_bytes=VMEM_LIMIT_BYTES),
        name=name,
    )(q, k, vt, lamv, subln)


def _rope_tables(n_tokens):
    t = np.arange(n_tokens)
    row = (t // GRID_W).astype(np.float32)
    col = (t % GRID_W).astype(np.float32)
    half = DH // 2
    inv = np.float32(ROPE_THETA) ** (-(np.arange(0, half, 2, dtype=np.float32) / np.float32(half)))
    ang_r = row[:, None] * inv
    ang_c = col[:, None] * inv
    cos = np.concatenate([np.cos(ang_r)] * 2 + [np.cos(ang_c)] * 2, axis=1)
    sin = np.concatenate([-np.sin(ang_r), np.sin(ang_r), -np.sin(ang_c), np.sin(ang_c)], axis=1)
    reps = LANES // DH
    return (jnp.asarray(np.concatenate([cos] * reps, axis=1), F32),
            jnp.asarray(np.concatenate([sin] * reps, axis=1), F32))


def _dup_heads(a):
    parts = []
    for n in range(H_KV):
        head = a[..., n * DH:(n + 1) * DH]
        parts += [head, head]
    return jnp.concatenate(parts, axis=-1)


def kernel(x_prompt, x_sample, c, cache_diff_k, cache_diff_v, cache_gqa_k, cache_gqa_v, c_ctx, w_ada, b_ada, norm_ff1, w_ff1_gu, w_ff1_down, norm_mix, w_in, q_norm, k_norm, lambda_q1, lambda_k1, lambda_q2, lambda_k2, subln, w_out, norm_ff2, w_ff2_gu, w_ff2_down, final_norm):
    assert w_ada.shape[0] == 1, "single trunk layer"
    bc, tc_, _ = x_prompt.shape
    bs, ts, _ = x_sample.shape
    tpast = cache_diff_k.shape[2]
    assert bs + 1 <= MOD_ROWS
    ctx_row = bs

    cvecs = jnp.concatenate([c, c_ctx[None], jnp.zeros((MOD_ROWS - bs - 1, D_MODEL), F32)], axis=0)
    wgu1 = w_ff1_gu[0].astype(BF16)
    wd1 = w_ff1_down[0].astype(BF16)
    wgu2 = w_ff2_gu[0].astype(BF16)
    wd2 = w_ff2_down[0].astype(BF16)
    wout = w_out[0].astype(BF16)
    wi = w_in[0]
    n_qkvq = 3 * DQ_DIFF + DQ_GQA
    w_ext = jnp.concatenate(
        [wi[:, :n_qkvq], _dup_heads(wi[:, n_qkvq:n_qkvq + DKV_GQA]), _dup_heads(wi[:, n_qkvq + DKV_GQA:])],
        axis=1).astype(BF16)
    seg128 = (jnp.arange(LANES)[:, None] // DH == jnp.arange(LANES)[None, :] // DH).astype(BF16)
    seg = jnp.concatenate([seg128, seg128], axis=0)
    qkn = jnp.concatenate([jnp.tile(q_norm[0], H_GQA), jnp.tile(k_norm[0], 2 * H_KV)])[None]
    lamv = jnp.stack([lambda_q1[0], lambda_k1[0], lambda_q2[0], lambda_k2[0]])
    g1 = norm_ff1
    gm = norm_mix
    g2 = norm_ff2
    fg = final_norm[None]
    cos, sin = _rope_tables(ts)
    kcache = jnp.concatenate([cache_diff_k[:, 0].reshape(bs, tpast, DQ_DIFF),
                              _dup_heads(cache_gqa_k[:, 0].reshape(bs, tpast, DKV_GQA))], axis=-1).astype(BF16)
    vcache = jnp.concatenate([cache_diff_v[:, 0].reshape(bs, tpast, DQ_DIFF),
                              _dup_heads(cache_gqa_v[:, 0].reshape(bs, tpast, DKV_GQA))], axis=-1).astype(BF16)
    vcache_t = vcache.transpose(2, 0, 1).reshape(D_KV_ALL, bs * tpast)

    mod = _mod_call(cvecs, w_ada[0], b_ada).reshape(MOD_ROWS, N_MOD, D_MODEL)

    ctx_map = lambda i: (ctx_row, 0, 0)
    xc = x_prompt.reshape(bc * tc_, D_MODEL)
    xc = _ffn_call(xc, mod, ctx_map, g1, wgu1, wd1, sub=0, name="ffn1_ctx")
    qc, kc, vtc, dkt32, dv4, gkt32, gvt32 = _proj_call(
        xc, mod, ctx_map, gm, w_ext, seg, qkn, rope_tables=None, cache_seq=tc_, name="proj_ctx")
    odc, ogc = _attn_ctx_call(qc.reshape(bc, tc_, D_MIX), kc.reshape(bc, tc_, D_KV_ALL), vtc,
                              lamv, subln, name="attn_ctx")
    yc = _ffn_call(xc, mod, ctx_map, g2, wgu2, wd2, sub=2,
                   pre=(odc.reshape(bc * tc_, DQ_DIFF), ogc.reshape(bc * tc_, DQ_GQA), wout),
                   final_gain=fg, name="ffn2_ctx")

    ffn_tiles = ts // FFN_TILE
    proj_tiles = ts // PROJ_TILE
    xs = x_sample.reshape(bs * ts, D_MODEL)
    xs = _ffn_call(xs, mod, lambda i: (i // ffn_tiles, 0, 0), g1, wgu1, wd1, sub=0, name="ffn1_smp")
    qs, ks, vts = _proj_call(xs, mod, lambda i: (i // proj_tiles, 0, 0), gm, w_ext, seg, qkn,
                             rope_tables=(cos, sin), cache_seq=None, name="proj_smp")
    qs = qs.reshape(bs, ts, D_MIX)
    ks = ks.reshape(bs, ts, D_KV_ALL)
    ods = _attn_sample_call(qs, ks, vts, kcache, vcache_t, lamv, subln, is_diff=True, name="attn_diff_smp")
    ogs = _attn_sample_call(qs, ks, vts, kcache, vcache_t, lamv, subln, is_diff=False, name="attn_gqa_smp")
    ys = _ffn_call(xs, mod, lambda i: (i // ffn_tiles, 0, 0), g2, wgu2, wd2, sub=2,
                   pre=(ods.reshape(bs * ts, DQ_DIFF), ogs.reshape(bs * ts, DQ_GQA), wout),
                   final_gain=fg, name="ffn2_smp")

    return (yc.reshape(bc, tc_, D_MODEL),
            ys.reshape(bs, ts, D_MODEL),
            dkt32.reshape(bc, 1, H_DIFF, 2, DH, tc_).transpose(0, 1, 5, 2, 3, 4),
            dv4.reshape(bc, 1, tc_, H_DIFF, 2 * DH),
            gkt32.reshape(bc, 1, H_KV, DH, tc_).transpose(0, 1, 4, 2, 3),
            gvt32.reshape(bc, 1, H_KV, DH, tc_).transpose(0, 1, 4, 2, 3))
```

```python
import functools
import math

import jax
import jax.numpy as jnp
import numpy as np
from jax import lax
from jax.experimental import pallas as pl
from jax.experimental.pallas import tpu as pltpu

F32 = jnp.float32
BF16 = jnp.bfloat16

D_MODEL = 1024
N_MOD = 9
D_FF = 2816
H_DIFF = 4
DH = 64
H_GQA = 8
H_KV = 2
G_GQA = H_GQA // H_KV
GRID_W = 64
ROPE_THETA = 10000.0
EPS = 1e-6
LAMBDA_INIT = 0.8 - 0.6 * math.exp(-0.3 * 0)
DQ_DIFF = H_DIFF * 2 * DH
DQ_GQA = H_GQA * DH
DKV_GQA = H_KV * DH
D_IN = 3 * DQ_DIFF + DQ_GQA + 2 * DKV_GQA
D_MIX = DQ_DIFF + DQ_GQA
D_KV_ALL = DQ_DIFF + 2 * DKV_GQA
D_NORMED = DQ_GQA + 2 * DKV_GQA
SCORE_SCALE = DH ** -0.5 * math.log2(math.e)

LANES = 128
SUBLANES = 8
VMEM_LIMIT_BYTES = 60000 * 1024

MOD_ROWS = 8
MOD_TILE_N = 1152
FFN_TILE = 512
PROJ_TILE = 1024
ATTN_TILE_Q = 256
ATTN_CHUNK_K = 512
ATTN_COLS_PER_STEP = 2
ATTN_CTX_MAPS_PER_GROUP = 8


def _sigmoid(x):
    return 1.0 / (1.0 + jnp.exp(-x))


def _rms(x):
    return x * lax.rsqrt(jnp.mean(x * x, axis=-1, keepdims=True) + EPS)


def _dot(a, b):
    return jnp.dot(a, b, preferred_element_type=F32)


def _dot_nt(a, b):
    return lax.dot_general(a, b, (((1,), (1,)), ((), ())), preferred_element_type=F32)


def _resident(shape):
    return pl.BlockSpec(shape, lambda *_: (0,) * len(shape), pipeline_mode=pl.Buffered(1))


def _mod_kernel(c_ref, w_ref, b_ref, o_ref):
    c = c_ref[...]
    s = c * _sigmoid(c)
    o_ref[...] = _dot(s.astype(BF16), w_ref[...].astype(BF16)) + b_ref[...]


def _mod_call(cvecs, w_ada, b_ada):
    n = w_ada.shape[1]
    return pl.pallas_call(
        _mod_kernel,
        grid=(n // MOD_TILE_N,),
        in_specs=[
            pl.BlockSpec((MOD_ROWS, D_MODEL), lambda j: (0, 0)),
            pl.BlockSpec((D_MODEL, MOD_TILE_N), lambda j: (0, j)),
            pl.BlockSpec((1, MOD_TILE_N), lambda j: (0, j)),
        ],
        out_specs=pl.BlockSpec((MOD_ROWS, MOD_TILE_N), lambda j: (0, j)),
        out_shape=jax.ShapeDtypeStruct((MOD_ROWS, n), F32),
        name="mod",
    )(cvecs, w_ada, b_ada)


def _ffn_kernel(*refs, sub, pre, final):
    refs = list(refs)
    x_ref, mod_ref = refs[:2]
    pos = 2
    if pre:
        od_ref, og_ref, wout_ref = refs[pos:pos + 3]
        pos += 3
    g_ref, wgu_ref, wd_ref = refs[pos:pos + 3]
    pos += 3
    if final:
        fg_ref = refs[pos]
        pos += 1
    o_ref = refs[pos]

    x = x_ref[...]
    mod = mod_ref[0]
    if pre:
        o = jnp.concatenate([od_ref[...], og_ref[...]], axis=1)
        x = x + mod[5:6] * _dot(o, wout_ref[...])
    shift = mod[3 * sub:3 * sub + 1]
    scale = mod[3 * sub + 1:3 * sub + 2]
    gate = mod[3 * sub + 2:3 * sub + 3]
    h = (_rms(x) * g_ref[...]) * (1.0 + scale) + shift
    gu = _dot(h.astype(BF16), wgu_ref[...])
    g = gu[:, :D_FF]
    u = gu[:, D_FF:]
    act = (g * _sigmoid(g)) * u
    y = _dot(act.astype(BF16), wd_ref[...])
    x = x + (0.5 * gate) * y
    if final:
        x = _rms(x) * fg_ref[...]
    o_ref[...] = x


def _ffn_call(x, mod, row_map, gain, wgu, wd, *, sub, pre=None, final_gain=None, name):
    t = x.shape[0]
    tm = FFN_TILE
    row_spec = lambda w: pl.BlockSpec((tm, w), lambda i: (i, 0))
    in_specs = [row_spec(D_MODEL), pl.BlockSpec((1, N_MOD, D_MODEL), row_map)]
    args = [x, mod]
    if pre is not None:
        od, og, wout = pre
        in_specs += [row_spec(DQ_DIFF), row_spec(DQ_GQA), _resident((D_MIX, D_MODEL))]
        args += [od, og, wout]
    in_specs += [_resident((1, D_MODEL)), _resident((D_MODEL, 2 * D_FF)), _resident((D_FF, D_MODEL))]
    args += [gain, wgu, wd]
    if final_gain is not None:
        in_specs.append(_resident((1, D_MODEL)))
        args.append(final_gain)
    return pl.pallas_call(
        functools.partial(_ffn_kernel, sub=sub, pre=pre is not None, final=final_gain is not None),
        grid=(t // tm,),
        in_specs=in_specs,
        out_specs=row_spec(D_MODEL),
        out_shape=jax.ShapeDtypeStruct((t, D_MODEL), F32),
        compiler_params=pltpu.CompilerParams(vmem_limit_bytes=VMEM_LIMIT_BYTES),
        name=name,
    )(*args)


def _rope(x, cos, sin, first_of_pair):
    w = x.shape[1]
    partner = jnp.where(first_of_pair, pltpu.roll(x, w - 16, 1), pltpu.roll(x, 16, 1))
    reps = w // LANES
    cos_w = jnp.concatenate([cos] * reps, axis=1) if reps > 1 else cos
    sin_w = jnp.concatenate([sin] * reps, axis=1) if reps > 1 else sin
    return x * cos_w + partner * sin_w


def _proj_kernel(*refs, rope, emit_f32):
    refs = list(refs)
    x_ref, mod_ref, g_ref, w_ref, seg_ref, qkn_ref = refs[:6]
    pos = 6
    if rope:
        cos_ref, sin_ref = refs[pos:pos + 2]
        pos += 2
    q_ref, k_ref, vt_ref = refs[pos:pos + 3]
    pos += 3

    x = x_ref[...]
    mod = mod_ref[0]
    h = (_rms(x) * g_ref[...]) * (1.0 + mod[4:5]) + mod[3:4]
    qkv = _dot(h.astype(BF16), w_ref[...])
    dq = qkv[:, 0:DQ_DIFF]
    dk = qkv[:, DQ_DIFF:2 * DQ_DIFF]
    dv = qkv[:, 2 * DQ_DIFF:3 * DQ_DIFF]
    n0 = 3 * DQ_DIFF
    raw = qkv[:, n0:n0 + D_NORMED]
    gv = qkv[:, n0 + D_NORMED:]

    sq = raw * raw
    hi = sq.astype(BF16)
    lo = (sq - hi.astype(F32)).astype(BF16)
    seg = seg_ref[...]
    ss = jnp.concatenate(
        [_dot(jnp.concatenate([hi[:, c:c + LANES], lo[:, c:c + LANES]], axis=1), seg)
         for c in range(0, D_NORMED, LANES)], axis=1)
    normed = (raw * lax.rsqrt(ss * (1.0 / DH) + EPS)) * qkn_ref[...]
    gq = normed[:, :DQ_GQA]
    gk = normed[:, DQ_GQA:]

    if rope:
        cos = cos_ref[...]
        sin = sin_ref[...]
        lane = lax.broadcasted_iota(jnp.int32, (x.shape[0], LANES), 1)
        first = (lane % 32) < 16
        first4 = jnp.concatenate([first] * 4, axis=1)
        first2 = jnp.concatenate([first] * 2, axis=1)
        dq = _rope(dq, cos, sin, first4)
        dk = _rope(dk, cos, sin, first4)
        gq = _rope(gq, cos, sin, first4)
        gk = _rope(gk, cos, sin, first2)

    q_ref[...] = jnp.concatenate([dq * SCORE_SCALE, gq * SCORE_SCALE], axis=1).astype(BF16)
    k_ref[...] = jnp.concatenate([dk, gk], axis=1).astype(BF16)
    vt_ref[...] = jnp.concatenate([dv, gv], axis=1).T.astype(BF16)

    if emit_f32:
        dkt_ref, dv4_ref, gkt_ref, gvt_ref = refs[pos:pos + 4]
        tm = x.shape[0]
        seq = dkt_ref.shape[2]
        lane = lax.broadcasted_iota(jnp.int32, (tm, LANES), 1)
        low = lane < DH
        dkt = dk.T
        gkt = jnp.where(low, gk[:, :LANES], gk[:, LANES:]).T
        gvt = jnp.where(low, gv[:, :LANES], gv[:, LANES:]).T
        for b in range(tm // seq):
            dkt_ref[b] = dkt[:, b * seq:(b + 1) * seq]
            gkt_ref[b] = gkt[:, b * seq:(b + 1) * seq]
            gvt_ref[b] = gvt[:, b * seq:(b + 1) * seq]
        for hd in range(H_DIFF):
            dv4_ref[pl.ds(hd, tm, stride=H_DIFF), :] = dv[:, hd * LANES:(hd + 1) * LANES]


def _proj_call(x, mod, row_map, gain, w_ext, seg, qkn, *, rope_tables, cache_seq, name):
    t = x.shape[0]
    emit_f32 = cache_seq is not None
    tm = PROJ_TILE
    row_spec = lambda w: pl.BlockSpec((tm, w), lambda i: (i, 0))
    n_ext = w_ext.shape[1]
    in_specs = [row_spec(D_MODEL), pl.BlockSpec((1, N_MOD, D_MODEL), row_map),
                _resident((1, D_MODEL)), _resident((D_MODEL, n_ext)),
                _resident((2 * LANES, LANES)), _resident((1, D_NORMED))]
    args = [x, mod, gain, w_ext, seg, qkn]
    if rope_tables is not None:
        cos, sin = rope_tables
        tiles_per_seq = cos.shape[0] // tm
        tab_spec = pl.BlockSpec((tm, LANES), lambda i: (i % tiles_per_seq, 0))
        in_specs += [tab_spec, tab_spec]
        args += [cos, sin]
    out_specs = [row_spec(D_MIX), row_spec(D_KV_ALL), pl.BlockSpec((D_KV_ALL, tm), lambda i: (0, i))]
    out_shape = [jax.ShapeDtypeStruct((t, D_MIX), BF16),
                 jax.ShapeDtypeStruct((t, D_KV_ALL), BF16),
                 jax.ShapeDtypeStruct((D_KV_ALL, t), BF16)]
    if emit_f32:
        assert tm % cache_seq == 0
        nb = tm // cache_seq
        slab = lambda rows: pl.BlockSpec((nb, rows, cache_seq), lambda i: (i, 0, 0))
        out_specs += [slab(DQ_DIFF), pl.BlockSpec((tm * H_DIFF, LANES), lambda i: (i, 0)),
                      slab(DKV_GQA), slab(DKV_GQA)]
        out_shape += [jax.ShapeDtypeStruct((t // cache_seq, DQ_DIFF, cache_seq), F32),
                      jax.ShapeDtypeStruct((t * H_DIFF, LANES), F32),
                      jax.ShapeDtypeStruct((t // cache_seq, DKV_GQA, cache_seq), F32),
                      jax.ShapeDtypeStruct((t // cache_seq, DKV_GQA, cache_seq), F32)]
    return pl.pallas_call(
        functools.partial(_proj_kernel, rope=rope_tables is not None, emit_f32=emit_f32),
        grid=(t // tm,),
        in_specs=in_specs,
        out_specs=out_specs,
        out_shape=out_shape,
        compiler_params=pltpu.CompilerParams(vmem_limit_bytes=VMEM_LIMIT_BYTES),
        name=name,
    )(*args)


def _lambda(lamv_ref):
    lamv = lamv_ref[...]
    return (jnp.exp(jnp.sum(lamv[0:1] * lamv[1:2], axis=-1, keepdims=True))
            - jnp.exp(jnp.sum(lamv[2:3] * lamv[3:4], axis=-1, keepdims=True)) + LAMBDA_INIT)


def _fold_rows(x, op):
    return op(x.reshape(x.shape[0] // SUBLANES, SUBLANES, x.shape[1]), axis=0)


def _value_rows(kvc):
    return slice(kvc * LANES, (kvc + 1) * LANES)


def _combine_heads(is_diff, ot_a, ot_b, lam, subln):
    if is_diff:
        o = (ot_a - lam * ot_b).T
        return (_rms(o) * subln) * (1.0 - LAMBDA_INIT)
    return jnp.concatenate([ot_a[:DH], ot_b[DH:]], axis=0).T


def _attn_kernel(q_ref, k_ref, vt_ref, lamv_ref, subln_ref, od_ref, og_ref, *, items, maps_per_group):
    tq = q_ref.shape[1]
    low = lax.broadcasted_iota(jnp.int32, (tq, LANES), 1) < DH

    class SoftmaxMap:
        def __init__(self, qm, kvc):
            self.qm = qm
            self.cs = slice(kvc * LANES, (kvc + 1) * LANES)
            self.vrows = _value_rows(kvc)

        def score_pass(self):
            self.s = _dot_nt(k_ref[0, :, self.cs], self.qm)
            self.m = _fold_rows(self.s, jnp.max).max(axis=0, keepdims=True)

        def pv_pass(self):
            e = jnp.exp2(self.s - self.m)
            tot = _fold_rows(e, jnp.sum).sum(axis=0, keepdims=True)
            self.ot = _dot(vt_ref[self.vrows, :], e.astype(BF16)) * (1.0 / tot)

    maps = []
    for is_diff, qc, kvc, _ in items:
        q = q_ref[0, :, qc * LANES:(qc + 1) * LANES]
        zero = jnp.zeros_like(q)
        maps += [SoftmaxMap(jnp.where(low, q, zero), kvc), SoftmaxMap(jnp.where(low, zero, q), kvc)]

    groups = [maps[g:g + maps_per_group] for g in range(0, len(maps), maps_per_group)]
    for stage in range(len(groups) + 1):
        if stage > 0:
            for mp in groups[stage - 1]:
                mp.pv_pass()
        if stage < len(groups):
            for mp in groups[stage]:
                mp.score_pass()

    lam = _lambda(lamv_ref)
    for n, (is_diff, _, _, oc) in enumerate(items):
        o = _combine_heads(is_diff, maps[2 * n].ot, maps[2 * n + 1].ot, lam, subln_ref[...])
        out_ref = od_ref if is_diff else og_ref
        out_ref[0, :, oc * LANES:(oc + 1) * LANES] = o.astype(BF16)


def _attn_loop_kernel(q_ref, kn_ref, vtn_ref, kc_ref, vtc_ref, lamv_ref, subln_ref, out_ref, *scratch,
                      is_diff, kv_cols):
    tq = ATTN_TILE_Q
    n_tiles = q_ref.shape[1] // tq
    n_groups = len(kv_cols)
    s_refs = scratch[:n_groups]
    m_refs = scratch[n_groups:]
    t_new = kn_ref.shape[1]
    segs = [(kn_ref, vtn_ref, 0), (kc_ref, vtc_ref, t_new)]
    chunks = []
    for si, (k_ref, _, base) in enumerate(segs):
        n = k_ref.shape[1]
        chunks += [(si, c0, min(n, c0 + ATTN_CHUNK_K), base + c0) for c0 in range(0, n, ATTN_CHUNK_K)]

    low = lax.broadcasted_iota(jnp.int32, (tq, LANES), 1) < DH
    lam = _lambda(lamv_ref)

    def rows(tile):
        return pl.ds(pl.multiple_of(tile * tq, tq), tq)

    def stage(b_g, a_g, a_tile):
        if a_g is not None:
            q = q_ref[0, rows(a_tile), a_g * LANES:(a_g + 1) * LANES]
            zero = jnp.zeros_like(q)
            qm = jnp.concatenate([jnp.where(low, q, zero), jnp.where(low, zero, q)], axis=0)
            a_cs = slice(kv_cols[a_g] * LANES, (kv_cols[a_g] + 1) * LANES)
            mpart = None
        if b_g is not None:
            m = m_refs[b_g][...]
            b_rows = _value_rows(kv_cols[b_g])
            acc = None
            lpart = None
        for si, c0, c1, r0 in chunks:
            if a_g is not None:
                s = _dot_nt(segs[si][0][0, c0:c1, a_cs], qm)
                s_refs[a_g][r0:r0 + c1 - c0, :] = s
                part = _fold_rows(s, jnp.max)
                mpart = part if mpart is None else jnp.maximum(mpart, part)
            if b_g is not None:
                e = jnp.exp2(s_refs[b_g][r0:r0 + c1 - c0, :] - m)
                part = _fold_rows(e, jnp.sum)
                pv = _dot(segs[si][1][b_rows, c0:c1], e.astype(BF16))
                lpart = part if lpart is None else lpart + part
                acc = pv if acc is None else acc + pv
        if a_g is not None:
            m_refs[a_g][...] = mpart.max(axis=0, keepdims=True)
        if b_g is not None:
            ot = acc * (1.0 / lpart.sum(axis=0, keepdims=True))
            return ot[:, :tq], ot[:, tq:]
        return None

    def emit(g, tile, ot_a, ot_b):
        o = _combine_heads(is_diff, ot_a, ot_b, lam, subln_ref[...])
        out_ref[0, rows(tile), g * LANES:(g + 1) * LANES] = o.astype(BF16)

    stage(None, 0, 0)

    def body(t, carry):
        for g in range(n_groups):
            if g + 1 < n_groups:
                ot_a, ot_b = stage(g, g + 1, t)
            else:
                ot_a, ot_b = stage(g, 0, jnp.minimum(t + 1, n_tiles - 1))
            emit(g, t, ot_a, ot_b)
        return carry

    lax.fori_loop(0, n_tiles, body, 0)


def _attn_sample_call(q, k, vt, kc, vtc, lamv, subln, *, is_diff, name):
    b, t, _ = q.shape
    tc = kc.shape[1]
    qw = ATTN_COLS_PER_STEP * LANES
    nblk = DQ_DIFF // qw
    if is_diff:
        kw = qw
        q0 = 0
        kv0 = 0
        kv_cols = tuple(range(ATTN_COLS_PER_STEP))
    else:
        assert ATTN_COLS_PER_STEP * (LANES // DH) == G_GQA
        kw = LANES
        q0 = DQ_DIFF // qw
        kv0 = DQ_DIFF // kw
        kv_cols = (0,) * ATTN_COLS_PER_STEP
    in_specs = [
        pl.BlockSpec((1, t, qw), lambda bi, j: (bi, 0, q0 + j)),
        pl.BlockSpec((1, t, kw), lambda bi, j: (bi, 0, kv0 + j)),
        pl.BlockSpec((kw, t), lambda bi, j: (kv0 + j, bi)),
        pl.BlockSpec((1, tc, kw), lambda bi, j: (bi, 0, kv0 + j)),
        pl.BlockSpec((kw, tc), lambda bi, j: (kv0 + j, bi)),
        pl.BlockSpec((4, DH), lambda bi, j: (0, 0)),
        pl.BlockSpec((1, LANES), lambda bi, j: (0, 0)),
    ]
    return pl.pallas_call(
        functools.partial(_attn_loop_kernel, is_diff=is_diff, kv_cols=kv_cols),
        grid=(b, nblk),
        in_specs=in_specs,
        out_specs=pl.BlockSpec((1, t, qw), lambda bi, j: (bi, 0, j)),
        out_shape=jax.ShapeDtypeStruct((b, t, nblk * qw), BF16),
        scratch_shapes=([pltpu.VMEM((t + tc, 2 * ATTN_TILE_Q), F32)] * ATTN_COLS_PER_STEP
                        + [pltpu.VMEM((1, 2 * ATTN_TILE_Q), F32)] * ATTN_COLS_PER_STEP),
        compiler_params=pltpu.CompilerParams(vmem_limit_bytes=VMEM_LIMIT_BYTES),
        name=name,
    )(q, k, vt, kc, vtc, lamv, subln)


def _attn_ctx_call(q, k, vt, lamv, subln, *, name):
    b, t, _ = q.shape
    ncol = DQ_DIFF // LANES
    items = tuple((True, j, j, j) for j in range(ncol)) + tuple(
        (False, ncol + j, ncol + j // (LANES // DH), j) for j in range(ncol))
    whole = lambda w: pl.BlockSpec((1, t, w), lambda bi: (bi, 0, 0))
    return pl.pallas_call(
        functools.partial(_attn_kernel, items=items, maps_per_group=ATTN_CTX_MAPS_PER_GROUP),
        grid=(b,),
        in_specs=[whole(D_MIX), whole(D_KV_ALL),
                  pl.BlockSpec((D_KV_ALL, t), lambda bi: (0, bi)),
                  pl.BlockSpec((4, DH), lambda bi: (0, 0)),
                  pl.BlockSpec((1, LANES), lambda bi: (0, 0))],
        out_specs=[whole(DQ_DIFF), whole(DQ_GQA)],
        out_shape=[jax.ShapeDtypeStruct((b, t, DQ_DIFF), BF16),
                   jax.ShapeDtypeStruct((b, t, DQ_GQA), BF16)],
        compiler_params=pltpu.CompilerParams(vmem_limit_bytes=VMEM_LIMIT_BYTES),
        name=name,
    )(q, k, vt, lamv, subln)


def _rope_tables(n_tokens):
    t = np.arange(n_tokens)
    row = (t // GRID_W).astype(np.float32)
    col = (t % GRID_W).astype(np.float32)
    half = DH // 2
    inv = np.float32(ROPE_THETA) ** (-(np.arange(0, half, 2, dtype=np.float32) / np.float32(half)))
    ang_r = row[:, None] * inv
    ang_c = col[:, None] * inv
    cos = np.concatenate([np.cos(ang_r)] * 2 + [np.cos(ang_c)] * 2, axis=1)
    sin = np.concatenate([-np.sin(ang_r), np.sin(ang_r), -np.sin(ang_c), np.sin(ang_c)], axis=1)
    reps = LANES // DH
    return (jnp.asarray(np.concatenate([cos] * reps, axis=1), F32),
            jnp.asarray(np.concatenate([sin] * reps, axis=1), F32))


def _dup_heads(a):
    parts = []
    for n in range(H_KV):
        head = a[..., n * DH:(n + 1) * DH]
        parts += [head, head]
    return jnp.concatenate(parts, axis=-1)


def kernel(x_prompt, x_sample, c, cache_diff_k, cache_diff_v, cache_gqa_k, cache_gqa_v, c_ctx, w_ada, b_ada, norm_ff1, w_ff1_gu, w_ff1_down, norm_mix, w_in, q_norm, k_norm, lambda_q1, lambda_k1, lambda_q2, lambda_k2, subln, w_out, norm_ff2, w_ff2_gu, w_ff2_down, final_norm):
    assert w_ada.shape[0] == 1, "single trunk layer"
    bc, tc_, _ = x_prompt.shape
    bs, ts, _ = x_sample.shape
    tpast = cache_diff_k.shape[2]
    assert bs + 1 <= MOD_ROWS
    ctx_row = bs

    cvecs = jnp.concatenate([c, c_ctx[None], jnp.zeros((MOD_ROWS - bs - 1, D_MODEL), F32)], axis=0)
    wgu1 = w_ff1_gu[0].astype(BF16)
    wd1 = w_ff1_down[0].astype(BF16)
    wgu2 = w_ff2_gu[0].astype(BF16)
    wd2 = w_ff2_down[0].astype(BF16)
    wout = w_out[0].astype(BF16)
    wi = w_in[0]
    n_qkvq = 3 * DQ_DIFF + DQ_GQA
    w_ext = jnp.concatenate(
        [wi[:, :n_qkvq], _dup_heads(wi[:, n_qkvq:n_qkvq + DKV_GQA]), _dup_heads(wi[:, n_qkvq + DKV_GQA:])],
        axis=1).astype(BF16)
    seg128 = (jnp.arange(LANES)[:, None] // DH == jnp.arange(LANES)[None, :] // DH).astype(BF16)
    seg = jnp.concatenate([seg128, seg128], axis=0)
    qkn = jnp.concatenate([jnp.tile(q_norm[0], H_GQA), jnp.tile(k_norm[0], 2 * H_KV)])[None]
    lamv = jnp.stack([lambda_q1[0], lambda_k1[0], lambda_q2[0], lambda_k2[0]])
    g1 = norm_ff1
    gm = norm_mix
    g2 = norm_ff2
    fg = final_norm[None]
    cos, sin = _rope_tables(ts)
    kcache = jnp.concatenate([cache_diff_k[:, 0].reshape(bs, tpast, DQ_DIFF),
                              _dup_heads(cache_gqa_k[:, 0].reshape(bs, tpast, DKV_GQA))], axis=-1).astype(BF16)
    vcache = jnp.concatenate([cache_diff_v[:, 0].reshape(bs, tpast, DQ_DIFF),
                              _dup_heads(cache_gqa_v[:, 0].reshape(bs, tpast, DKV_GQA))], axis=-1).astype(BF16)
    vcache_t = vcache.transpose(2, 0, 1).reshape(D_KV_ALL, bs * tpast)

    mod = _mod_call(cvecs, w_ada[0], b_ada).reshape(MOD_ROWS, N_MOD, D_MODEL)

    ctx_map = lambda i: (ctx_row, 0, 0)
    xc = x_prompt.reshape(bc * tc_, D_MODEL)
    xc = _ffn_call(xc, mod, ctx_map, g1, wgu1, wd1, sub=0, name="ffn1_ctx")
    qc, kc, vtc, dkt32, dv4, gkt32, gvt32 = _proj_call(
        xc, mod, ctx_map, gm, w_ext, seg, qkn, rope_tables=None, cache_seq=tc_, name="proj_ctx")
    odc, ogc = _attn_ctx_call(qc.reshape(bc, tc_, D_MIX), kc.reshape(bc, tc_, D_KV_ALL), vtc,
                              lamv, subln, name="attn_ctx")
    yc = _ffn_call(xc, mod, ctx_map, g2, wgu2, wd2, sub=2,
                   pre=(odc.reshape(bc * tc_, DQ_DIFF), ogc.reshape(bc * tc_, DQ_GQA), wout),
                   final_gain=fg, name="ffn2_ctx")

    ffn_tiles = ts // FFN_TILE
    proj_tiles = ts // PROJ_TILE
    xs = x_sample.reshape(bs * ts, D_MODEL)
    xs = _ffn_call(xs, mod, lambda i: (i // ffn_tiles, 0, 0), g1, wgu1, wd1, sub=0, name="ffn1_smp")
    qs, ks, vts = _proj_call(xs, mod, lambda i: (i // proj_tiles, 0, 0), gm, w_ext, seg, qkn,
                             rope_tables=(cos, sin), cache_seq=None, name="proj_smp")
    qs = qs.reshape(bs, ts, D_MIX)
    ks = ks.reshape(bs, ts, D_KV_ALL)
    ods = _attn_sample_call(qs, ks, vts, kcache, vcache_t, lamv, subln, is_diff=True, name="attn_diff_smp")
    ogs = _attn_sample_call(qs, ks, vts, kcache, vcache_t, lamv, subln, is_diff=False, name="attn_gqa_smp")
    ys = _ffn_call(xs, mod, lambda i: (i // ffn_tiles, 0, 0), g2, wgu2, wd2, sub=2,
                   pre=(ods.reshape(bs * ts, DQ_DIFF), ogs.reshape(bs * ts, DQ_GQA), wout),
                   final_gain=fg, name="ffn2_smp")

    return (yc.reshape(bc, tc_, D_MODEL),
            ys.reshape(bs, ts, D_MODEL),
            dkt32.reshape(bc, 1, H_DIFF, 2, DH, tc_).transpose(0, 1, 5, 2, 3, 4),
            dv4.reshape(bc, 1, tc_, H_DIFF, 2 * DH),
            gkt32.reshape(bc, 1, H_KV, DH, tc_).transpose(0, 1, 4, 2, 3),
            gvt32.reshape(bc, 1, H_KV, DH, tc_).transpose(0, 1, 4, 2, 3))
```

```python
import functools
import math

import jax
import jax.numpy as jnp
import numpy as np
from jax import lax
from jax.experimental import pallas as pl
from jax.experimental.pallas import tpu as pltpu

F32 = jnp.float32
BF16 = jnp.bfloat16

D_MODEL = 1024
N_MOD = 9
D_FF = 2816
H_DIFF = 4
DH = 64
H_GQA = 8
H_KV = 2
G_GQA = H_GQA // H_KV
GRID_W = 64
ROPE_THETA = 10000.0
EPS = 1e-6
LAMBDA_INIT = 0.8 - 0.6 * math.exp(-0.3 * 0)
DQ_DIFF = H_DIFF * 2 * DH
DQ_GQA = H_GQA * DH
DKV_GQA = H_KV * DH
D_IN = 3 * DQ_DIFF + DQ_GQA + 2 * DKV_GQA
D_MIX = DQ_DIFF + DQ_GQA
D_KV_ALL = DQ_DIFF + 2 * DKV_GQA
D_NORMED = DQ_GQA + 2 * DKV_GQA
SCORE_SCALE = DH ** -0.5 * math.log2(math.e)

LANES = 128
SUBLANES = 8
VMEM_LIMIT_BYTES = 60000 * 1024

MOD_ROWS = 8
MOD_TILE_N = 1152
FFN_TILE = 512
PROJ_TILE = 1024
ATTN_TILE_Q = 256
ATTN_CHUNK_K = 512
ATTN_SKEW = 1
ATTN_COLS_PER_STEP = 2
ATTN_CTX_MAPS_PER_GROUP = 8


def _sigmoid(x):
    return 1.0 / (1.0 + jnp.exp(-x))


def _rms(x):
    return x * lax.rsqrt(jnp.mean(x * x, axis=-1, keepdims=True) + EPS)


def _dot(a, b):
    return jnp.dot(a, b, preferred_element_type=F32)


def _dot_nt(a, b):
    return lax.dot_general(a, b, (((1,), (1,)), ((), ())), preferred_element_type=F32)


def _resident(shape):
    return pl.BlockSpec(shape, lambda *_: (0,) * len(shape), pipeline_mode=pl.Buffered(1))


def _mod_kernel(c_ref, w_ref, b_ref, o_ref):
    c = c_ref[...]
    s = c * _sigmoid(c)
    o_ref[...] = _dot(s.astype(BF16), w_ref[...].astype(BF16)) + b_ref[...]


def _mod_call(cvecs, w_ada, b_ada):
    n = w_ada.shape[1]
    return pl.pallas_call(
        _mod_kernel,
        grid=(n // MOD_TILE_N,),
        in_specs=[
            pl.BlockSpec((MOD_ROWS, D_MODEL), lambda j: (0, 0)),
            pl.BlockSpec((D_MODEL, MOD_TILE_N), lambda j: (0, j)),
            pl.BlockSpec((1, MOD_TILE_N), lambda j: (0, j)),
        ],
        out_specs=pl.BlockSpec((MOD_ROWS, MOD_TILE_N), lambda j: (0, j)),
        out_shape=jax.ShapeDtypeStruct((MOD_ROWS, n), F32),
        name="mod",
    )(cvecs, w_ada, b_ada)


def _ffn_kernel(*refs, sub, pre, final):
    refs = list(refs)
    x_ref, mod_ref = refs[:2]
    pos = 2
    if pre:
        od_ref, og_ref, wout_ref = refs[pos:pos + 3]
        pos += 3
    g_ref, wgu_ref, wd_ref = refs[pos:pos + 3]
    pos += 3
    if final:
        fg_ref = refs[pos]
        pos += 1
    o_ref = refs[pos]

    x = x_ref[...]
    mod = mod_ref[0]
    if pre:
        o = jnp.concatenate([od_ref[...], og_ref[...]], axis=1)
        x = x + mod[5:6] * _dot(o, wout_ref[...])
    shift = mod[3 * sub:3 * sub + 1]
    scale = mod[3 * sub + 1:3 * sub + 2]
    gate = mod[3 * sub + 2:3 * sub + 3]
    h = (_rms(x) * g_ref[...]) * (1.0 + scale) + shift
    gu = _dot(h.astype(BF16), wgu_ref[...])
    g = gu[:, :D_FF]
    u = gu[:, D_FF:]
    act = (g * _sigmoid(g)) * u
    y = _dot(act.astype(BF16), wd_ref[...])
    x = x + (0.5 * gate) * y
    if final:
        x = _rms(x) * fg_ref[...]
    o_ref[...] = x


def _ffn_call(x, mod, row_map, gain, wgu, wd, *, sub, pre=None, final_gain=None, name):
    t = x.shape[0]
    tm = FFN_TILE
    row_spec = lambda w: pl.BlockSpec((tm, w), lambda i: (i, 0))
    in_specs = [row_spec(D_MODEL), pl.BlockSpec((1, N_MOD, D_MODEL), row_map)]
    args = [x, mod]
    if pre is not None:
        od, og, wout = pre
        in_specs += [row_spec(DQ_DIFF), row_spec(DQ_GQA), _resident((D_MIX, D_MODEL))]
        args += [od, og, wout]
    in_specs += [_resident((1, D_MODEL)), _resident((D_MODEL, 2 * D_FF)), _resident((D_FF, D_MODEL))]
    args += [gain, wgu, wd]
    if final_gain is not None:
        in_specs.append(_resident((1, D_MODEL)))
        args.append(final_gain)
    return pl.pallas_call(
        functools.partial(_ffn_kernel, sub=sub, pre=pre is not None, final=final_gain is not None),
        grid=(t // tm,),
        in_specs=in_specs,
        out_specs=row_spec(D_MODEL),
        out_shape=jax.ShapeDtypeStruct((t, D_MODEL), F32),
        compiler_params=pltpu.CompilerParams(vmem_limit_bytes=VMEM_LIMIT_BYTES),
        name=name,
    )(*args)


def _rope(x, cos, sin, first_of_pair):
    w = x.shape[1]
    partner = jnp.where(first_of_pair, pltpu.roll(x, w - 16, 1), pltpu.roll(x, 16, 1))
    reps = w // LANES
    cos_w = jnp.concatenate([cos] * reps, axis=1) if reps > 1 else cos
    sin_w = jnp.concatenate([sin] * reps, axis=1) if reps > 1 else sin
    return x * cos_w + partner * sin_w


def _proj_kernel(*refs, rope, emit_f32):
    refs = list(refs)
    x_ref, mod_ref, g_ref, w_ref, seg_ref, qkn_ref = refs[:6]
    pos = 6
    if rope:
        cos_ref, sin_ref = refs[pos:pos + 2]
        pos += 2
    q_ref, k_ref, vt_ref = refs[pos:pos + 3]
    pos += 3

    x = x_ref[...]
    mod = mod_ref[0]
    h = (_rms(x) * g_ref[...]) * (1.0 + mod[4:5]) + mod[3:4]
    qkv = _dot(h.astype(BF16), w_ref[...])
    dq = qkv[:, 0:DQ_DIFF]
    dk = qkv[:, DQ_DIFF:2 * DQ_DIFF]
    dv = qkv[:, 2 * DQ_DIFF:3 * DQ_DIFF]
    n0 = 3 * DQ_DIFF
    raw = qkv[:, n0:n0 + D_NORMED]
    gv = qkv[:, n0 + D_NORMED:]

    sq = raw * raw
    hi = sq.astype(BF16)
    lo = (sq - hi.astype(F32)).astype(BF16)
    seg = seg_ref[...]
    ss = jnp.concatenate(
        [_dot(jnp.concatenate([hi[:, c:c + LANES], lo[:, c:c + LANES]], axis=1), seg)
         for c in range(0, D_NORMED, LANES)], axis=1)
    normed = (raw * lax.rsqrt(ss * (1.0 / DH) + EPS)) * qkn_ref[...]
    gq = normed[:, :DQ_GQA]
    gk = normed[:, DQ_GQA:]

    if rope:
        cos = cos_ref[...]
        sin = sin_ref[...]
        lane = lax.broadcasted_iota(jnp.int32, (x.shape[0], LANES), 1)
        first = (lane % 32) < 16
        first4 = jnp.concatenate([first] * 4, axis=1)
        first2 = jnp.concatenate([first] * 2, axis=1)
        dq = _rope(dq, cos, sin, first4)
        dk = _rope(dk, cos, sin, first4)
        gq = _rope(gq, cos, sin, first4)
        gk = _rope(gk, cos, sin, first2)

    q_ref[...] = jnp.concatenate([dq * SCORE_SCALE, gq * SCORE_SCALE], axis=1).astype(BF16)
    k_ref[...] = jnp.concatenate([dk, gk], axis=1).astype(BF16)
    vt_ref[...] = jnp.concatenate([dv, gv], axis=1).T.astype(BF16)

    if emit_f32:
        dkt_ref, dv4_ref, gkt_ref, gvt_ref = refs[pos:pos + 4]
        tm = x.shape[0]
        seq = dkt_ref.shape[2]
        lane = lax.broadcasted_iota(jnp.int32, (tm, LANES), 1)
        low = lane < DH
        dkt = dk.T
        gkt = jnp.where(low, gk[:, :LANES], gk[:, LANES:]).T
        gvt = jnp.where(low, gv[:, :LANES], gv[:, LANES:]).T
        for b in range(tm // seq):
            dkt_ref[b] = dkt[:, b * seq:(b + 1) * seq]
            gkt_ref[b] = gkt[:, b * seq:(b + 1) * seq]
            gvt_ref[b] = gvt[:, b * seq:(b + 1) * seq]
        for hd in range(H_DIFF):
            dv4_ref[pl.ds(hd, tm, stride=H_DIFF), :] = dv[:, hd * LANES:(hd + 1) * LANES]


def _proj_call(x, mod, row_map, gain, w_ext, seg, qkn, *, rope_tables, cache_seq, name):
    t = x.shape[0]
    emit_f32 = cache_seq is not None
    tm = PROJ_TILE
    row_spec = lambda w: pl.BlockSpec((tm, w), lambda i: (i, 0))
    n_ext = w_ext.shape[1]
    in_specs = [row_spec(D_MODEL), pl.BlockSpec((1, N_MOD, D_MODEL), row_map),
                _resident((1, D_MODEL)), _resident((D_MODEL, n_ext)),
                _resident((2 * LANES, LANES)), _resident((1, D_NORMED))]
    args = [x, mod, gain, w_ext, seg, qkn]
    if rope_tables is not None:
        cos, sin = rope_tables
        tiles_per_seq = cos.shape[0] // tm
        tab_spec = pl.BlockSpec((tm, LANES), lambda i: (i % tiles_per_seq, 0))
        in_specs += [tab_spec, tab_spec]
        args += [cos, sin]
    out_specs = [row_spec(D_MIX), row_spec(D_KV_ALL), pl.BlockSpec((D_KV_ALL, tm), lambda i: (0, i))]
    out_shape = [jax.ShapeDtypeStruct((t, D_MIX), BF16),
                 jax.ShapeDtypeStruct((t, D_KV_ALL), BF16),
                 jax.ShapeDtypeStruct((D_KV_ALL, t), BF16)]
    if emit_f32:
        assert tm % cache_seq == 0
        nb = tm // cache_seq
        slab = lambda rows: pl.BlockSpec((nb, rows, cache_seq), lambda i: (i, 0, 0))
        out_specs += [slab(DQ_DIFF), pl.BlockSpec((tm * H_DIFF, LANES), lambda i: (i, 0)),
                      slab(DKV_GQA), slab(DKV_GQA)]
        out_shape += [jax.ShapeDtypeStruct((t // cache_seq, DQ_DIFF, cache_seq), F32),
                      jax.ShapeDtypeStruct((t * H_DIFF, LANES), F32),
                      jax.ShapeDtypeStruct((t // cache_seq, DKV_GQA, cache_seq), F32),
                      jax.ShapeDtypeStruct((t // cache_seq, DKV_GQA, cache_seq), F32)]
    return pl.pallas_call(
        functools.partial(_proj_kernel, rope=rope_tables is not None, emit_f32=emit_f32),
        grid=(t // tm,),
        in_specs=in_specs,
        out_specs=out_specs,
        out_shape=out_shape,
        compiler_params=pltpu.CompilerParams(vmem_limit_bytes=VMEM_LIMIT_BYTES),
        name=name,
    )(*args)


def _lambda(lamv_ref):
    lamv = lamv_ref[...]
    return (jnp.exp(jnp.sum(lamv[0:1] * lamv[1:2], axis=-1, keepdims=True))
            - jnp.exp(jnp.sum(lamv[2:3] * lamv[3:4], axis=-1, keepdims=True)) + LAMBDA_INIT)


def _fold_rows(x, op):
    return op(x.reshape(x.shape[0] // SUBLANES, SUBLANES, x.shape[1]), axis=0)


def _value_rows(kvc):
    return slice(kvc * LANES, (kvc + 1) * LANES)


def _combine_heads(is_diff, ot_a, ot_b, lam, subln):
    if is_diff:
        o = (ot_a - lam * ot_b).T
        return (_rms(o) * subln) * (1.0 - LAMBDA_INIT)
    return jnp.concatenate([ot_a[:DH], ot_b[DH:]], axis=0).T


def _attn_kernel(q_ref, k_ref, vt_ref, lamv_ref, subln_ref, od_ref, og_ref, *, items, maps_per_group):
    tq = q_ref.shape[1]
    low = lax.broadcasted_iota(jnp.int32, (tq, LANES), 1) < DH

    class SoftmaxMap:
        def __init__(self, qm, kvc):
            self.qm = qm
            self.cs = slice(kvc * LANES, (kvc + 1) * LANES)
            self.vrows = _value_rows(kvc)

        def score_pass(self):
            self.s = _dot_nt(k_ref[0, :, self.cs], self.qm)
            self.m = _fold_rows(self.s, jnp.max).max(axis=0, keepdims=True)

        def pv_pass(self):
            e = jnp.exp2(self.s - self.m)
            tot = _fold_rows(e, jnp.sum).sum(axis=0, keepdims=True)
            self.ot = _dot(vt_ref[self.vrows, :], e.astype(BF16)) * (1.0 / tot)

    maps = []
    for is_diff, qc, kvc, _ in items:
        q = q_ref[0, :, qc * LANES:(qc + 1) * LANES]
        zero = jnp.zeros_like(q)
        maps += [SoftmaxMap(jnp.where(low, q, zero), kvc), SoftmaxMap(jnp.where(low, zero, q), kvc)]

    groups = [maps[g:g + maps_per_group] for g in range(0, len(maps), maps_per_group)]
    for stage in range(len(groups) + 1):
        if stage > 0:
            for mp in groups[stage - 1]:
                mp.pv_pass()
        if stage < len(groups):
            for mp in groups[stage]:
                mp.score_pass()

    lam = _lambda(lamv_ref)
    for n, (is_diff, _, _, oc) in enumerate(items):
        o = _combine_heads(is_diff, maps[2 * n].ot, maps[2 * n + 1].ot, lam, subln_ref[...])
        out_ref = od_ref if is_diff else og_ref
        out_ref[0, :, oc * LANES:(oc + 1) * LANES] = o.astype(BF16)


def _attn_loop_kernel(q_ref, kn_ref, vtn_ref, kc_ref, vtc_ref, lamv_ref, subln_ref, out_ref, *scratch,
                      is_diff, kv_cols):
    tq = ATTN_TILE_Q
    n_tiles = q_ref.shape[1] // tq
    n_groups = len(kv_cols)
    s_refs = scratch[:n_groups]
    m_refs = scratch[n_groups:]
    t_new = kn_ref.shape[1]
    segs = [(kn_ref, vtn_ref, 0), (kc_ref, vtc_ref, t_new)]
    chunks = []
    for si, (k_ref, _, base) in enumerate(segs):
        n = k_ref.shape[1]
        chunks += [(si, c0, min(n, c0 + ATTN_CHUNK_K), base + c0) for c0 in range(0, n, ATTN_CHUNK_K)]
    n_chunks = len(chunks)
    per_tile = n_groups * n_chunks
    lead = n_chunks + ATTN_SKEW
    assert 0 < ATTN_SKEW < n_chunks and lead < per_tile

    low = lax.broadcasted_iota(jnp.int32, (tq, LANES), 1) < DH
    lam = _lambda(lamv_ref)

    def rows(tile):
        return pl.ds(pl.multiple_of(tile * tq, tq), tq)

    def masked_q(tile, g):
        q = q_ref[0, rows(tile), g * LANES:(g + 1) * LANES]
        zero = jnp.zeros_like(q)
        return jnp.concatenate([jnp.where(low, q, zero), jnp.where(low, zero, q)], axis=0)

    def score_unit(qm, g, ci, mpart):
        si, c0, c1, r0 = chunks[ci]
        cs = slice(kv_cols[g] * LANES, (kv_cols[g] + 1) * LANES)
        s = _dot_nt(segs[si][0][0, c0:c1, cs], qm)
        s_refs[g][r0:r0 + c1 - c0, :] = s
        part = _fold_rows(s, jnp.max)
        mpart = part if mpart is None else jnp.maximum(mpart, part)
        if ci == n_chunks - 1:
            m_refs[g][...] = mpart.max(axis=0, keepdims=True)
            return None
        return mpart

    def pv_unit(g, ci, m, acc, lpart):
        si, c0, c1, r0 = chunks[ci]
        e = jnp.exp2(s_refs[g][r0:r0 + c1 - c0, :] - m)
        part = _fold_rows(e, jnp.sum)
        pv = _dot(segs[si][1][_value_rows(kv_cols[g]), c0:c1], e.astype(BF16))
        return (pv if acc is None else acc + pv), (part if lpart is None else lpart + part)

    def emit(g, tile, acc, lpart):
        ot = acc * (1.0 / lpart.sum(axis=0, keepdims=True))
        o = _combine_heads(is_diff, ot[:, :tq], ot[:, tq:], lam, subln_ref[...])
        out_ref[0, rows(tile), g * LANES:(g + 1) * LANES] = o.astype(BF16)

    mpart = None
    qm = None
    for u in range(lead):
        g, ci = divmod(u, n_chunks)
        if ci == 0:
            qm = masked_q(0, g)
        mpart = score_unit(qm, g, ci, mpart)

    def body(t, carried_mpart):
        nxt = jnp.minimum(t + 1, n_tiles - 1)
        a_state = {}
        a_state[(lead // per_tile, (lead // n_chunks) % n_groups)] = (None, carried_mpart)
        acc = lpart = m = None
        for j in range(per_tile):
            b_g, b_ci = divmod(j, n_chunks)
            if b_ci == 0:
                m = m_refs[b_g][...]
                acc = lpart = None
            acc, lpart = pv_unit(b_g, b_ci, m, acc, lpart)
            if b_ci == n_chunks - 1:
                emit(b_g, t, acc, lpart)

            a_off, a_rem = divmod(j + lead, per_tile)
            a_g, a_ci = divmod(a_rem, n_chunks)
            a_qm, a_mpart = a_state.get((a_off, a_g), (None, None))
            if a_qm is None:
                a_qm = masked_q(t if a_off == 0 else nxt, a_g)
            a_state[(a_off, a_g)] = (a_qm, score_unit(a_qm, a_g, a_ci, a_mpart))

        last = per_tile - 1 + lead
        return a_state[(last // per_tile, (last % per_tile) // n_chunks)][1]

    lax.fori_loop(0, n_tiles, body, mpart)


def _attn_sample_call(q, k, vt, kc, vtc, lamv, subln, *, is_diff, name):
    b, t, _ = q.shape
    tc = kc.shape[1]
    qw = ATTN_COLS_PER_STEP * LANES
    nblk = DQ_DIFF // qw
    if is_diff:
        kw = qw
        q0 = 0
        kv0 = 0
        kv_cols = tuple(range(ATTN_COLS_PER_STEP))
    else:
        assert ATTN_COLS_PER_STEP * (LANES // DH) == G_GQA
        kw = LANES
        q0 = DQ_DIFF // qw
        kv0 = DQ_DIFF // kw
        kv_cols = (0,) * ATTN_COLS_PER_STEP
    in_specs = [
        pl.BlockSpec((1, t, qw), lambda bi, j: (bi, 0, q0 + j)),
        pl.BlockSpec((1, t, kw), lambda bi, j: (bi, 0, kv0 + j)),
        pl.BlockSpec((kw, t), lambda bi, j: (kv0 + j, bi)),
        pl.BlockSpec((1, tc, kw), lambda bi, j: (bi, 0, kv0 + j)),
        pl.BlockSpec((kw, tc), lambda bi, j: (kv0 + j, bi)),
        pl.BlockSpec((4, DH), lambda bi, j: (0, 0)),
        pl.BlockSpec((1, LANES), lambda bi, j: (0, 0)),
    ]
    return pl.pallas_call(
        functools.partial(_attn_loop_kernel, is_diff=is_diff, kv_cols=kv_cols),
        grid=(b, nblk),
        in_specs=in_specs,
        out_specs=pl.BlockSpec((1, t, qw), lambda bi, j: (bi, 0, j)),
        out_shape=jax.ShapeDtypeStruct((b, t, nblk * qw), BF16),
        scratch_shapes=([pltpu.VMEM((t + tc, 2 * ATTN_TILE_Q), F32)] * ATTN_COLS_PER_STEP
                        + [pltpu.VMEM((1, 2 * ATTN_TILE_Q), F32)] * ATTN_COLS_PER_STEP),
        compiler_params=pltpu.CompilerParams(vmem_limit_bytes=VMEM_LIMIT_BYTES),
        name=name,
    )(q, k, vt, kc, vtc, lamv, subln)


def _attn_ctx_call(q, k, vt, lamv, subln, *, name):
    b, t, _ = q.shape
    ncol = DQ_DIFF // LANES
    items = tuple((True, j, j, j) for j in range(ncol)) + tuple(
        (False, ncol + j, ncol + j // (LANES // DH), j) for j in range(ncol))
    whole = lambda w: pl.BlockSpec((1, t, w), lambda bi: (bi, 0, 0))
    return pl.pallas_call(
        functools.partial(_attn_kernel, items=items, maps_per_group=ATTN_CTX_MAPS_PER_GROUP),
        grid=(b,),
        in_specs=[whole(D_MIX), whole(D_KV_ALL),
                  pl.BlockSpec((D_KV_ALL, t), lambda bi: (0, bi)),
                  pl.BlockSpec((4, DH), lambda bi: (0, 0)),
                  pl.BlockSpec((1, LANES), lambda bi: (0, 0))],
        out_specs=[whole(DQ_DIFF), whole(DQ_GQA)],
        out_shape=[jax.ShapeDtypeStruct((b, t, DQ_DIFF), BF16),
                   jax.ShapeDtypeStruct((b, t, DQ_GQA), BF16)],
        compiler_params=pltpu.CompilerParams(vmem_limit_bytes=VMEM_LIMIT_BYTES),
        name=name,
    )(q, k, vt, lamv, subln)


def _rope_tables(n_tokens):
    t = np.arange(n_tokens)
    row = (t // GRID_W).astype(np.float32)
    col = (t % GRID_W).astype(np.float32)
    half = DH // 2
    inv = np.float32(ROPE_THETA) ** (-(np.arange(0, half, 2, dtype=np.float32) / np.float32(half)))
    ang_r = row[:, None] * inv
    ang_c = col[:, None] * inv
    cos = np.concatenate([np.cos(ang_r)] * 2 + [np.cos(ang_c)] * 2, axis=1)
    sin = np.concatenate([-np.sin(ang_r), np.sin(ang_r), -np.sin(ang_c), np.sin(ang_c)], axis=1)
    reps = LANES // DH
    return (jnp.asarray(np.concatenate([cos] * reps, axis=1), F32),
            jnp.asarray(np.concatenate([sin] * reps, axis=1), F32))


def _dup_heads(a):
    parts = []
    for n in range(H_KV):
        head = a[..., n * DH:(n + 1) * DH]
        parts += [head, head]
    return jnp.concatenate(parts, axis=-1)


def kernel(x_prompt, x_sample, c, cache_diff_k, cache_diff_v, cache_gqa_k, cache_gqa_v, c_ctx, w_ada, b_ada, norm_ff1, w_ff1_gu, w_ff1_down, norm_mix, w_in, q_norm, k_norm, lambda_q1, lambda_k1, lambda_q2, lambda_k2, subln, w_out, norm_ff2, w_ff2_gu, w_ff2_down, final_norm):
    assert w_ada.shape[0] == 1, "single trunk layer"
    bc, tc_, _ = x_prompt.shape
    bs, ts, _ = x_sample.shape
    tpast = cache_diff_k.shape[2]
    assert bs + 1 <= MOD_ROWS
    ctx_row = bs

    cvecs = jnp.concatenate([c, c_ctx[None], jnp.zeros((MOD_ROWS - bs - 1, D_MODEL), F32)], axis=0)
    wgu1 = w_ff1_gu[0].astype(BF16)
    wd1 = w_ff1_down[0].astype(BF16)
    wgu2 = w_ff2_gu[0].astype(BF16)
    wd2 = w_ff2_down[0].astype(BF16)
    wout = w_out[0].astype(BF16)
    wi = w_in[0]
    n_qkvq = 3 * DQ_DIFF + DQ_GQA
    w_ext = jnp.concatenate(
        [wi[:, :n_qkvq], _dup_heads(wi[:, n_qkvq:n_qkvq + DKV_GQA]), _dup_heads(wi[:, n_qkvq + DKV_GQA:])],
        axis=1).astype(BF16)
    seg128 = (jnp.arange(LANES)[:, None] // DH == jnp.arange(LANES)[None, :] // DH).astype(BF16)
    seg = jnp.concatenate([seg128, seg128], axis=0)
    qkn = jnp.concatenate([jnp.tile(q_norm[0], H_GQA), jnp.tile(k_norm[0], 2 * H_KV)])[None]
    lamv = jnp.stack([lambda_q1[0], lambda_k1[0], lambda_q2[0], lambda_k2[0]])
    g1 = norm_ff1
    gm = norm_mix
    g2 = norm_ff2
    fg = final_norm[None]
    cos, sin = _rope_tables(ts)
    kcache = jnp.concatenate([cache_diff_k[:, 0].reshape(bs, tpast, DQ_DIFF),
                              _dup_heads(cache_gqa_k[:, 0].reshape(bs, tpast, DKV_GQA))], axis=-1).astype(BF16)
    vcache = jnp.concatenate([cache_diff_v[:, 0].reshape(bs, tpast, DQ_DIFF),
                              _dup_heads(cache_gqa_v[:, 0].reshape(bs, tpast, DKV_GQA))], axis=-1).astype(BF16)
    vcache_t = vcache.transpose(2, 0, 1).reshape(D_KV_ALL, bs * tpast)

    mod = _mod_call(cvecs, w_ada[0], b_ada).reshape(MOD_ROWS, N_MOD, D_MODEL)

    ctx_map = lambda i: (ctx_row, 0, 0)
    xc = x_prompt.reshape(bc * tc_, D_MODEL)
    xc = _ffn_call(xc, mod, ctx_map, g1, wgu1, wd1, sub=0, name="ffn1_ctx")
    qc, kc, vtc, dkt32, dv4, gkt32, gvt32 = _proj_call(
        xc, mod, ctx_map, gm, w_ext, seg, qkn, rope_tables=None, cache_seq=tc_, name="proj_ctx")
    odc, ogc = _attn_ctx_call(qc.reshape(bc, tc_, D_MIX), kc.reshape(bc, tc_, D_KV_ALL), vtc,
                              lamv, subln, name="attn_ctx")
    yc = _ffn_call(xc, mod, ctx_map, g2, wgu2, wd2, sub=2,
                   pre=(odc.reshape(bc * tc_, DQ_DIFF), ogc.reshape(bc * tc_, DQ_GQA), wout),
                   final_gain=fg, name="ffn2_ctx")

    ffn_tiles = ts // FFN_TILE
    proj_tiles = ts // PROJ_TILE
    xs = x_sample.reshape(bs * ts, D_MODEL)
    xs = _ffn_call(xs, mod, lambda i: (i // ffn_tiles, 0, 0), g1, wgu1, wd1, sub=0, name="ffn1_smp")
    qs, ks, vts = _proj_call(xs, mod, lambda i: (i // proj_tiles, 0, 0), gm, w_ext, seg, qkn,
                             rope_tables=(cos, sin), cache_seq=None, name="proj_smp")
    qs = qs.reshape(bs, ts, D_MIX)
    ks = ks.reshape(bs, ts, D_KV_ALL)
    ods = _attn_sample_call(qs, ks, vts, kcache, vcache_t, lamv, subln, is_diff=True, name="attn_diff_smp")
    ogs = _attn_sample_call(qs, ks, vts, kcache, vcache_t, lamv, subln, is_diff=False, name="attn_gqa_smp")
    ys = _ffn_call(xs, mod, lambda i: (i // ffn_tiles, 0, 0), g2, wgu2, wd2, sub=2,
                   pre=(ods.reshape(bs * ts, DQ_DIFF), ogs.reshape(bs * ts, DQ_GQA), wout),
                   final_gain=fg, name="ffn2_smp")

    return (yc.reshape(bc, tc_, D_MODEL),
            ys.reshape(bs, ts, D_MODEL),
            dkt32.reshape(bc, 1, H_DIFF, 2, DH, tc_).transpose(0, 1, 5, 2, 3, 4),
            dv4.reshape(bc, 1, tc_, H_DIFF, 2 * DH),
            gkt32.reshape(bc, 1, H_KV, DH, tc_).transpose(0, 1, 4, 2, 3),
            gvt32.reshape(bc, 1, H_KV, DH, tc_).transpose(0, 1, 4, 2, 3))
```

```python
import functools
import math

import jax
import jax.numpy as jnp
import numpy as np
from jax import lax
from jax.experimental import pallas as pl
from jax.experimental.pallas import tpu as pltpu

F32 = jnp.float32
BF16 = jnp.bfloat16

D_MODEL = 1024
N_MOD = 9
D_FF = 2816
H_DIFF = 4
DH = 64
H_GQA = 8
H_KV = 2
G_GQA = H_GQA // H_KV
GRID_W = 64
ROPE_THETA = 10000.0
EPS = 1e-6
LAMBDA_INIT = 0.8 - 0.6 * math.exp(-0.3 * 0)
DQ_DIFF = H_DIFF * 2 * DH
DQ_GQA = H_GQA * DH
DKV_GQA = H_KV * DH
D_IN = 3 * DQ_DIFF + DQ_GQA + 2 * DKV_GQA
D_MIX = DQ_DIFF + DQ_GQA
D_KV_ALL = DQ_DIFF + 2 * DKV_GQA
D_NORMED = DQ_GQA + 2 * DKV_GQA
SCORE_SCALE = DH ** -0.5 * math.log2(math.e)

LANES = 128
SUBLANES = 8
BF16_ROWS = 16
VMEM_LIMIT_BYTES = 60000 * 1024

MOD_ROWS = 8
MOD_TILE_N = 1152
FFN_TILE = 512
PROJ_TILE = 1024
ATTN_TILE_Q = 256
ATTN_CHUNK_K = 512
ATTN_SKEW = 2
ATTN_COLS_PER_STEP = 2
ATTN_CTX_MAPS_PER_GROUP = 8


def _sigmoid(x):
    return 1.0 / (1.0 + jnp.exp(-x))


def _rms(x):
    return x * lax.rsqrt(jnp.mean(x * x, axis=-1, keepdims=True) + EPS)


def _dot(a, b):
    return jnp.dot(a, b, preferred_element_type=F32)


def _dot_nt(a, b):
    return lax.dot_general(a, b, (((1,), (1,)), ((), ())), preferred_element_type=F32)


def _resident(shape):
    return pl.BlockSpec(shape, lambda *_: (0,) * len(shape), pipeline_mode=pl.Buffered(1))


def _mod_kernel(c_ref, w_ref, b_ref, o_ref):
    c = c_ref[...]
    s = c * _sigmoid(c)
    o_ref[...] = _dot(s.astype(BF16), w_ref[...].astype(BF16)) + b_ref[...]


def _mod_call(cvecs, w_ada, b_ada):
    n = w_ada.shape[1]
    return pl.pallas_call(
        _mod_kernel,
        grid=(n // MOD_TILE_N,),
        in_specs=[
            pl.BlockSpec((MOD_ROWS, D_MODEL), lambda j: (0, 0)),
            pl.BlockSpec((D_MODEL, MOD_TILE_N), lambda j: (0, j)),
            pl.BlockSpec((1, MOD_TILE_N), lambda j: (0, j)),
        ],
        out_specs=pl.BlockSpec((MOD_ROWS, MOD_TILE_N), lambda j: (0, j)),
        out_shape=jax.ShapeDtypeStruct((MOD_ROWS, n), F32),
        name="mod",
    )(cvecs, w_ada, b_ada)


def _ffn_kernel(*refs, sub, pre, final, n_side):
    refs = list(refs)
    x_ref, mod_ref = refs[:2]
    pos = 2
    if pre:
        od_ref, og_ref, wout_ref = refs[pos:pos + 3]
        pos += 3
    g_ref, wgu_ref, wd_ref = refs[pos:pos + 3]
    pos += 3
    if final:
        fg_ref = refs[pos]
        pos += 1
    side_in = refs[pos:pos + n_side]
    o_ref = refs[pos + n_side]
    side_out = refs[pos + n_side + 1:]

    x = x_ref[...]
    mod = mod_ref[0]
    if pre:
        o = jnp.concatenate([od_ref[...], og_ref[...]], axis=1)
        x = x + mod[5:6] * _dot(o, wout_ref[...])
    shift = mod[3 * sub:3 * sub + 1]
    scale = mod[3 * sub + 1:3 * sub + 2]
    gate = mod[3 * sub + 2:3 * sub + 3]
    h = (_rms(x) * g_ref[...]) * (1.0 + scale) + shift
    gu = _dot(h.astype(BF16), wgu_ref[...])
    g = gu[:, :D_FF]
    u = gu[:, D_FF:]
    act = (g * _sigmoid(g)) * u
    y = _dot(act.astype(BF16), wd_ref[...])
    x = x + (0.5 * gate) * y
    if final:
        x = _rms(x) * fg_ref[...]
    o_ref[...] = x
    for src, dst in zip(side_in, side_out):
        dst[...] = src[...].astype(BF16)


def _ffn_call(x, mod, row_map, gain, wgu, wd, *, sub, pre=None, final_gain=None, side_casts=(), name):
    t = x.shape[0]
    tm = FFN_TILE
    steps = t // tm
    row_spec = lambda w: pl.BlockSpec((tm, w), lambda i: (i, 0))
    in_specs = [row_spec(D_MODEL), pl.BlockSpec((1, N_MOD, D_MODEL), row_map)]
    args = [x, mod]
    if pre is not None:
        od, og, wout = pre
        in_specs += [row_spec(DQ_DIFF), row_spec(DQ_GQA), _resident((D_MIX, D_MODEL))]
        args += [od, og, wout]
    in_specs += [_resident((1, D_MODEL)), _resident((D_MODEL, 2 * D_FF)), _resident((D_FF, D_MODEL))]
    args += [gain, wgu, wd]
    if final_gain is not None:
        in_specs.append(_resident((1, D_MODEL)))
        args.append(final_gain)
    out_specs = [row_spec(D_MODEL)]
    out_shape = [jax.ShapeDtypeStruct((t, D_MODEL), F32)]
    for w in side_casts:
        rows, cols = w.shape
        assert rows % (steps * BF16_ROWS) == 0
        spec = pl.BlockSpec((rows // steps, cols), lambda i: (i, 0))
        in_specs.append(spec)
        args.append(w)
        out_specs.append(spec)
        out_shape.append(jax.ShapeDtypeStruct((rows, cols), BF16))
    outs = pl.pallas_call(
        functools.partial(_ffn_kernel, sub=sub, pre=pre is not None, final=final_gain is not None,
                          n_side=len(side_casts)),
        grid=(steps,),
        in_specs=in_specs,
        out_specs=out_specs,
        out_shape=out_shape,
        compiler_params=pltpu.CompilerParams(vmem_limit_bytes=VMEM_LIMIT_BYTES),
        name=name,
    )(*args)
    return outs if side_casts else outs[0]


def _rope(x, cos, sin, first_of_pair):
    w = x.shape[1]
    partner = jnp.where(first_of_pair, pltpu.roll(x, w - 16, 1), pltpu.roll(x, 16, 1))
    reps = w // LANES
    cos_w = jnp.concatenate([cos] * reps, axis=1) if reps > 1 else cos
    sin_w = jnp.concatenate([sin] * reps, axis=1) if reps > 1 else sin
    return x * cos_w + partner * sin_w


def _proj_kernel(*refs, rope, emit_f32):
    refs = list(refs)
    x_ref, mod_ref, g_ref, w_ref, seg_ref, qkn_ref = refs[:6]
    pos = 6
    if rope:
        cos_ref, sin_ref = refs[pos:pos + 2]
        pos += 2
    q_ref, k_ref, vt_ref = refs[pos:pos + 3]
    pos += 3

    x = x_ref[...]
    mod = mod_ref[0]
    h = (_rms(x) * g_ref[...]) * (1.0 + mod[4:5]) + mod[3:4]
    qkv = _dot(h.astype(BF16), w_ref[...])
    dq = qkv[:, 0:DQ_DIFF]
    dk = qkv[:, DQ_DIFF:2 * DQ_DIFF]
    dv = qkv[:, 2 * DQ_DIFF:3 * DQ_DIFF]
    n0 = 3 * DQ_DIFF
    raw = qkv[:, n0:n0 + D_NORMED]
    gv = qkv[:, n0 + D_NORMED:]

    sq = raw * raw
    hi = sq.astype(BF16)
    lo = (sq - hi.astype(F32)).astype(BF16)
    seg = seg_ref[...]
    ss = jnp.concatenate(
        [_dot(jnp.concatenate([hi[:, c:c + LANES], lo[:, c:c + LANES]], axis=1), seg)
         for c in range(0, D_NORMED, LANES)], axis=1)
    normed = (raw * lax.rsqrt(ss * (1.0 / DH) + EPS)) * qkn_ref[...]
    gq = normed[:, :DQ_GQA]
    gk = normed[:, DQ_GQA:]

    if rope:
        cos = cos_ref[...]
        sin = sin_ref[...]
        lane = lax.broadcasted_iota(jnp.int32, (x.shape[0], LANES), 1)
        first = (lane % 32) < 16
        first4 = jnp.concatenate([first] * 4, axis=1)
        first2 = jnp.concatenate([first] * 2, axis=1)
        dq = _rope(dq, cos, sin, first4)
        dk = _rope(dk, cos, sin, first4)
        gq = _rope(gq, cos, sin, first4)
        gk = _rope(gk, cos, sin, first2)

    q_ref[...] = jnp.concatenate([dq * SCORE_SCALE, gq * SCORE_SCALE], axis=1).astype(BF16)
    k_ref[...] = jnp.concatenate([dk, gk], axis=1).astype(BF16)
    vt_ref[...] = jnp.concatenate([dv, gv], axis=1).T.astype(BF16)

    if emit_f32:
        dkt_ref, dv4_ref, gkt_ref, gvt_ref = refs[pos:pos + 4]
        tm = x.shape[0]
        seq = dkt_ref.shape[2]
        lane = lax.broadcasted_iota(jnp.int32, (tm, LANES), 1)
        low = lane < DH
        dkt = dk.T
        gkt = jnp.where(low, gk[:, :LANES], gk[:, LANES:]).T
        gvt = jnp.where(low, gv[:, :LANES], gv[:, LANES:]).T
        for b in range(tm // seq):
            dkt_ref[b] = dkt[:, b * seq:(b + 1) * seq]
            gkt_ref[b] = gkt[:, b * seq:(b + 1) * seq]
            gvt_ref[b] = gvt[:, b * seq:(b + 1) * seq]
        for hd in range(H_DIFF):
            dv4_ref[pl.ds(hd, tm, stride=H_DIFF), :] = dv[:, hd * LANES:(hd + 1) * LANES]


def _proj_call(x, mod, row_map, gain, w_ext, seg, qkn, *, rope_tables, cache_seq, name):
    t = x.shape[0]
    emit_f32 = cache_seq is not None
    tm = PROJ_TILE
    row_spec = lambda w: pl.BlockSpec((tm, w), lambda i: (i, 0))
    n_ext = w_ext.shape[1]
    in_specs = [row_spec(D_MODEL), pl.BlockSpec((1, N_MOD, D_MODEL), row_map),
                _resident((1, D_MODEL)), _resident((D_MODEL, n_ext)),
                _resident((2 * LANES, LANES)), _resident((1, D_NORMED))]
    args = [x, mod, gain, w_ext, seg, qkn]
    if rope_tables is not None:
        cos, sin = rope_tables
        tiles_per_seq = cos.shape[0] // tm
        tab_spec = pl.BlockSpec((tm, LANES), lambda i: (i % tiles_per_seq, 0))
        in_specs += [tab_spec, tab_spec]
        args += [cos, sin]
    out_specs = [row_spec(D_MIX), row_spec(D_KV_ALL), pl.BlockSpec((D_KV_ALL, tm), lambda i: (0, i))]
    out_shape = [jax.ShapeDtypeStruct((t, D_MIX), BF16),
                 jax.ShapeDtypeStruct((t, D_KV_ALL), BF16),
                 jax.ShapeDtypeStruct((D_KV_ALL, t), BF16)]
    if emit_f32:
        assert tm % cache_seq == 0
        nb = tm // cache_seq
        slab = lambda rows: pl.BlockSpec((nb, rows, cache_seq), lambda i: (i, 0, 0))
        out_specs += [slab(DQ_DIFF), pl.BlockSpec((tm * H_DIFF, LANES), lambda i: (i, 0)),
                      slab(DKV_GQA), slab(DKV_GQA)]
        out_shape += [jax.ShapeDtypeStruct((t // cache_seq, DQ_DIFF, cache_seq), F32),
                      jax.ShapeDtypeStruct((t * H_DIFF, LANES), F32),
                      jax.ShapeDtypeStruct((t // cache_seq, DKV_GQA, cache_seq), F32),
                      jax.ShapeDtypeStruct((t // cache_seq, DKV_GQA, cache_seq), F32)]
    return pl.pallas_call(
        functools.partial(_proj_kernel, rope=rope_tables is not None, emit_f32=emit_f32),
        grid=(t // tm,),
        in_specs=in_specs,
        out_specs=out_specs,
        out_shape=out_shape,
        compiler_params=pltpu.CompilerParams(vmem_limit_bytes=VMEM_LIMIT_BYTES),
        name=name,
    )(*args)


def _lambda(lamv_ref):
    lamv = lamv_ref[...]
    return (jnp.exp(jnp.sum(lamv[0:1] * lamv[1:2], axis=-1, keepdims=True))
            - jnp.exp(jnp.sum(lamv[2:3] * lamv[3:4], axis=-1, keepdims=True)) + LAMBDA_INIT)


def _fold_rows(x, op):
    return op(x.reshape(x.shape[0] // SUBLANES, SUBLANES, x.shape[1]), axis=0)


def _value_rows(kvc):
    return slice(kvc * LANES, (kvc + 1) * LANES)


def _combine_heads(is_diff, ot_a, ot_b, lam, subln):
    if is_diff:
        o = (ot_a - lam * ot_b).T
        return (_rms(o) * subln) * (1.0 - LAMBDA_INIT)
    return jnp.concatenate([ot_a[:DH], ot_b[DH:]], axis=0).T


def _attn_kernel(q_ref, k_ref, vt_ref, lamv_ref, subln_ref, od_ref, og_ref, *, items, maps_per_group):
    tq = q_ref.shape[1]
    low = lax.broadcasted_iota(jnp.int32, (tq, LANES), 1) < DH

    class SoftmaxMap:
        def __init__(self, qm, kvc):
            self.qm = qm
            self.cs = slice(kvc * LANES, (kvc + 1) * LANES)
            self.vrows = _value_rows(kvc)

        def score_pass(self):
            self.s = _dot_nt(k_ref[0, :, self.cs], self.qm)
            self.m = _fold_rows(self.s, jnp.max).max(axis=0, keepdims=True)

        def pv_pass(self):
            e = jnp.exp2(self.s - self.m)
            tot = _fold_rows(e, jnp.sum).sum(axis=0, keepdims=True)
            self.ot = _dot(vt_ref[self.vrows, :], e.astype(BF16)) * (1.0 / tot)

    maps = []
    for is_diff, qc, kvc, _ in items:
        q = q_ref[0, :, qc * LANES:(qc + 1) * LANES]
        zero = jnp.zeros_like(q)
        maps += [SoftmaxMap(jnp.where(low, q, zero), kvc), SoftmaxMap(jnp.where(low, zero, q), kvc)]

    groups = [maps[g:g + maps_per_group] for g in range(0, len(maps), maps_per_group)]
    for stage in range(len(groups) + 1):
        if stage > 0:
            for mp in groups[stage - 1]:
                mp.pv_pass()
        if stage < len(groups):
            for mp in groups[stage]:
                mp.score_pass()

    lam = _lambda(lamv_ref)
    for n, (is_diff, _, _, oc) in enumerate(items):
        o = _combine_heads(is_diff, maps[2 * n].ot, maps[2 * n + 1].ot, lam, subln_ref[...])
        out_ref = od_ref if is_diff else og_ref
        out_ref[0, :, oc * LANES:(oc + 1) * LANES] = o.astype(BF16)


def _attn_loop_kernel(q_ref, kn_ref, vtn_ref, kc_ref, vtc_ref, lamv_ref, subln_ref, out_ref, *scratch,
                      is_diff, kv_cols):
    tq = ATTN_TILE_Q
    n_tiles = q_ref.shape[1] // tq
    n_groups = len(kv_cols)
    s_refs = scratch[:n_groups]
    m_refs = scratch[n_groups:]
    t_new = kn_ref.shape[1]
    segs = [(kn_ref, vtn_ref, 0), (kc_ref, vtc_ref, t_new)]
    chunks = []
    for si, (k_ref, _, base) in enumerate(segs):
        n = k_ref.shape[1]
        chunks += [(si, c0, min(n, c0 + ATTN_CHUNK_K), base + c0) for c0 in range(0, n, ATTN_CHUNK_K)]
    n_chunks = len(chunks)
    per_tile = n_groups * n_chunks
    lead = n_chunks + ATTN_SKEW
    assert 0 < ATTN_SKEW < n_chunks and lead < per_tile

    low = lax.broadcasted_iota(jnp.int32, (tq, LANES), 1) < DH
    lam = _lambda(lamv_ref)

    def rows(tile):
        return pl.ds(pl.multiple_of(tile * tq, tq), tq)

    def masked_q(tile, g):
        q = q_ref[0, rows(tile), g * LANES:(g + 1) * LANES]
        zero = jnp.zeros_like(q)
        return jnp.concatenate([jnp.where(low, q, zero), jnp.where(low, zero, q)], axis=0)

    def score_unit(qm, g, ci, mpart):
        si, c0, c1, r0 = chunks[ci]
        cs = slice(kv_cols[g] * LANES, (kv_cols[g] + 1) * LANES)
        s = _dot_nt(segs[si][0][0, c0:c1, cs], qm)
        s_refs[g][r0:r0 + c1 - c0, :] = s
        part = _fold_rows(s, jnp.max)
        mpart = part if mpart is None else jnp.maximum(mpart, part)
        if ci == n_chunks - 1:
            m_refs[g][...] = mpart.max(axis=0, keepdims=True)
            return None
        return mpart

    def pv_unit(g, ci, m, acc, lpart):
        si, c0, c1, r0 = chunks[ci]
        e = jnp.exp2(s_refs[g][r0:r0 + c1 - c0, :] - m)
        part = _fold_rows(e, jnp.sum)
        pv = _dot(segs[si][1][_value_rows(kv_cols[g]), c0:c1], e.astype(BF16))
        return (pv if acc is None else acc + pv), (part if lpart is None else lpart + part)

    def emit(g, tile, acc, lpart):
        ot = acc * (1.0 / lpart.sum(axis=0, keepdims=True))
        o = _combine_heads(is_diff, ot[:, :tq], ot[:, tq:], lam, subln_ref[...])
        out_ref[0, rows(tile), g * LANES:(g + 1) * LANES] = o.astype(BF16)

    mpart = None
    qm = None
    for u in range(lead):
        g, ci = divmod(u, n_chunks)
        if ci == 0:
            qm = masked_q(0, g)
        mpart = score_unit(qm, g, ci, mpart)

    def body(t, carried_mpart):
        nxt = jnp.minimum(t + 1, n_tiles - 1)
        a_state = {}
        a_state[(lead // per_tile, (lead // n_chunks) % n_groups)] = (None, carried_mpart)
        acc = lpart = m = None
        for j in range(per_tile):
            b_g, b_ci = divmod(j, n_chunks)
            if b_ci == 0:
                m = m_refs[b_g][...]
                acc = lpart = None
            acc, lpart = pv_unit(b_g, b_ci, m, acc, lpart)
            if b_ci == n_chunks - 1:
                emit(b_g, t, acc, lpart)

            a_off, a_rem = divmod(j + lead, per_tile)
            a_g, a_ci = divmod(a_rem, n_chunks)
            a_qm, a_mpart = a_state.get((a_off, a_g), (None, None))
            if a_qm is None:
                a_qm = masked_q(t if a_off == 0 else nxt, a_g)
            a_state[(a_off, a_g)] = (a_qm, score_unit(a_qm, a_g, a_ci, a_mpart))

        last = per_tile - 1 + lead
        return a_state[(last // per_tile, (last % per_tile) // n_chunks)][1]

    lax.fori_loop(0, n_tiles, body, mpart)


def _attn_sample_call(q, k, vt, kc, vtc, lamv, subln, *, is_diff, name):
    b, t, _ = q.shape
    tc = kc.shape[1]
    qw = ATTN_COLS_PER_STEP * LANES
    nblk = DQ_DIFF // qw
    if is_diff:
        kw = qw
        q0 = 0
        kv0 = 0
        kv_cols = tuple(range(ATTN_COLS_PER_STEP))
    else:
        assert ATTN_COLS_PER_STEP * (LANES // DH) == G_GQA
        kw = LANES
        q0 = DQ_DIFF // qw
        kv0 = DQ_DIFF // kw
        kv_cols = (0,) * ATTN_COLS_PER_STEP
    in_specs = [
        pl.BlockSpec((1, t, qw), lambda bi, j: (bi, 0, q0 + j)),
        pl.BlockSpec((1, t, kw), lambda bi, j: (bi, 0, kv0 + j)),
        pl.BlockSpec((kw, t), lambda bi, j: (kv0 + j, bi)),
        pl.BlockSpec((1, tc, kw), lambda bi, j: (bi, 0, kv0 + j)),
        pl.BlockSpec((kw, tc), lambda bi, j: (kv0 + j, bi)),
        pl.BlockSpec((4, DH), lambda bi, j: (0, 0)),
        pl.BlockSpec((1, LANES), lambda bi, j: (0, 0)),
    ]
    return pl.pallas_call(
        functools.partial(_attn_loop_kernel, is_diff=is_diff, kv_cols=kv_cols),
        grid=(b, nblk),
        in_specs=in_specs,
        out_specs=pl.BlockSpec((1, t, qw), lambda bi, j: (bi, 0, j)),
        out_shape=jax.ShapeDtypeStruct((b, t, nblk * qw), BF16),
        scratch_shapes=([pltpu.VMEM((t + tc, 2 * ATTN_TILE_Q), F32)] * ATTN_COLS_PER_STEP
                        + [pltpu.VMEM((1, 2 * ATTN_TILE_Q), F32)] * ATTN_COLS_PER_STEP),
        compiler_params=pltpu.CompilerParams(vmem_limit_bytes=VMEM_LIMIT_BYTES),
        name=name,
    )(q, k, vt, kc, vtc, lamv, subln)


def _attn_ctx_call(q, k, vt, lamv, subln, *, name):
    b, t, _ = q.shape
    ncol = DQ_DIFF // LANES
    items = tuple((True, j, j, j) for j in range(ncol)) + tuple(
        (False, ncol + j, ncol + j // (LANES // DH), j) for j in range(ncol))
    whole = lambda w: pl.BlockSpec((1, t, w), lambda bi: (bi, 0, 0))
    return pl.pallas_call(
        functools.partial(_attn_kernel, items=items, maps_per_group=ATTN_CTX_MAPS_PER_GROUP),
        grid=(b,),
        in_specs=[whole(D_MIX), whole(D_KV_ALL),
                  pl.BlockSpec((D_KV_ALL, t), lambda bi: (0, bi)),
                  pl.BlockSpec((4, DH), lambda bi: (0, 0)),
                  pl.BlockSpec((1, LANES), lambda bi: (0, 0))],
        out_specs=[whole(DQ_DIFF), whole(DQ_GQA)],
        out_shape=[jax.ShapeDtypeStruct((b, t, DQ_DIFF), BF16),
                   jax.ShapeDtypeStruct((b, t, DQ_GQA), BF16)],
        compiler_params=pltpu.CompilerParams(vmem_limit_bytes=VMEM_LIMIT_BYTES),
        name=name,
    )(q, k, vt, lamv, subln)


def _rope_tables(n_tokens):
    t = np.arange(n_tokens)
    row = (t // GRID_W).astype(np.float32)
    col = (t % GRID_W).astype(np.float32)
    half = DH // 2
    inv = np.float32(ROPE_THETA) ** (-(np.arange(0, half, 2, dtype=np.float32) / np.float32(half)))
    ang_r = row[:, None] * inv
    ang_c = col[:, None] * inv
    cos = np.concatenate([np.cos(ang_r)] * 2 + [np.cos(ang_c)] * 2, axis=1)
    sin = np.concatenate([-np.sin(ang_r), np.sin(ang_r), -np.sin(ang_c), np.sin(ang_c)], axis=1)
    reps = LANES // DH
    return (jnp.asarray(np.concatenate([cos] * reps, axis=1), F32),
            jnp.asarray(np.concatenate([sin] * reps, axis=1), F32))


def _dup_heads(a):
    parts = []
    for n in range(H_KV):
        head = a[..., n * DH:(n + 1) * DH]
        parts += [head, head]
    return jnp.concatenate(parts, axis=-1)


def kernel(x_prompt, x_sample, c, cache_diff_k, cache_diff_v, cache_gqa_k, cache_gqa_v, c_ctx, w_ada, b_ada, norm_ff1, w_ff1_gu, w_ff1_down, norm_mix, w_in, q_norm, k_norm, lambda_q1, lambda_k1, lambda_q2, lambda_k2, subln, w_out, norm_ff2, w_ff2_gu, w_ff2_down, final_norm):
    assert w_ada.shape[0] == 1, "single trunk layer"
    bc, tc_, _ = x_prompt.shape
    bs, ts, _ = x_sample.shape
    tpast = cache_diff_k.shape[2]
    assert bs + 1 <= MOD_ROWS
    ctx_row = bs

    cvecs = jnp.concatenate([c, c_ctx[None], jnp.zeros((MOD_ROWS - bs - 1, D_MODEL), F32)], axis=0)
    wgu1 = w_ff1_gu[0].astype(BF16)
    wd1 = w_ff1_down[0].astype(BF16)
    wi = w_in[0]
    n_qkvq = 3 * DQ_DIFF + DQ_GQA
    w_ext = jnp.concatenate(
        [wi[:, :n_qkvq], _dup_heads(wi[:, n_qkvq:n_qkvq + DKV_GQA]), _dup_heads(wi[:, n_qkvq + DKV_GQA:])],
        axis=1).astype(BF16)
    seg128 = (jnp.arange(LANES)[:, None] // DH == jnp.arange(LANES)[None, :] // DH).astype(BF16)
    seg = jnp.concatenate([seg128, seg128], axis=0)
    qkn = jnp.concatenate([jnp.tile(q_norm[0], H_GQA), jnp.tile(k_norm[0], 2 * H_KV)])[None]
    lamv = jnp.stack([lambda_q1[0], lambda_k1[0], lambda_q2[0], lambda_k2[0]])
    g1 = norm_ff1
    gm = norm_mix
    g2 = norm_ff2
    fg = final_norm[None]
    cos, sin = _rope_tables(ts)
    kcache = jnp.concatenate([cache_diff_k[:, 0].reshape(bs, tpast, DQ_DIFF),
                              _dup_heads(cache_gqa_k[:, 0].reshape(bs, tpast, DKV_GQA))], axis=-1).astype(BF16)
    vcache = jnp.concatenate([cache_diff_v[:, 0].reshape(bs, tpast, DQ_DIFF),
                              _dup_heads(cache_gqa_v[:, 0].reshape(bs, tpast, DKV_GQA))], axis=-1).astype(BF16)
    vcache_t = vcache.transpose(2, 0, 1).reshape(D_KV_ALL, bs * tpast)

    mod = _mod_call(cvecs, w_ada[0], b_ada).reshape(MOD_ROWS, N_MOD, D_MODEL)

    ctx_map = lambda i: (ctx_row, 0, 0)
    xc = x_prompt.reshape(bc * tc_, D_MODEL)
    xc, wgu2, wd2, wout = _ffn_call(xc, mod, ctx_map, g1, wgu1, wd1, sub=0,
                                    side_casts=(w_ff2_gu[0], w_ff2_down[0], w_out[0]), name="ffn1_ctx")
    qc, kc, vtc, dkt32, dv4, gkt32, gvt32 = _proj_call(
        xc, mod, ctx_map, gm, w_ext, seg, qkn, rope_tables=None, cache_seq=tc_, name="proj_ctx")
    odc, ogc = _attn_ctx_call(qc.reshape(bc, tc_, D_MIX), kc.reshape(bc, tc_, D_KV_ALL), vtc,
                              lamv, subln, name="attn_ctx")
    yc = _ffn_call(xc, mod, ctx_map, g2, wgu2, wd2, sub=2,
                   pre=(odc.reshape(bc * tc_, DQ_DIFF), ogc.reshape(bc * tc_, DQ_GQA), wout),
                   final_gain=fg, name="ffn2_ctx")

    ffn_tiles = ts // FFN_TILE
    proj_tiles = ts // PROJ_TILE
    xs = x_sample.reshape(bs * ts, D_MODEL)
    xs = _ffn_call(xs, mod, lambda i: (i // ffn_tiles, 0, 0), g1, wgu1, wd1, sub=0, name="ffn1_smp")
    qs, ks, vts = _proj_call(xs, mod, lambda i: (i // proj_tiles, 0, 0), gm, w_ext, seg, qkn,
                             rope_tables=(cos, sin), cache_seq=None, name="proj_smp")
    qs = qs.reshape(bs, ts, D_MIX)
    ks = ks.reshape(bs, ts, D_KV_ALL)
    ods = _attn_sample_call(qs, ks, vts, kcache, vcache_t, lamv, subln, is_diff=True, name="attn_diff_smp")
    ogs = _attn_sample_call(qs, ks, vts, kcache, vcache_t, lamv, subln, is_diff=False, name="attn_gqa_smp")
    ys = _ffn_call(xs, mod, lambda i: (i // ffn_tiles, 0, 0), g2, wgu2, wd2, sub=2,
                   pre=(ods.reshape(bs * ts, DQ_DIFF), ogs.reshape(bs * ts, DQ_GQA), wout),
                   final_gain=fg, name="ffn2_smp")

    return (yc.reshape(bc, tc_, D_MODEL),
            ys.reshape(bs, ts, D_MODEL),
            dkt32.reshape(bc, 1, H_DIFF, 2, DH, tc_).transpose(0, 1, 5, 2, 3, 4),
            dv4.reshape(bc, 1, tc_, H_DIFF, 2 * DH),
            gkt32.reshape(bc, 1, H_KV, DH, tc_).transpose(0, 1, 4, 2, 3),
            gvt32.reshape(bc, 1, H_KV, DH, tc_).transpose(0, 1, 4, 2, 3))
```

```python
import functools
import math

import jax
import jax.numpy as jnp
import numpy as np
from jax import lax
from jax.experimental import pallas as pl
from jax.experimental.pallas import tpu as pltpu

F32 = jnp.float32
BF16 = jnp.bfloat16

D_MODEL = 1024
N_MOD = 9
D_FF = 2816
H_DIFF = 4
DH = 64
H_GQA = 8
H_KV = 2
G_GQA = H_GQA // H_KV
GRID_W = 64
ROPE_THETA = 10000.0
EPS = 1e-6
LAMBDA_INIT = 0.8 - 0.6 * math.exp(-0.3 * 0)
DQ_DIFF = H_DIFF * 2 * DH
DQ_GQA = H_GQA * DH
DKV_GQA = H_KV * DH
ROPE_HALF = DH // 4
D_MIX = DQ_DIFF + DQ_GQA
D_KV_ALL = DQ_DIFF + 2 * DKV_GQA
D_NORMED = DQ_GQA + 2 * DKV_GQA
SCORE_SCALE = DH ** -0.5 * math.log2(math.e)

LANES = 128
SUBLANES = 8
BF16_ROWS = 16
VMEM_LIMIT_BYTES = 60000 * 1024

MOD_ROWS = 8
MOD_TILE_N = 4608
FFN_TILE = 512
PROJ_TILE = 1024
ATTN_TILE_Q = 256
ATTN_CHUNK_K = 512
ATTN_SKEW = 2
ATTN_COLS_PER_STEP = 2
ATTN_CTX_MAPS_PER_GROUP = 8


def _sigmoid(x):
    return 1.0 / (1.0 + jnp.exp(-x))


def _rms(x):
    return x * lax.rsqrt(jnp.mean(x * x, axis=-1, keepdims=True) + EPS)


def _dot(a, b):
    return jnp.dot(a, b, preferred_element_type=F32)


def _dot_nt(a, b):
    return lax.dot_general(a, b, (((1,), (1,)), ((), ())), preferred_element_type=F32)


def _resident(shape):
    return pl.BlockSpec(shape, lambda *_: (0,) * len(shape), pipeline_mode=pl.Buffered(1))


def _mod_kernel(c_ref, w_ref, b_ref, o_ref):
    c = c_ref[...]
    s = c * _sigmoid(c)
    o_ref[...] = _dot(s.astype(BF16), w_ref[...].astype(BF16)) + b_ref[...]


def _mod_call(cvecs, w_ada, b_ada):
    n = w_ada.shape[1]
    return pl.pallas_call(
        _mod_kernel,
        grid=(n // MOD_TILE_N,),
        in_specs=[
            pl.BlockSpec((MOD_ROWS, D_MODEL), lambda j: (0, 0)),
            pl.BlockSpec((D_MODEL, MOD_TILE_N), lambda j: (0, j)),
            pl.BlockSpec((1, MOD_TILE_N), lambda j: (0, j)),
        ],
        out_specs=pl.BlockSpec((MOD_ROWS, MOD_TILE_N), lambda j: (0, j)),
        out_shape=jax.ShapeDtypeStruct((MOD_ROWS, n), F32),
        name="mod",
    )(cvecs, w_ada, b_ada)


def _ffn_kernel(*refs, sub, pre, final, n_side):
    refs = list(refs)
    x_ref, mod_ref = refs[:2]
    pos = 2
    if pre:
        od_ref, og_ref, wout_ref = refs[pos:pos + 3]
        pos += 3
    g_ref, wgu_ref, wd_ref = refs[pos:pos + 3]
    pos += 3
    if final:
        fg_ref = refs[pos]
        pos += 1
    side_in = refs[pos:pos + n_side]
    o_ref = refs[pos + n_side]
    side_out = refs[pos + n_side + 1:]

    x = x_ref[...]
    mod = mod_ref[0]
    if pre:
        o = jnp.concatenate([od_ref[...], og_ref[...]], axis=1)
        x = x + mod[5:6] * _dot(o, wout_ref[...])
    shift = mod[3 * sub:3 * sub + 1]
    scale = mod[3 * sub + 1:3 * sub + 2]
    gate = mod[3 * sub + 2:3 * sub + 3]
    h = (_rms(x) * g_ref[...]) * (1.0 + scale) + shift
    gu = _dot(h.astype(BF16), wgu_ref[...])
    g = gu[:, :D_FF]
    u = gu[:, D_FF:]
    act = (g * _sigmoid(g)) * u
    y = _dot(act.astype(BF16), wd_ref[...])
    x = x + (0.5 * gate) * y
    if final:
        x = _rms(x) * fg_ref[...]
    o_ref[...] = x
    for src, dst in zip(side_in, side_out):
        dst[...] = src[...].astype(BF16)


def _ffn_call(x, mod, row_map, gain, wgu, wd, *, sub, pre=None, final_gain=None, side_casts=(), name):
    t = x.shape[0]
    tm = FFN_TILE
    steps = t // tm
    row_spec = lambda w: pl.BlockSpec((tm, w), lambda i: (i, 0))
    in_specs = [row_spec(D_MODEL), pl.BlockSpec((1, N_MOD, D_MODEL), row_map)]
    args = [x, mod]
    if pre is not None:
        od, og, wout = pre
        in_specs += [row_spec(DQ_DIFF), row_spec(DQ_GQA), _resident((D_MIX, D_MODEL))]
        args += [od, og, wout]
    in_specs += [_resident((1, D_MODEL)), _resident((D_MODEL, 2 * D_FF)), _resident((D_FF, D_MODEL))]
    args += [gain, wgu, wd]
    if final_gain is not None:
        in_specs.append(_resident((1, D_MODEL)))
        args.append(final_gain)
    out_specs = [row_spec(D_MODEL)]
    out_shape = [jax.ShapeDtypeStruct((t, D_MODEL), F32)]
    for w in side_casts:
        rows, cols = w.shape
        assert rows % (steps * BF16_ROWS) == 0
        spec = pl.BlockSpec((rows // steps, cols), lambda i: (i, 0))
        in_specs.append(spec)
        args.append(w)
        out_specs.append(spec)
        out_shape.append(jax.ShapeDtypeStruct((rows, cols), BF16))
    outs = pl.pallas_call(
        functools.partial(_ffn_kernel, sub=sub, pre=pre is not None, final=final_gain is not None,
                          n_side=len(side_casts)),
        grid=(steps,),
        in_specs=in_specs,
        out_specs=out_specs,
        out_shape=out_shape,
        compiler_params=pltpu.CompilerParams(vmem_limit_bytes=VMEM_LIMIT_BYTES),
        name=name,
    )(*args)
    return outs if side_casts else outs[0]


def _rope(x, cos, sin, first_of_pair):
    w = x.shape[1]
    partner = jnp.where(first_of_pair, pltpu.roll(x, w - ROPE_HALF, 1), pltpu.roll(x, ROPE_HALF, 1))
    reps = w // LANES
    cos_w = jnp.concatenate([cos] * reps, axis=1) if reps > 1 else cos
    sin_w = jnp.concatenate([sin] * reps, axis=1) if reps > 1 else sin
    return x * cos_w + partner * sin_w


def _proj_kernel(*refs, rope, emit_f32):
    refs = list(refs)
    x_ref, mod_ref, g_ref, w_ref, seg_ref, qkn_ref = refs[:6]
    pos = 6
    if rope:
        cos_ref, sin_ref = refs[pos:pos + 2]
        pos += 2
    q_ref, k_ref, vt_ref = refs[pos:pos + 3]
    pos += 3

    x = x_ref[...]
    mod = mod_ref[0]
    h = (_rms(x) * g_ref[...]) * (1.0 + mod[4:5]) + mod[3:4]
    qkv = _dot(h.astype(BF16), w_ref[...])
    dq = qkv[:, 0:DQ_DIFF]
    dk = qkv[:, DQ_DIFF:2 * DQ_DIFF]
    dv = qkv[:, 2 * DQ_DIFF:3 * DQ_DIFF]
    n0 = 3 * DQ_DIFF
    raw = qkv[:, n0:n0 + D_NORMED]
    gv = qkv[:, n0 + D_NORMED:]

    sq = raw * raw
    hi = sq.astype(BF16)
    lo = (sq - hi.astype(F32)).astype(BF16)
    seg = seg_ref[...]
    ss = jnp.concatenate(
        [_dot(jnp.concatenate([hi[:, c:c + LANES], lo[:, c:c + LANES]], axis=1), seg)
         for c in range(0, D_NORMED, LANES)], axis=1)
    normed = (raw * lax.rsqrt(ss * (1.0 / DH) + EPS)) * qkn_ref[...]
    gq = normed[:, :DQ_GQA]
    gk = normed[:, DQ_GQA:]

    if rope:
        cos = cos_ref[...]
        sin = sin_ref[...]
        lane = lax.broadcasted_iota(jnp.int32, (x.shape[0], LANES), 1)
        first = (lane % (2 * ROPE_HALF)) < ROPE_HALF
        first4 = jnp.concatenate([first] * 4, axis=1)
        first2 = jnp.concatenate([first] * 2, axis=1)
        dq = _rope(dq, cos, sin, first4)
        dk = _rope(dk, cos, sin, first4)
        gq = _rope(gq, cos, sin, first4)
        gk = _rope(gk, cos, sin, first2)

    q_ref[...] = jnp.concatenate([dq * SCORE_SCALE, gq * SCORE_SCALE], axis=1).astype(BF16)
    k_ref[...] = jnp.concatenate([dk, gk], axis=1).astype(BF16)
    vt_ref[...] = jnp.concatenate([dv, gv], axis=1).T.astype(BF16)

    if emit_f32:
        dkt_ref, dv4_ref, gkt_ref, gvt_ref = refs[pos:pos + 4]
        tm = x.shape[0]
        seq = dkt_ref.shape[2]
        lane = lax.broadcasted_iota(jnp.int32, (tm, LANES), 1)
        low = lane < DH
        dkt = dk.T
        gkt = jnp.where(low, gk[:, :LANES], gk[:, LANES:]).T
        gvt = jnp.where(low, gv[:, :LANES], gv[:, LANES:]).T
        for b in range(tm // seq):
            dkt_ref[b] = dkt[:, b * seq:(b + 1) * seq]
            gkt_ref[b] = gkt[:, b * seq:(b + 1) * seq]
            gvt_ref[b] = gvt[:, b * seq:(b + 1) * seq]
        for hd in range(H_DIFF):
            dv4_ref[pl.ds(hd, tm, stride=H_DIFF), :] = dv[:, hd * LANES:(hd + 1) * LANES]


def _proj_call(x, mod, row_map, gain, w_ext, seg, qkn, *, rope_tables, cache_seq, name):
    t = x.shape[0]
    emit_f32 = cache_seq is not None
    tm = PROJ_TILE
    row_spec = lambda w: pl.BlockSpec((tm, w), lambda i: (i, 0))
    n_ext = w_ext.shape[1]
    in_specs = [row_spec(D_MODEL), pl.BlockSpec((1, N_MOD, D_MODEL), row_map),
                _resident((1, D_MODEL)), _resident((D_MODEL, n_ext)),
                _resident((2 * LANES, LANES)), _resident((1, D_NORMED))]
    args = [x, mod, gain, w_ext, seg, qkn]
    if rope_tables is not None:
        cos, sin = rope_tables
        tiles_per_seq = cos.shape[0] // tm
        tab_spec = pl.BlockSpec((tm, LANES), lambda i: (i % tiles_per_seq, 0))
        in_specs += [tab_spec, tab_spec]
        args += [cos, sin]
    out_specs = [row_spec(D_MIX), row_spec(D_KV_ALL), pl.BlockSpec((D_KV_ALL, tm), lambda i: (0, i))]
    out_shape = [jax.ShapeDtypeStruct((t, D_MIX), BF16),
                 jax.ShapeDtypeStruct((t, D_KV_ALL), BF16),
                 jax.ShapeDtypeStruct((D_KV_ALL, t), BF16)]
    if emit_f32:
        assert tm % cache_seq == 0
        nb = tm // cache_seq
        slab = lambda rows: pl.BlockSpec((nb, rows, cache_seq), lambda i: (i, 0, 0))
        out_specs += [slab(DQ_DIFF), pl.BlockSpec((tm * H_DIFF, LANES), lambda i: (i, 0)),
                      slab(DKV_GQA), slab(DKV_GQA)]
        out_shape += [jax.ShapeDtypeStruct((t // cache_seq, DQ_DIFF, cache_seq), F32),
                      jax.ShapeDtypeStruct((t * H_DIFF, LANES), F32),
                      jax.ShapeDtypeStruct((t // cache_seq, DKV_GQA, cache_seq), F32),
                      jax.ShapeDtypeStruct((t // cache_seq, DKV_GQA, cache_seq), F32)]
    return pl.pallas_call(
        functools.partial(_proj_kernel, rope=rope_tables is not None, emit_f32=emit_f32),
        grid=(t // tm,),
        in_specs=in_specs,
        out_specs=out_specs,
        out_shape=out_shape,
        compiler_params=pltpu.CompilerParams(vmem_limit_bytes=VMEM_LIMIT_BYTES),
        name=name,
    )(*args)


def _lambda(lamv_ref):
    lamv = lamv_ref[...]
    return (jnp.exp(jnp.sum(lamv[0:1] * lamv[1:2], axis=-1, keepdims=True))
            - jnp.exp(jnp.sum(lamv[2:3] * lamv[3:4], axis=-1, keepdims=True)) + LAMBDA_INIT)


def _fold_rows(x, op):
    return op(x.reshape(x.shape[0] // SUBLANES, SUBLANES, x.shape[1]), axis=0)


def _value_rows(kvc):
    return slice(kvc * LANES, (kvc + 1) * LANES)


def _combine_heads(is_diff, ot_a, ot_b, lam, subln):
    if is_diff:
        o = (ot_a - lam * ot_b).T
        return (_rms(o) * subln) * (1.0 - LAMBDA_INIT)
    return jnp.concatenate([ot_a[:DH], ot_b[DH:]], axis=0).T


def _attn_kernel(q_ref, k_ref, vt_ref, lamv_ref, subln_ref, od_ref, og_ref, *, items, maps_per_group):
    tq = q_ref.shape[1]
    low = lax.broadcasted_iota(jnp.int32, (tq, LANES), 1) < DH

    class SoftmaxMap:
        def __init__(self, qm, kvc):
            self.qm = qm
            self.cs = slice(kvc * LANES, (kvc + 1) * LANES)
            self.vrows = _value_rows(kvc)

        def score_pass(self):
            self.s = _dot_nt(k_ref[0, :, self.cs], self.qm)
            self.m = _fold_rows(self.s, jnp.max).max(axis=0, keepdims=True)

        def pv_pass(self):
            e = jnp.exp2(self.s - self.m)
            tot = _fold_rows(e, jnp.sum).sum(axis=0, keepdims=True)
            self.ot = _dot(vt_ref[self.vrows, :], e.astype(BF16)) * (1.0 / tot)

    maps = []
    for is_diff, qc, kvc, _ in items:
        q = q_ref[0, :, qc * LANES:(qc + 1) * LANES]
        zero = jnp.zeros_like(q)
        maps += [SoftmaxMap(jnp.where(low, q, zero), kvc), SoftmaxMap(jnp.where(low, zero, q), kvc)]

    groups = [maps[g:g + maps_per_group] for g in range(0, len(maps), maps_per_group)]
    for stage in range(len(groups) + 1):
        if stage > 0:
            for mp in groups[stage - 1]:
                mp.pv_pass()
        if stage < len(groups):
            for mp in groups[stage]:
                mp.score_pass()

    lam = _lambda(lamv_ref)
    for n, (is_diff, _, _, oc) in enumerate(items):
        o = _combine_heads(is_diff, maps[2 * n].ot, maps[2 * n + 1].ot, lam, subln_ref[...])
        out_ref = od_ref if is_diff else og_ref
        out_ref[0, :, oc * LANES:(oc + 1) * LANES] = o.astype(BF16)


def _attn_loop_kernel(q_ref, kn_ref, vtn_ref, kc_ref, vtc_ref, lamv_ref, subln_ref, out_ref, *scratch,
                      is_diff, kv_cols):
    tq = ATTN_TILE_Q
    n_tiles = q_ref.shape[1] // tq
    n_groups = len(kv_cols)
    s_refs = scratch[:n_groups]
    m_refs = scratch[n_groups:]
    t_new = kn_ref.shape[1]
    segs = [(kn_ref, vtn_ref, 0), (kc_ref, vtc_ref, t_new)]
    chunks = []
    for si, (k_ref, _, base) in enumerate(segs):
        n = k_ref.shape[1]
        chunks += [(si, c0, min(n, c0 + ATTN_CHUNK_K), base + c0) for c0 in range(0, n, ATTN_CHUNK_K)]
    n_chunks = len(chunks)
    per_tile = n_groups * n_chunks
    lead = n_chunks + ATTN_SKEW
    assert 0 < ATTN_SKEW < n_chunks and lead < per_tile

    low = lax.broadcasted_iota(jnp.int32, (tq, LANES), 1) < DH
    lam = _lambda(lamv_ref)

    def rows(tile):
        return pl.ds(pl.multiple_of(tile * tq, tq), tq)

    def masked_q(tile, g):
        q = q_ref[0, rows(tile), g * LANES:(g + 1) * LANES]
        zero = jnp.zeros_like(q)
        return jnp.concatenate([jnp.where(low, q, zero), jnp.where(low, zero, q)], axis=0)

    def score_unit(qm, g, ci, mpart):
        si, c0, c1, r0 = chunks[ci]
        cs = slice(kv_cols[g] * LANES, (kv_cols[g] + 1) * LANES)
        s = _dot_nt(segs[si][0][0, c0:c1, cs], qm)
        s_refs[g][r0:r0 + c1 - c0, :] = s
        part = _fold_rows(s, jnp.max)
        mpart = part if mpart is None else jnp.maximum(mpart, part)
        if ci == n_chunks - 1:
            m_refs[g][...] = mpart.max(axis=0, keepdims=True)
            return None
        return mpart

    def pv_unit(g, ci, m, acc, lpart):
        si, c0, c1, r0 = chunks[ci]
        e = jnp.exp2(s_refs[g][r0:r0 + c1 - c0, :] - m)
        part = _fold_rows(e, jnp.sum)
        pv = _dot(segs[si][1][_value_rows(kv_cols[g]), c0:c1], e.astype(BF16))
        return (pv if acc is None else acc + pv), (part if lpart is None else lpart + part)

    def emit(g, tile, acc, lpart):
        ot = acc * (1.0 / lpart.sum(axis=0, keepdims=True))
        o = _combine_heads(is_diff, ot[:, :tq], ot[:, tq:], lam, subln_ref[...])
        out_ref[0, rows(tile), g * LANES:(g + 1) * LANES] = o.astype(BF16)

    mpart = None
    qm = None
    for u in range(lead):
        g, ci = divmod(u, n_chunks)
        if ci == 0:
            qm = masked_q(0, g)
        mpart = score_unit(qm, g, ci, mpart)

    def body(t, carried_mpart):
        nxt = jnp.minimum(t + 1, n_tiles - 1)
        a_state = {}
        a_state[(lead // per_tile, (lead // n_chunks) % n_groups)] = (None, carried_mpart)
        acc = lpart = m = None
        for j in range(per_tile):
            b_g, b_ci = divmod(j, n_chunks)
            if b_ci == 0:
                m = m_refs[b_g][...]
                acc = lpart = None
            acc, lpart = pv_unit(b_g, b_ci, m, acc, lpart)
            if b_ci == n_chunks - 1:
                emit(b_g, t, acc, lpart)

            a_off, a_rem = divmod(j + lead, per_tile)
            a_g, a_ci = divmod(a_rem, n_chunks)
            a_qm, a_mpart = a_state.get((a_off, a_g), (None, None))
            if a_qm is None:
                a_qm = masked_q(t if a_off == 0 else nxt, a_g)
            a_state[(a_off, a_g)] = (a_qm, score_unit(a_qm, a_g, a_ci, a_mpart))

        last = per_tile - 1 + lead
        return a_state[(last // per_tile, (last % per_tile) // n_chunks)][1]

    lax.fori_loop(0, n_tiles, body, mpart)


def _attn_sample_call(q, k, vt, kc, vtc, lamv, subln, *, is_diff, name):
    b, t, _ = q.shape
    tc = kc.shape[1]
    qw = ATTN_COLS_PER_STEP * LANES
    nblk = DQ_DIFF // qw
    if is_diff:
        kw = qw
        q0 = 0
        kv0 = 0
        kv_cols = tuple(range(ATTN_COLS_PER_STEP))
    else:
        assert ATTN_COLS_PER_STEP * (LANES // DH) == G_GQA
        kw = LANES
        q0 = DQ_DIFF // qw
        kv0 = DQ_DIFF // kw
        kv_cols = (0,) * ATTN_COLS_PER_STEP
    in_specs = [
        pl.BlockSpec((1, t, qw), lambda bi, j: (bi, 0, q0 + j)),
        pl.BlockSpec((1, t, kw), lambda bi, j: (bi, 0, kv0 + j)),
        pl.BlockSpec((kw, t), lambda bi, j: (kv0 + j, bi)),
        pl.BlockSpec((1, tc, kw), lambda bi, j: (bi, 0, kv0 + j)),
        pl.BlockSpec((kw, tc), lambda bi, j: (kv0 + j, bi)),
        pl.BlockSpec((4, DH), lambda bi, j: (0, 0)),
        pl.BlockSpec((1, LANES), lambda bi, j: (0, 0)),
    ]
    return pl.pallas_call(
        functools.partial(_attn_loop_kernel, is_diff=is_diff, kv_cols=kv_cols),
        grid=(b, nblk),
        in_specs=in_specs,
        out_specs=pl.BlockSpec((1, t, qw), lambda bi, j: (bi, 0, j)),
        out_shape=jax.ShapeDtypeStruct((b, t, nblk * qw), BF16),
        scratch_shapes=([pltpu.VMEM((t + tc, 2 * ATTN_TILE_Q), F32)] * ATTN_COLS_PER_STEP
                        + [pltpu.VMEM((1, 2 * ATTN_TILE_Q), F32)] * ATTN_COLS_PER_STEP),
        compiler_params=pltpu.CompilerParams(vmem_limit_bytes=VMEM_LIMIT_BYTES),
        name=name,
    )(q, k, vt, kc, vtc, lamv, subln)


def _attn_ctx_call(q, k, vt, lamv, subln, *, name):
    b, t, _ = q.shape
    ncol = DQ_DIFF // LANES
    items = tuple((True, j, j, j) for j in range(ncol)) + tuple(
        (False, ncol + j, ncol + j // (LANES // DH), j) for j in range(ncol))
    whole = lambda w: pl.BlockSpec((1, t, w), lambda bi: (bi, 0, 0))
    return pl.pallas_call(
        functools.partial(_attn_kernel, items=items, maps_per_group=ATTN_CTX_MAPS_PER_GROUP),
        grid=(b,),
        in_specs=[whole(D_MIX), whole(D_KV_ALL),
                  pl.BlockSpec((D_KV_ALL, t), lambda bi: (0, bi)),
                  pl.BlockSpec((4, DH), lambda bi: (0, 0)),
                  pl.BlockSpec((1, LANES), lambda bi: (0, 0))],
        out_specs=[whole(DQ_DIFF), whole(DQ_GQA)],
        out_shape=[jax.ShapeDtypeStruct((b, t, DQ_DIFF), BF16),
                   jax.ShapeDtypeStruct((b, t, DQ_GQA), BF16)],
        compiler_params=pltpu.CompilerParams(vmem_limit_bytes=VMEM_LIMIT_BYTES),
        name=name,
    )(q, k, vt, lamv, subln)


def _rope_tables(n_tokens):
    t = np.arange(n_tokens)
    row = (t // GRID_W).astype(np.float32)
    col = (t % GRID_W).astype(np.float32)
    half = DH // 2
    inv = np.float32(ROPE_THETA) ** (-(np.arange(0, half, 2, dtype=np.float32) / np.float32(half)))
    ang_r = row[:, None] * inv
    ang_c = col[:, None] * inv
    cos = np.concatenate([np.cos(ang_r)] * 2 + [np.cos(ang_c)] * 2, axis=1)
    sin = np.concatenate([-np.sin(ang_r), np.sin(ang_r), -np.sin(ang_c), np.sin(ang_c)], axis=1)
    reps = LANES // DH
    return (jnp.asarray(np.concatenate([cos] * reps, axis=1), F32),
            jnp.asarray(np.concatenate([sin] * reps, axis=1), F32))


def _dup_heads(a):
    parts = []
    for n in range(H_KV):
        head = a[..., n * DH:(n + 1) * DH]
        parts += [head, head]
    return jnp.concatenate(parts, axis=-1)


def kernel(x_prompt, x_sample, c, cache_diff_k, cache_diff_v, cache_gqa_k, cache_gqa_v, c_ctx, w_ada, b_ada, norm_ff1, w_ff1_gu, w_ff1_down, norm_mix, w_in, q_norm, k_norm, lambda_q1, lambda_k1, lambda_q2, lambda_k2, subln, w_out, norm_ff2, w_ff2_gu, w_ff2_down, final_norm):
    assert w_ada.shape[0] == 1, "single trunk layer"
    bc, tc_, _ = x_prompt.shape
    bs, ts, _ = x_sample.shape
    tpast = cache_diff_k.shape[2]
    assert bs + 1 <= MOD_ROWS
    ctx_row = bs

    cvecs = jnp.concatenate([c, c_ctx[None], jnp.zeros((MOD_ROWS - bs - 1, D_MODEL), F32)], axis=0)
    wgu1 = w_ff1_gu[0].astype(BF16)
    wd1 = w_ff1_down[0].astype(BF16)
    wi = w_in[0]
    n_qkvq = 3 * DQ_DIFF + DQ_GQA
    w_ext = jnp.concatenate(
        [wi[:, :n_qkvq], _dup_heads(wi[:, n_qkvq:n_qkvq + DKV_GQA]), _dup_heads(wi[:, n_qkvq + DKV_GQA:])],
        axis=1).astype(BF16)
    seg128 = (jnp.arange(LANES)[:, None] // DH == jnp.arange(LANES)[None, :] // DH).astype(BF16)
    seg = jnp.concatenate([seg128, seg128], axis=0)
    qkn = jnp.concatenate([jnp.tile(q_norm[0], H_GQA), jnp.tile(k_norm[0], 2 * H_KV)])[None]
    lamv = jnp.stack([lambda_q1[0], lambda_k1[0], lambda_q2[0], lambda_k2[0]])
    g1 = norm_ff1
    gm = norm_mix
    g2 = norm_ff2
    fg = final_norm[None]
    cos, sin = _rope_tables(ts)
    kcache = jnp.concatenate([cache_diff_k[:, 0].reshape(bs, tpast, DQ_DIFF),
                              _dup_heads(cache_gqa_k[:, 0].reshape(bs, tpast, DKV_GQA))], axis=-1).astype(BF16)
    vcache = jnp.concatenate([cache_diff_v[:, 0].reshape(bs, tpast, DQ_DIFF),
                              _dup_heads(cache_gqa_v[:, 0].reshape(bs, tpast, DKV_GQA))], axis=-1).astype(BF16)
    vcache_t = vcache.transpose(2, 0, 1).reshape(D_KV_ALL, bs * tpast)

    mod = _mod_call(cvecs, w_ada[0], b_ada).reshape(MOD_ROWS, N_MOD, D_MODEL)

    ctx_map = lambda i: (ctx_row, 0, 0)
    xc = x_prompt.reshape(bc * tc_, D_MODEL)
    xc, wgu2, wd2, wout = _ffn_call(xc, mod, ctx_map, g1, wgu1, wd1, sub=0,
                                    side_casts=(w_ff2_gu[0], w_ff2_down[0], w_out[0]), name="ffn1_ctx")
    qc, kc, vtc, dkt32, dv4, gkt32, gvt32 = _proj_call(
        xc, mod, ctx_map, gm, w_ext, seg, qkn, rope_tables=None, cache_seq=tc_, name="proj_ctx")
    odc, ogc = _attn_ctx_call(qc.reshape(bc, tc_, D_MIX), kc.reshape(bc, tc_, D_KV_ALL), vtc,
                              lamv, subln, name="attn_ctx")
    yc = _ffn_call(xc, mod, ctx_map, g2, wgu2, wd2, sub=2,
                   pre=(odc.reshape(bc * tc_, DQ_DIFF), ogc.reshape(bc * tc_, DQ_GQA), wout),
                   final_gain=fg, name="ffn2_ctx")

    ffn_tiles = ts // FFN_TILE
    proj_tiles = ts // PROJ_TILE
    xs = x_sample.reshape(bs * ts, D_MODEL)
    xs = _ffn_call(xs, mod, lambda i: (i // ffn_tiles, 0, 0), g1, wgu1, wd1, sub=0, name="ffn1_smp")
    qs, ks, vts = _proj_call(xs, mod, lambda i: (i // proj_tiles, 0, 0), gm, w_ext, seg, qkn,
                             rope_tables=(cos, sin), cache_seq=None, name="proj_smp")
    qs = qs.reshape(bs, ts, D_MIX)
    ks = ks.reshape(bs, ts, D_KV_ALL)
    ods = _attn_sample_call(qs, ks, vts, kcache, vcache_t, lamv, subln, is_diff=True, name="attn_diff_smp")
    ogs = _attn_sample_call(qs, ks, vts, kcache, vcache_t, lamv, subln, is_diff=False, name="attn_gqa_smp")
    ys = _ffn_call(xs, mod, lambda i: (i // ffn_tiles, 0, 0), g2, wgu2, wd2, sub=2,
                   pre=(ods.reshape(bs * ts, DQ_DIFF), ogs.reshape(bs * ts, DQ_GQA), wout),
                   final_gain=fg, name="ffn2_smp")

    return (yc.reshape(bc, tc_, D_MODEL),
            ys.reshape(bs, ts, D_MODEL),
            dkt32.reshape(bc, 1, H_DIFF, 2, DH, tc_).transpose(0, 1, 5, 2, 3, 4),
            dv4.reshape(bc, 1, tc_, H_DIFF, 2 * DH),
            gkt32.reshape(bc, 1, H_KV, DH, tc_).transpose(0, 1, 4, 2, 3),
            gvt32.reshape(bc, 1, H_KV, DH, tc_).transpose(0, 1, 4, 2, 3))
```

```python
import functools
import math

import jax
import jax.numpy as jnp
import numpy as np
from jax import lax
from jax.experimental import pallas as pl
from jax.experimental.pallas import tpu as pltpu

F32 = jnp.float32
BF16 = jnp.bfloat16

D_MODEL = 1024
N_MOD = 9
D_FF = 2816
H_DIFF = 4
DH = 64
H_GQA = 8
H_KV = 2
G_GQA = H_GQA // H_KV
GRID_W = 64
ROPE_THETA = 10000.0
EPS = 1e-6
LAMBDA_INIT = 0.8 - 0.6 * math.exp(-0.3 * 0)
DQ_DIFF = H_DIFF * 2 * DH
DQ_GQA = H_GQA * DH
DKV_GQA = H_KV * DH
ROPE_HALF = DH // 4
D_MIX = DQ_DIFF + DQ_GQA
D_KV_ALL = DQ_DIFF + 2 * DKV_GQA
D_NORMED = DQ_GQA + 2 * DKV_GQA
SCORE_SCALE = DH ** -0.5 * math.log2(math.e)

LANES = 128
SUBLANES = 8
BF16_ROWS = 16
VMEM_LIMIT_BYTES = 60000 * 1024

MOD_ROWS = 8
MOD_TILE_N = 4608
FFN_TILE = 512
PROJ_TILE = 1024
ATTN_TILE_Q = 256
ATTN_CHUNK_K = 512
ATTN_SKEW = 2
ATTN_COLS_PER_STEP = 2
ATTN_CTX_MAPS_PER_GROUP = 8


def _sigmoid(x):
    return 1.0 / (1.0 + jnp.exp(-x))


def _rms(x):
    return x * lax.rsqrt(jnp.mean(x * x, axis=-1, keepdims=True) + EPS)


def _dot(a, b):
    return jnp.dot(a, b, preferred_element_type=F32)


def _dot_nt(a, b):
    return lax.dot_general(a, b, (((1,), (1,)), ((), ())), preferred_element_type=F32)


def _resident(shape):
    return pl.BlockSpec(shape, lambda *_: (0,) * len(shape), pipeline_mode=pl.Buffered(1))


def _mod_kernel(c_ref, w_ref, b_ref, o_ref):
    c = c_ref[...]
    s = c * _sigmoid(c)
    o_ref[...] = _dot(s.astype(BF16), w_ref[...].astype(BF16)) + b_ref[...]


def _mod_call(cvecs, w_ada, b_ada):
    n = w_ada.shape[1]
    return pl.pallas_call(
        _mod_kernel,
        grid=(n // MOD_TILE_N,),
        in_specs=[
            pl.BlockSpec((MOD_ROWS, D_MODEL), lambda j: (0, 0)),
            pl.BlockSpec((D_MODEL, MOD_TILE_N), lambda j: (0, j)),
            pl.BlockSpec((1, MOD_TILE_N), lambda j: (0, j)),
        ],
        out_specs=pl.BlockSpec((MOD_ROWS, MOD_TILE_N), lambda j: (0, j)),
        out_shape=jax.ShapeDtypeStruct((MOD_ROWS, n), F32),
        name="mod",
    )(cvecs, w_ada, b_ada)


def _ffn_kernel(*refs, sub, pre, final, n_side):
    refs = list(refs)
    x_ref, mod_ref = refs[:2]
    pos = 2
    if pre:
        od_ref, og_ref, wout_ref = refs[pos:pos + 3]
        pos += 3
    g_ref, wgu_ref, wd_ref = refs[pos:pos + 3]
    pos += 3
    if final:
        fg_ref = refs[pos]
        pos += 1
    side_in = refs[pos:pos + n_side]
    o_ref = refs[pos + n_side]
    side_out = refs[pos + n_side + 1:]

    x = x_ref[...]
    mod = mod_ref[0]
    if pre:
        o = jnp.concatenate([od_ref[...], og_ref[...]], axis=1)
        x = x + mod[5:6] * _dot(o, wout_ref[...])
    shift = mod[3 * sub:3 * sub + 1]
    scale = mod[3 * sub + 1:3 * sub + 2]
    gate = mod[3 * sub + 2:3 * sub + 3]
    h = (_rms(x) * g_ref[...]) * (1.0 + scale) + shift
    gu = _dot(h.astype(BF16), wgu_ref[...])
    g = gu[:, :D_FF]
    u = gu[:, D_FF:]
    act = (g * _sigmoid(g)) * u
    y = _dot(act.astype(BF16), wd_ref[...])
    x = x + (0.5 * gate) * y
    if final:
        x = _rms(x) * fg_ref[...]
    o_ref[...] = x
    for src, dst in zip(side_in, side_out):
        dst[...] = src[...].astype(BF16)


def _ffn_call(x, mod, row_map, gain, wgu, wd, *, sub, pre=None, final_gain=None, side_casts=(), name):
    t = x.shape[0]
    tm = FFN_TILE
    steps = t // tm
    row_spec = lambda w: pl.BlockSpec((tm, w), lambda i: (i, 0))
    in_specs = [row_spec(D_MODEL), pl.BlockSpec((1, N_MOD, D_MODEL), row_map)]
    args = [x, mod]
    if pre is not None:
        od, og, wout = pre
        in_specs += [row_spec(DQ_DIFF), row_spec(DQ_GQA), _resident((D_MIX, D_MODEL))]
        args += [od, og, wout]
    in_specs += [_resident((1, D_MODEL)), _resident((D_MODEL, 2 * D_FF)), _resident((D_FF, D_MODEL))]
    args += [gain, wgu, wd]
    if final_gain is not None:
        in_specs.append(_resident((1, D_MODEL)))
        args.append(final_gain)
    out_specs = [row_spec(D_MODEL)]
    out_shape = [jax.ShapeDtypeStruct((t, D_MODEL), F32)]
    for w in side_casts:
        rows, cols = w.shape
        assert rows % (steps * BF16_ROWS) == 0
        spec = pl.BlockSpec((rows // steps, cols), lambda i: (i, 0))
        in_specs.append(spec)
        args.append(w)
        out_specs.append(spec)
        out_shape.append(jax.ShapeDtypeStruct((rows, cols), BF16))
    outs = pl.pallas_call(
        functools.partial(_ffn_kernel, sub=sub, pre=pre is not None, final=final_gain is not None,
                          n_side=len(side_casts)),
        grid=(steps,),
        in_specs=in_specs,
        out_specs=out_specs,
        out_shape=out_shape,
        compiler_params=pltpu.CompilerParams(vmem_limit_bytes=VMEM_LIMIT_BYTES),
        name=name,
    )(*args)
    return outs if side_casts else outs[0]


def _rope(x, cos, sin, first_of_pair):
    w = x.shape[1]
    partner = jnp.where(first_of_pair, pltpu.roll(x, w - ROPE_HALF, 1), pltpu.roll(x, ROPE_HALF, 1))
    reps = w // LANES
    cos_w = jnp.concatenate([cos] * reps, axis=1) if reps > 1 else cos
    sin_w = jnp.concatenate([sin] * reps, axis=1) if reps > 1 else sin
    return x * cos_w + partner * sin_w


def _proj_kernel(*refs, rope, emit_f32, sum_row):
    refs = list(refs)
    x_ref, mod_ref, g_ref, w_ref, seg_ref, qkn_ref = refs[:6]
    pos = 6
    if rope:
        cos_ref, sin_ref = refs[pos:pos + 2]
        pos += 2
    q_ref, k_ref, vt_ref = refs[pos:pos + 3]
    pos += 3

    x = x_ref[...]
    mod = mod_ref[0]
    h = (_rms(x) * g_ref[...]) * (1.0 + mod[4:5]) + mod[3:4]
    qkv = _dot(h.astype(BF16), w_ref[...])
    dq = qkv[:, 0:DQ_DIFF]
    dk = qkv[:, DQ_DIFF:2 * DQ_DIFF]
    dv = qkv[:, 2 * DQ_DIFF:3 * DQ_DIFF]
    n0 = 3 * DQ_DIFF
    raw = qkv[:, n0:n0 + D_NORMED]
    gv = qkv[:, n0 + D_NORMED:]

    sq = raw * raw
    hi = sq.astype(BF16)
    lo = (sq - hi.astype(F32)).astype(BF16)
    seg = seg_ref[...]
    ss = jnp.concatenate(
        [_dot(jnp.concatenate([hi[:, c:c + LANES], lo[:, c:c + LANES]], axis=1), seg)
         for c in range(0, D_NORMED, LANES)], axis=1)
    normed = (raw * lax.rsqrt(ss * (1.0 / DH) + EPS)) * qkn_ref[...]
    gq = normed[:, :DQ_GQA]
    gk = normed[:, DQ_GQA:]

    if rope:
        cos = cos_ref[...]
        sin = sin_ref[...]
        lane = lax.broadcasted_iota(jnp.int32, (x.shape[0], LANES), 1)
        first = (lane % (2 * ROPE_HALF)) < ROPE_HALF
        first4 = jnp.concatenate([first] * 4, axis=1)
        first2 = jnp.concatenate([first] * 2, axis=1)
        dq = _rope(dq, cos, sin, first4)
        dk = _rope(dk, cos, sin, first4)
        gq = _rope(gq, cos, sin, first4)
        gk = _rope(gk, cos, sin, first2)

    if sum_row:
        lane = lax.broadcasted_iota(jnp.int32, (x.shape[0], LANES), 1)
        marker = jnp.where(lane == DH, 1.0, 0.0)
        gv = jnp.concatenate([jnp.where(lane < DH, gv[:, c:c + LANES], marker)
                              for c in range(0, gv.shape[1], LANES)], axis=1)

    q_ref[...] = jnp.concatenate([dq * SCORE_SCALE, gq * SCORE_SCALE], axis=1).astype(BF16)
    k_ref[...] = jnp.concatenate([dk, gk], axis=1).astype(BF16)
    vt_ref[...] = jnp.concatenate([dv, gv], axis=1).T.astype(BF16)

    if emit_f32:
        dkt_ref, dv4_ref, gkt_ref, gvt_ref = refs[pos:pos + 4]
        tm = x.shape[0]
        seq = dkt_ref.shape[2]
        lane = lax.broadcasted_iota(jnp.int32, (tm, LANES), 1)
        low = lane < DH
        dkt = dk.T
        gkt = jnp.where(low, gk[:, :LANES], gk[:, LANES:]).T
        gvt = jnp.where(low, gv[:, :LANES], gv[:, LANES:]).T
        for b in range(tm // seq):
            dkt_ref[b] = dkt[:, b * seq:(b + 1) * seq]
            gkt_ref[b] = gkt[:, b * seq:(b + 1) * seq]
            gvt_ref[b] = gvt[:, b * seq:(b + 1) * seq]
        for hd in range(H_DIFF):
            dv4_ref[pl.ds(hd, tm, stride=H_DIFF), :] = dv[:, hd * LANES:(hd + 1) * LANES]


def _proj_call(x, mod, row_map, gain, w_ext, seg, qkn, *, rope_tables, cache_seq, sum_row, name):
    t = x.shape[0]
    emit_f32 = cache_seq is not None
    tm = PROJ_TILE
    row_spec = lambda w: pl.BlockSpec((tm, w), lambda i: (i, 0))
    n_ext = w_ext.shape[1]
    in_specs = [row_spec(D_MODEL), pl.BlockSpec((1, N_MOD, D_MODEL), row_map),
                _resident((1, D_MODEL)), _resident((D_MODEL, n_ext)),
                _resident((2 * LANES, LANES)), _resident((1, D_NORMED))]
    args = [x, mod, gain, w_ext, seg, qkn]
    if rope_tables is not None:
        cos, sin = rope_tables
        tiles_per_seq = cos.shape[0] // tm
        tab_spec = pl.BlockSpec((tm, LANES), lambda i: (i % tiles_per_seq, 0))
        in_specs += [tab_spec, tab_spec]
        args += [cos, sin]
    out_specs = [row_spec(D_MIX), row_spec(D_KV_ALL), pl.BlockSpec((D_KV_ALL, tm), lambda i: (0, i))]
    out_shape = [jax.ShapeDtypeStruct((t, D_MIX), BF16),
                 jax.ShapeDtypeStruct((t, D_KV_ALL), BF16),
                 jax.ShapeDtypeStruct((D_KV_ALL, t), BF16)]
    if emit_f32:
        assert tm % cache_seq == 0
        nb = tm // cache_seq
        slab = lambda rows: pl.BlockSpec((nb, rows, cache_seq), lambda i: (i, 0, 0))
        out_specs += [slab(DQ_DIFF), pl.BlockSpec((tm * H_DIFF, LANES), lambda i: (i, 0)),
                      slab(DKV_GQA), slab(DKV_GQA)]
        out_shape += [jax.ShapeDtypeStruct((t // cache_seq, DQ_DIFF, cache_seq), F32),
                      jax.ShapeDtypeStruct((t * H_DIFF, LANES), F32),
                      jax.ShapeDtypeStruct((t // cache_seq, DKV_GQA, cache_seq), F32),
                      jax.ShapeDtypeStruct((t // cache_seq, DKV_GQA, cache_seq), F32)]
    return pl.pallas_call(
        functools.partial(_proj_kernel, rope=rope_tables is not None, emit_f32=emit_f32, sum_row=sum_row),
        grid=(t // tm,),
        in_specs=in_specs,
        out_specs=out_specs,
        out_shape=out_shape,
        compiler_params=pltpu.CompilerParams(vmem_limit_bytes=VMEM_LIMIT_BYTES),
        name=name,
    )(*args)


def _lambda(lamv_ref):
    lamv = lamv_ref[...]
    return (jnp.exp(jnp.sum(lamv[0:1] * lamv[1:2], axis=-1, keepdims=True))
            - jnp.exp(jnp.sum(lamv[2:3] * lamv[3:4], axis=-1, keepdims=True)) + LAMBDA_INIT)


def _fold_rows(x, op):
    return op(x.reshape(x.shape[0] // SUBLANES, SUBLANES, x.shape[1]), axis=0)


def _value_rows(kvc):
    return slice(kvc * LANES, (kvc + 1) * LANES)


def _combine_heads(is_diff, ot_a, ot_b, lam, subln):
    if is_diff:
        o = (ot_a - lam * ot_b).T
        return (_rms(o) * subln) * (1.0 - LAMBDA_INIT)
    return jnp.concatenate([ot_a[:DH], ot_b[:DH]], axis=0).T


def _attn_kernel(q_ref, k_ref, vt_ref, lamv_ref, subln_ref, od_ref, og_ref, *, items, maps_per_group):
    tq = q_ref.shape[1]
    low = lax.broadcasted_iota(jnp.int32, (tq, LANES), 1) < DH

    class SoftmaxMap:
        def __init__(self, qm, kvc):
            self.qm = qm
            self.cs = slice(kvc * LANES, (kvc + 1) * LANES)
            self.vrows = _value_rows(kvc)

        def score_pass(self):
            self.s = _dot_nt(k_ref[0, :, self.cs], self.qm)
            self.m = _fold_rows(self.s, jnp.max).max(axis=0, keepdims=True)

        def pv_pass(self):
            e = jnp.exp2(self.s - self.m)
            tot = _fold_rows(e, jnp.sum).sum(axis=0, keepdims=True)
            self.ot = _dot(vt_ref[self.vrows, :], e.astype(BF16)) * (1.0 / tot)

    maps = []
    for is_diff, qc, kvc, _ in items:
        q = q_ref[0, :, qc * LANES:(qc + 1) * LANES]
        zero = jnp.zeros_like(q)
        maps += [SoftmaxMap(jnp.where(low, q, zero), kvc), SoftmaxMap(jnp.where(low, zero, q), kvc)]

    groups = [maps[g:g + maps_per_group] for g in range(0, len(maps), maps_per_group)]
    for stage in range(len(groups) + 1):
        if stage > 0:
            for mp in groups[stage - 1]:
                mp.pv_pass()
        if stage < len(groups):
            for mp in groups[stage]:
                mp.score_pass()

    lam = _lambda(lamv_ref)
    for n, (is_diff, _, _, oc) in enumerate(items):
        o = _combine_heads(is_diff, maps[2 * n].ot, maps[2 * n + 1].ot, lam, subln_ref[...])
        out_ref = od_ref if is_diff else og_ref
        out_ref[0, :, oc * LANES:(oc + 1) * LANES] = o.astype(BF16)


def _attn_loop_kernel(q_ref, kn_ref, vtn_ref, kc_ref, vtc_ref, lamv_ref, subln_ref, out_ref, *scratch,
                      is_diff, kv_cols):
    tq = ATTN_TILE_Q
    n_tiles = q_ref.shape[1] // tq
    n_groups = len(kv_cols)
    s_refs = scratch[:n_groups]
    m_refs = scratch[n_groups:]
    t_new = kn_ref.shape[1]
    segs = [(kn_ref, vtn_ref, 0), (kc_ref, vtc_ref, t_new)]
    chunks = []
    for si, (k_ref, _, base) in enumerate(segs):
        n = k_ref.shape[1]
        chunks += [(si, c0, min(n, c0 + ATTN_CHUNK_K), base + c0) for c0 in range(0, n, ATTN_CHUNK_K)]
    n_chunks = len(chunks)
    per_tile = n_groups * n_chunks
    lead = n_chunks + ATTN_SKEW
    assert 0 < ATTN_SKEW < n_chunks and lead < per_tile

    low = lax.broadcasted_iota(jnp.int32, (tq, LANES), 1) < DH
    lam = _lambda(lamv_ref)

    def rows(tile):
        return pl.ds(pl.multiple_of(tile * tq, tq), tq)

    def masked_q(tile, g):
        q = q_ref[0, rows(tile), g * LANES:(g + 1) * LANES]
        zero = jnp.zeros_like(q)
        return jnp.concatenate([jnp.where(low, q, zero), jnp.where(low, zero, q)], axis=0)

    def score_unit(qm, g, ci, mpart):
        si, c0, c1, r0 = chunks[ci]
        cs = slice(kv_cols[g] * LANES, (kv_cols[g] + 1) * LANES)
        s = _dot_nt(segs[si][0][0, c0:c1, cs], qm)
        s_refs[g][r0:r0 + c1 - c0, :] = s
        part = _fold_rows(s, jnp.max)
        mpart = part if mpart is None else jnp.maximum(mpart, part)
        if ci == n_chunks - 1:
            m_refs[g][...] = mpart.max(axis=0, keepdims=True)
            return None
        return mpart

    def pv_unit(g, ci, m, acc, lpart):
        si, c0, c1, r0 = chunks[ci]
        e = jnp.exp2(s_refs[g][r0:r0 + c1 - c0, :] - m)
        pv = _dot(segs[si][1][_value_rows(kv_cols[g]), c0:c1], e.astype(BF16))
        acc = pv if acc is None else acc + pv
        if not is_diff:
            return acc, None
        part = _fold_rows(e, jnp.sum)
        return acc, (part if lpart is None else lpart + part)

    def emit(g, tile, acc, lpart):
        lsum = lpart.sum(axis=0, keepdims=True) if is_diff else acc[DH:DH + 1]
        ot = acc * (1.0 / lsum)
        o = _combine_heads(is_diff, ot[:, :tq], ot[:, tq:], lam, subln_ref[...])
        out_ref[0, rows(tile), g * LANES:(g + 1) * LANES] = o.astype(BF16)

    mpart = None
    qm = None
    for u in range(lead):
        g, ci = divmod(u, n_chunks)
        if ci == 0:
            qm = masked_q(0, g)
        mpart = score_unit(qm, g, ci, mpart)

    def body(t, carried_mpart):
        nxt = jnp.minimum(t + 1, n_tiles - 1)
        a_state = {}
        a_state[(lead // per_tile, (lead // n_chunks) % n_groups)] = (None, carried_mpart)
        acc = lpart = m = None
        for j in range(per_tile):
            b_g, b_ci = divmod(j, n_chunks)
            if b_ci == 0:
                m = m_refs[b_g][...]
                acc = lpart = None
            acc, lpart = pv_unit(b_g, b_ci, m, acc, lpart)
            if b_ci == n_chunks - 1:
                emit(b_g, t, acc, lpart)

            a_off, a_rem = divmod(j + lead, per_tile)
            a_g, a_ci = divmod(a_rem, n_chunks)
            a_qm, a_mpart = a_state.get((a_off, a_g), (None, None))
            if a_qm is None:
                a_qm = masked_q(t if a_off == 0 else nxt, a_g)
            a_state[(a_off, a_g)] = (a_qm, score_unit(a_qm, a_g, a_ci, a_mpart))

        last = per_tile - 1 + lead
        return a_state[(last // per_tile, (last % per_tile) // n_chunks)][1]

    lax.fori_loop(0, n_tiles, body, mpart)


def _attn_sample_call(q, k, vt, kc, vtc, lamv, subln, *, is_diff, name):
    b, t, _ = q.shape
    tc = kc.shape[1]
    qw = ATTN_COLS_PER_STEP * LANES
    nblk = DQ_DIFF // qw
    if is_diff:
        kw = qw
        q0 = 0
        kv0 = 0
        kv_cols = tuple(range(ATTN_COLS_PER_STEP))
    else:
        assert ATTN_COLS_PER_STEP * (LANES // DH) == G_GQA
        kw = LANES
        q0 = DQ_DIFF // qw
        kv0 = DQ_DIFF // kw
        kv_cols = (0,) * ATTN_COLS_PER_STEP
    in_specs = [
        pl.BlockSpec((1, t, qw), lambda bi, j: (bi, 0, q0 + j)),
        pl.BlockSpec((1, t, kw), lambda bi, j: (bi, 0, kv0 + j)),
        pl.BlockSpec((kw, t), lambda bi, j: (kv0 + j, bi)),
        pl.BlockSpec((1, tc, kw), lambda bi, j: (bi, 0, kv0 + j)),
        pl.BlockSpec((kw, tc), lambda bi, j: (kv0 + j, bi)),
        pl.BlockSpec((4, DH), lambda bi, j: (0, 0)),
        pl.BlockSpec((1, LANES), lambda bi, j: (0, 0)),
    ]
    return pl.pallas_call(
        functools.partial(_attn_loop_kernel, is_diff=is_diff, kv_cols=kv_cols),
        grid=(b, nblk),
        in_specs=in_specs,
        out_specs=pl.BlockSpec((1, t, qw), lambda bi, j: (bi, 0, j)),
        out_shape=jax.ShapeDtypeStruct((b, t, nblk * qw), BF16),
        scratch_shapes=([pltpu.VMEM((t + tc, 2 * ATTN_TILE_Q), F32)] * ATTN_COLS_PER_STEP
                        + [pltpu.VMEM((1, 2 * ATTN_TILE_Q), F32)] * ATTN_COLS_PER_STEP),
        compiler_params=pltpu.CompilerParams(vmem_limit_bytes=VMEM_LIMIT_BYTES),
        name=name,
    )(q, k, vt, kc, vtc, lamv, subln)


def _attn_ctx_call(q, k, vt, lamv, subln, *, name):
    b, t, _ = q.shape
    ncol = DQ_DIFF // LANES
    items = tuple((True, j, j, j) for j in range(ncol)) + tuple(
        (False, ncol + j, ncol + j // (LANES // DH), j) for j in range(ncol))
    whole = lambda w: pl.BlockSpec((1, t, w), lambda bi: (bi, 0, 0))
    return pl.pallas_call(
        functools.partial(_attn_kernel, items=items, maps_per_group=ATTN_CTX_MAPS_PER_GROUP),
        grid=(b,),
        in_specs=[whole(D_MIX), whole(D_KV_ALL),
                  pl.BlockSpec((D_KV_ALL, t), lambda bi: (0, bi)),
                  pl.BlockSpec((4, DH), lambda bi: (0, 0)),
                  pl.BlockSpec((1, LANES), lambda bi: (0, 0))],
        out_specs=[whole(DQ_DIFF), whole(DQ_GQA)],
        out_shape=[jax.ShapeDtypeStruct((b, t, DQ_DIFF), BF16),
                   jax.ShapeDtypeStruct((b, t, DQ_GQA), BF16)],
        compiler_params=pltpu.CompilerParams(vmem_limit_bytes=VMEM_LIMIT_BYTES),
        name=name,
    )(q, k, vt, lamv, subln)


def _rope_tables(n_tokens):
    t = np.arange(n_tokens)
    row = (t // GRID_W).astype(np.float32)
    col = (t % GRID_W).astype(np.float32)
    half = DH // 2
    inv = np.float32(ROPE_THETA) ** (-(np.arange(0, half, 2, dtype=np.float32) / np.float32(half)))
    ang_r = row[:, None] * inv
    ang_c = col[:, None] * inv
    cos = np.concatenate([np.cos(ang_r)] * 2 + [np.cos(ang_c)] * 2, axis=1)
    sin = np.concatenate([-np.sin(ang_r), np.sin(ang_r), -np.sin(ang_c), np.sin(ang_c)], axis=1)
    reps = LANES // DH
    return (jnp.asarray(np.concatenate([cos] * reps, axis=1), F32),
            jnp.asarray(np.concatenate([sin] * reps, axis=1), F32))


def _dup_heads(a):
    parts = []
    for n in range(H_KV):
        head = a[..., n * DH:(n + 1) * DH]
        parts += [head, head]
    return jnp.concatenate(parts, axis=-1)


def _with_sum_feature(a):
    ones = jnp.ones(a.shape[:-1] + (1,), a.dtype)
    zeros = jnp.zeros(a.shape[:-1] + (DH - 1,), a.dtype)
    parts = []
    for n in range(H_KV):
        parts += [a[..., n * DH:(n + 1) * DH], ones, zeros]
    return jnp.concatenate(parts, axis=-1)


def kernel(x_prompt, x_sample, c, cache_diff_k, cache_diff_v, cache_gqa_k, cache_gqa_v, c_ctx, w_ada, b_ada, norm_ff1, w_ff1_gu, w_ff1_down, norm_mix, w_in, q_norm, k_norm, lambda_q1, lambda_k1, lambda_q2, lambda_k2, subln, w_out, norm_ff2, w_ff2_gu, w_ff2_down, final_norm):
    assert w_ada.shape[0] == 1, "single trunk layer"
    bc, tc_, _ = x_prompt.shape
    bs, ts, _ = x_sample.shape
    tpast = cache_diff_k.shape[2]
    assert bs + 1 <= MOD_ROWS
    ctx_row = bs

    cvecs = jnp.concatenate([c, c_ctx[None], jnp.zeros((MOD_ROWS - bs - 1, D_MODEL), F32)], axis=0)
    wgu1 = w_ff1_gu[0].astype(BF16)
    wd1 = w_ff1_down[0].astype(BF16)
    wi = w_in[0]
    n_qkvq = 3 * DQ_DIFF + DQ_GQA
    w_ext = jnp.concatenate(
        [wi[:, :n_qkvq], _dup_heads(wi[:, n_qkvq:n_qkvq + DKV_GQA]), _dup_heads(wi[:, n_qkvq + DKV_GQA:])],
        axis=1).astype(BF16)
    seg128 = (jnp.arange(LANES)[:, None] // DH == jnp.arange(LANES)[None, :] // DH).astype(BF16)
    seg = jnp.concatenate([seg128, seg128], axis=0)
    qkn = jnp.concatenate([jnp.tile(q_norm[0], H_GQA), jnp.tile(k_norm[0], 2 * H_KV)])[None]
    lamv = jnp.stack([lambda_q1[0], lambda_k1[0], lambda_q2[0], lambda_k2[0]])
    g1 = norm_ff1
    gm = norm_mix
    g2 = norm_ff2
    fg = final_norm[None]
    cos, sin = _rope_tables(ts)
    kcache = jnp.concatenate([cache_diff_k[:, 0].reshape(bs, tpast, DQ_DIFF),
                              _dup_heads(cache_gqa_k[:, 0].reshape(bs, tpast, DKV_GQA))], axis=-1).astype(BF16)
    vcache = jnp.concatenate([cache_diff_v[:, 0].reshape(bs, tpast, DQ_DIFF),
                              _with_sum_feature(cache_gqa_v[:, 0].reshape(bs, tpast, DKV_GQA))], axis=-1).astype(BF16)
    vcache_t = vcache.transpose(2, 0, 1).reshape(D_KV_ALL, bs * tpast)

    mod = _mod_call(cvecs, w_ada[0], b_ada).reshape(MOD_ROWS, N_MOD, D_MODEL)

    ctx_map = lambda i: (ctx_row, 0, 0)
    xc = x_prompt.reshape(bc * tc_, D_MODEL)
    xc, wgu2, wd2, wout = _ffn_call(xc, mod, ctx_map, g1, wgu1, wd1, sub=0,
                                    side_casts=(w_ff2_gu[0], w_ff2_down[0], w_out[0]), name="ffn1_ctx")
    qc, kc, vtc, dkt32, dv4, gkt32, gvt32 = _proj_call(
        xc, mod, ctx_map, gm, w_ext, seg, qkn, rope_tables=None, cache_seq=tc_, sum_row=False, name="proj_ctx")
    odc, ogc = _attn_ctx_call(qc.reshape(bc, tc_, D_MIX), kc.reshape(bc, tc_, D_KV_ALL), vtc,
                              lamv, subln, name="attn_ctx")
    yc = _ffn_call(xc, mod, ctx_map, g2, wgu2, wd2, sub=2,
                   pre=(odc.reshape(bc * tc_, DQ_DIFF), ogc.reshape(bc * tc_, DQ_GQA), wout),
                   final_gain=fg, name="ffn2_ctx")

    ffn_tiles = ts // FFN_TILE
    proj_tiles = ts // PROJ_TILE
    xs = x_sample.reshape(bs * ts, D_MODEL)
    xs = _ffn_call(xs, mod, lambda i: (i // ffn_tiles, 0, 0), g1, wgu1, wd1, sub=0, name="ffn1_smp")
    qs, ks, vts = _proj_call(xs, mod, lambda i: (i // proj_tiles, 0, 0), gm, w_ext, seg, qkn,
                             rope_tables=(cos, sin), cache_seq=None, sum_row=True, name="proj_smp")
    qs = qs.reshape(bs, ts, D_MIX)
    ks = ks.reshape(bs, ts, D_KV_ALL)
    ods = _attn_sample_call(qs, ks, vts, kcache, vcache_t, lamv, subln, is_diff=True, name="attn_diff_smp")
    ogs = _attn_sample_call(qs, ks, vts, kcache, vcache_t, lamv, subln, is_diff=False, name="attn_gqa_smp")
    ys = _ffn_call(xs, mod, lambda i: (i // ffn_tiles, 0, 0), g2, wgu2, wd2, sub=2,
                   pre=(ods.reshape(bs * ts, DQ_DIFF), ogs.reshape(bs * ts, DQ_GQA), wout),
                   final_gain=fg, name="ffn2_smp")

    return (yc.reshape(bc, tc_, D_MODEL),
            ys.reshape(bs, ts, D_MODEL),
            dkt32.reshape(bc, 1, H_DIFF, 2, DH, tc_).transpose(0, 1, 5, 2, 3, 4),
            dv4.reshape(bc, 1, tc_, H_DIFF, 2 * DH),
            gkt32.reshape(bc, 1, H_KV, DH, tc_).transpose(0, 1, 4, 2, 3),
            gvt32.reshape(bc, 1, H_KV, DH, tc_).transpose(0, 1, 4, 2, 3))
```

```python
import functools
import math

import jax
import jax.numpy as jnp
import numpy as np
from jax import lax
from jax.experimental import pallas as pl
from jax.experimental.pallas import tpu as pltpu

F32 = jnp.float32
BF16 = jnp.bfloat16

D_MODEL = 1024
N_MOD = 9
D_FF = 2816
H_DIFF = 4
DH = 64
H_GQA = 8
H_KV = 2
G_GQA = H_GQA // H_KV
GRID_W = 64
ROPE_THETA = 10000.0
EPS = 1e-6
LAMBDA_INIT = 0.8 - 0.6 * math.exp(-0.3 * 0)
DQ_DIFF = H_DIFF * 2 * DH
DQ_GQA = H_GQA * DH
DKV_GQA = H_KV * DH
ROPE_HALF = DH // 4
D_MIX = DQ_DIFF + DQ_GQA
D_KV_ALL = DQ_DIFF + 2 * DKV_GQA
D_NORMED = DQ_GQA + 2 * DKV_GQA
SCORE_SCALE = DH ** -0.5 * math.log2(math.e)

LANES = 128
SUBLANES = 8
BF16_ROWS = 16
VMEM_LIMIT_BYTES = 60000 * 1024

MOD_ROWS = 8
MOD_TILE_N = 4608
FFN_TILE = 512
PROJ_TILE = 1024
ATTN_TILE_Q = 256
ATTN_CHUNK_K = 512
ATTN_SKEW = 2
ATTN_COLS_PER_STEP = 2
ATTN_CTX_MAPS_PER_GROUP = 8


def _sigmoid(x):
    return 1.0 / (1.0 + jnp.exp(-x))


def _rms(x):
    return x * lax.rsqrt(jnp.mean(x * x, axis=-1, keepdims=True) + EPS)


def _dot(a, b):
    return jnp.dot(a, b, preferred_element_type=F32)


def _dot_nt(a, b):
    return lax.dot_general(a, b, (((1,), (1,)), ((), ())), preferred_element_type=F32)


def _resident(shape):
    return pl.BlockSpec(shape, lambda *_: (0,) * len(shape), pipeline_mode=pl.Buffered(1))


def _mod_kernel(c_ref, w_ref, b_ref, o_ref):
    c = c_ref[...]
    s = c * _sigmoid(c)
    o_ref[...] = _dot(s.astype(BF16), w_ref[...].astype(BF16)) + b_ref[...]


def _mod_call(cvecs, w_ada, b_ada):
    n = w_ada.shape[1]
    return pl.pallas_call(
        _mod_kernel,
        grid=(n // MOD_TILE_N,),
        in_specs=[
            pl.BlockSpec((MOD_ROWS, D_MODEL), lambda j: (0, 0)),
            pl.BlockSpec((D_MODEL, MOD_TILE_N), lambda j: (0, j)),
            pl.BlockSpec((1, MOD_TILE_N), lambda j: (0, j)),
        ],
        out_specs=pl.BlockSpec((MOD_ROWS, MOD_TILE_N), lambda j: (0, j)),
        out_shape=jax.ShapeDtypeStruct((MOD_ROWS, n), F32),
        name="mod",
    )(cvecs, w_ada, b_ada)


def _ffn_kernel(*refs, sub, pre, final, n_side):
    refs = list(refs)
    x_ref, mod_ref = refs[:2]
    pos = 2
    if pre:
        od_ref, og_ref, wout_ref = refs[pos:pos + 3]
        pos += 3
    g_ref, wgu_ref, wd_ref = refs[pos:pos + 3]
    pos += 3
    if final:
        fg_ref = refs[pos]
        pos += 1
    side_in = refs[pos:pos + n_side]
    o_ref = refs[pos + n_side]
    side_out = refs[pos + n_side + 1:]

    x = x_ref[...]
    mod = mod_ref[0]
    if pre:
        o = jnp.concatenate([od_ref[...], og_ref[...]], axis=1)
        x = x + mod[5:6] * _dot(o, wout_ref[...])
    shift = mod[3 * sub:3 * sub + 1]
    scale = mod[3 * sub + 1:3 * sub + 2]
    gate = mod[3 * sub + 2:3 * sub + 3]
    h = (_rms(x) * g_ref[...]) * (1.0 + scale) + shift
    gu = _dot(h.astype(BF16), wgu_ref[...])
    g = gu[:, :D_FF]
    u = gu[:, D_FF:]
    act = (g * _sigmoid(g)) * u
    y = _dot(act.astype(BF16), wd_ref[...])
    x = x + (0.5 * gate) * y
    if final:
        x = _rms(x) * fg_ref[...]
    o_ref[...] = x
    for src, dst in zip(side_in, side_out):
        dst[...] = src[...].astype(BF16)


def _ffn_call(x, mod, row_map, gain, wgu, wd, *, sub, pre=None, final_gain=None, side_casts=(), name):
    t = x.shape[0]
    tm = FFN_TILE
    steps = t // tm
    row_spec = lambda w: pl.BlockSpec((tm, w), lambda i: (i, 0))
    in_specs = [row_spec(D_MODEL), pl.BlockSpec((1, N_MOD, D_MODEL), row_map)]
    args = [x, mod]
    if pre is not None:
        od, og, wout = pre
        in_specs += [row_spec(DQ_DIFF), row_spec(DQ_GQA), _resident((D_MIX, D_MODEL))]
        args += [od, og, wout]
    in_specs += [_resident((1, D_MODEL)), _resident((D_MODEL, 2 * D_FF)), _resident((D_FF, D_MODEL))]
    args += [gain, wgu, wd]
    if final_gain is not None:
        in_specs.append(_resident((1, D_MODEL)))
        args.append(final_gain)
    out_specs = [row_spec(D_MODEL)]
    out_shape = [jax.ShapeDtypeStruct((t, D_MODEL), F32)]
    for w in side_casts:
        rows, cols = w.shape
        assert rows % (steps * BF16_ROWS) == 0
        spec = pl.BlockSpec((rows // steps, cols), lambda i: (i, 0))
        in_specs.append(spec)
        args.append(w)
        out_specs.append(spec)
        out_shape.append(jax.ShapeDtypeStruct((rows, cols), BF16))
    outs = pl.pallas_call(
        functools.partial(_ffn_kernel, sub=sub, pre=pre is not None, final=final_gain is not None,
                          n_side=len(side_casts)),
        grid=(steps,),
        in_specs=in_specs,
        out_specs=out_specs,
        out_shape=out_shape,
        compiler_params=pltpu.CompilerParams(vmem_limit_bytes=VMEM_LIMIT_BYTES),
        name=name,
    )(*args)
    return outs if side_casts else outs[0]


def _rope(x, cos, sin, first_of_pair):
    w = x.shape[1]
    partner = jnp.where(first_of_pair, pltpu.roll(x, w - ROPE_HALF, 1), pltpu.roll(x, ROPE_HALF, 1))
    reps = w // LANES
    cos_w = jnp.concatenate([cos] * reps, axis=1) if reps > 1 else cos
    sin_w = jnp.concatenate([sin] * reps, axis=1) if reps > 1 else sin
    return x * cos_w + partner * sin_w


def _proj_kernel(*refs, rope, emit_f32, sum_row):
    refs = list(refs)
    x_ref, mod_ref, g_ref, w_ref, seg_ref, qkn_ref = refs[:6]
    pos = 6
    if rope:
        cos_ref, sin_ref = refs[pos:pos + 2]
        pos += 2
    q_ref, k_ref, vt_ref = refs[pos:pos + 3]
    pos += 3

    x = x_ref[...]
    mod = mod_ref[0]
    h = (_rms(x) * g_ref[...]) * (1.0 + mod[4:5]) + mod[3:4]
    qkv = _dot(h.astype(BF16), w_ref[...])
    dq = qkv[:, 0:DQ_DIFF]
    dk = qkv[:, DQ_DIFF:2 * DQ_DIFF]
    dv = qkv[:, 2 * DQ_DIFF:3 * DQ_DIFF]
    n0 = 3 * DQ_DIFF
    raw = qkv[:, n0:n0 + D_NORMED]
    gv = qkv[:, n0 + D_NORMED:]

    sq = raw * raw
    hi = sq.astype(BF16)
    lo = (sq - hi.astype(F32)).astype(BF16)
    seg = seg_ref[...]
    ss = jnp.concatenate(
        [_dot(jnp.concatenate([hi[:, c:c + LANES], lo[:, c:c + LANES]], axis=1), seg)
         for c in range(0, D_NORMED, LANES)], axis=1)
    normed = (raw * lax.rsqrt(ss * (1.0 / DH) + EPS)) * qkn_ref[...]
    gq = normed[:, :DQ_GQA]
    gk = normed[:, DQ_GQA:]

    if rope:
        cos = cos_ref[...]
        sin = sin_ref[...]
        lane = lax.broadcasted_iota(jnp.int32, (x.shape[0], LANES), 1)
        first = (lane % (2 * ROPE_HALF)) < ROPE_HALF
        first4 = jnp.concatenate([first] * 4, axis=1)
        first2 = jnp.concatenate([first] * 2, axis=1)
        dq = _rope(dq, cos, sin, first4)
        dk = _rope(dk, cos, sin, first4)
        gq = _rope(gq, cos, sin, first4)
        gk = _rope(gk, cos, sin, first2)

    if sum_row:
        lane = lax.broadcasted_iota(jnp.int32, (x.shape[0], LANES), 1)
        marker = jnp.where(lane == DH, 1.0, 0.0)
        gv = jnp.concatenate([jnp.where(lane < DH, gv[:, c:c + LANES], marker)
                              for c in range(0, gv.shape[1], LANES)], axis=1)

    q_ref[...] = jnp.concatenate([dq * SCORE_SCALE, gq * SCORE_SCALE], axis=1).astype(BF16)
    k_ref[...] = jnp.concatenate([dk, gk], axis=1).astype(BF16)
    vt_ref[...] = jnp.concatenate([dv, gv], axis=1).T.astype(BF16)

    if emit_f32:
        dkt_ref, dv4_ref, gkt_ref, gvt_ref = refs[pos:pos + 4]
        tm = x.shape[0]
        seq = dkt_ref.shape[2]
        lane = lax.broadcasted_iota(jnp.int32, (tm, LANES), 1)
        low = lane < DH
        dkt = dk.T
        gkt = jnp.where(low, gk[:, :LANES], gk[:, LANES:]).T
        gvt = jnp.where(low, gv[:, :LANES], gv[:, LANES:]).T
        for b in range(tm // seq):
            dkt_ref[b] = dkt[:, b * seq:(b + 1) * seq]
            gkt_ref[b] = gkt[:, b * seq:(b + 1) * seq]
            gvt_ref[b] = gvt[:, b * seq:(b + 1) * seq]
        for hd in range(H_DIFF):
            dv4_ref[pl.ds(hd, tm, stride=H_DIFF), :] = dv[:, hd * LANES:(hd + 1) * LANES]


def _proj_call(x, mod, row_map, gain, w_ext, seg, qkn, *, rope_tables, cache_seq, sum_row, name):
    t = x.shape[0]
    emit_f32 = cache_seq is not None
    tm = PROJ_TILE
    row_spec = lambda w: pl.BlockSpec((tm, w), lambda i: (i, 0))
    n_ext = w_ext.shape[1]
    in_specs = [row_spec(D_MODEL), pl.BlockSpec((1, N_MOD, D_MODEL), row_map),
                _resident((1, D_MODEL)), _resident((D_MODEL, n_ext)),
                _resident((2 * LANES, LANES)), _resident((1, D_NORMED))]
    args = [x, mod, gain, w_ext, seg, qkn]
    if rope_tables is not None:
        cos, sin = rope_tables
        tiles_per_seq = cos.shape[0] // tm
        tab_spec = pl.BlockSpec((tm, LANES), lambda i: (i % tiles_per_seq, 0))
        in_specs += [tab_spec, tab_spec]
        args += [cos, sin]
    out_specs = [row_spec(D_MIX), row_spec(D_KV_ALL), pl.BlockSpec((D_KV_ALL, tm), lambda i: (0, i))]
    out_shape = [jax.ShapeDtypeStruct((t, D_MIX), BF16),
                 jax.ShapeDtypeStruct((t, D_KV_ALL), BF16),
                 jax.ShapeDtypeStruct((D_KV_ALL, t), BF16)]
    if emit_f32:
        assert tm % cache_seq == 0
        nb = tm // cache_seq
        slab = lambda rows: pl.BlockSpec((nb, rows, cache_seq), lambda i: (i, 0, 0))
        out_specs += [slab(DQ_DIFF), pl.BlockSpec((tm * H_DIFF, LANES), lambda i: (i, 0)),
                      slab(DKV_GQA), slab(DKV_GQA)]
        out_shape += [jax.ShapeDtypeStruct((t // cache_seq, DQ_DIFF, cache_seq), F32),
                      jax.ShapeDtypeStruct((t * H_DIFF, LANES), F32),
                      jax.ShapeDtypeStruct((t // cache_seq, DKV_GQA, cache_seq), F32),
                      jax.ShapeDtypeStruct((t // cache_seq, DKV_GQA, cache_seq), F32)]
    return pl.pallas_call(
        functools.partial(_proj_kernel, rope=rope_tables is not None, emit_f32=emit_f32, sum_row=sum_row),
        grid=(t // tm,),
        in_specs=in_specs,
        out_specs=out_specs,
        out_shape=out_shape,
        compiler_params=pltpu.CompilerParams(vmem_limit_bytes=VMEM_LIMIT_BYTES),
        name=name,
    )(*args)


def _lambda(lamv_ref):
    lamv = lamv_ref[...]
    return (jnp.exp(jnp.sum(lamv[0:1] * lamv[1:2], axis=-1, keepdims=True))
            - jnp.exp(jnp.sum(lamv[2:3] * lamv[3:4], axis=-1, keepdims=True)) + LAMBDA_INIT)


def _fold_rows(x, op):
    return op(x.reshape(x.shape[0] // SUBLANES, SUBLANES, x.shape[1]), axis=0)


def _value_rows(kvc):
    return slice(kvc * LANES, (kvc + 1) * LANES)


def _combine_heads(is_diff, ot_a, ot_b, lam, subln):
    if is_diff:
        o = (ot_a - lam * ot_b).T
        return (_rms(o) * subln) * (1.0 - LAMBDA_INIT)
    return jnp.concatenate([ot_a[:DH], ot_b[:DH]], axis=0).T


def _attn_kernel(q_ref, k_ref, vt_ref, lamv_ref, subln_ref, od_ref, og_ref, *, items, maps_per_group):
    tq = q_ref.shape[1]
    low = lax.broadcasted_iota(jnp.int32, (tq, LANES), 1) < DH

    class SoftmaxMap:
        def __init__(self, qm, kvc):
            self.qm = qm
            self.cs = slice(kvc * LANES, (kvc + 1) * LANES)
            self.vrows = _value_rows(kvc)

        def score_pass(self):
            self.s = _dot_nt(k_ref[0, :, self.cs], self.qm)
            self.m = _fold_rows(self.s, jnp.max).max(axis=0, keepdims=True)

        def pv_pass(self):
            e = jnp.exp2(self.s - self.m)
            tot = _fold_rows(e, jnp.sum).sum(axis=0, keepdims=True)
            self.ot = _dot(vt_ref[self.vrows, :], e.astype(BF16)) * (1.0 / tot)

    maps = []
    for is_diff, qc, kvc, _ in items:
        q = q_ref[0, :, qc * LANES:(qc + 1) * LANES]
        zero = jnp.zeros_like(q)
        maps += [SoftmaxMap(jnp.where(low, q, zero), kvc), SoftmaxMap(jnp.where(low, zero, q), kvc)]

    groups = [maps[g:g + maps_per_group] for g in range(0, len(maps), maps_per_group)]
    for stage in range(len(groups) + 1):
        if stage > 0:
            for mp in groups[stage - 1]:
                mp.pv_pass()
        if stage < len(groups):
            for mp in groups[stage]:
                mp.score_pass()

    lam = _lambda(lamv_ref)
    for n, (is_diff, _, _, oc) in enumerate(items):
        o = _combine_heads(is_diff, maps[2 * n].ot, maps[2 * n + 1].ot, lam, subln_ref[...])
        out_ref = od_ref if is_diff else og_ref
        out_ref[0, :, oc * LANES:(oc + 1) * LANES] = o.astype(BF16)


def _attn_loop_kernel(q_ref, kn_ref, vtn_ref, kc_ref, vtc_ref, lamv_ref, subln_ref, out_ref, *scratch,
                      is_diff, kv_cols):
    tq = ATTN_TILE_Q
    n_tiles = q_ref.shape[1] // tq
    n_groups = len(kv_cols)
    s_refs = scratch[:n_groups]
    m_refs = scratch[n_groups:]
    t_new = kn_ref.shape[1]
    segs = [(kn_ref, vtn_ref, 0), (kc_ref, vtc_ref, t_new)]
    chunks = []
    for si, (k_ref, _, base) in enumerate(segs):
        n = k_ref.shape[1]
        chunks += [(si, c0, min(n, c0 + ATTN_CHUNK_K), base + c0) for c0 in range(0, n, ATTN_CHUNK_K)]
    n_chunks = len(chunks)
    per_tile = n_groups * n_chunks
    lead = n_chunks + ATTN_SKEW
    assert 0 < ATTN_SKEW < n_chunks and lead < per_tile

    low = lax.broadcasted_iota(jnp.int32, (tq, LANES), 1) < DH
    lam = _lambda(lamv_ref)

    def rows(tile):
        return pl.ds(pl.multiple_of(tile * tq, tq), tq)

    def masked_q(tile, g):
        q = q_ref[0, rows(tile), g * LANES:(g + 1) * LANES]
        zero = jnp.zeros_like(q)
        return jnp.concatenate([jnp.where(low, q, zero), jnp.where(low, zero, q)], axis=0)

    def score_unit(qm, g, ci, mpart):
        si, c0, c1, r0 = chunks[ci]
        cs = slice(kv_cols[g] * LANES, (kv_cols[g] + 1) * LANES)
        s = _dot_nt(segs[si][0][0, c0:c1, cs], qm)
        s_refs[g][r0:r0 + c1 - c0, :] = s
        part = _fold_rows(s, jnp.max)
        mpart = part if mpart is None else jnp.maximum(mpart, part)
        if ci == n_chunks - 1:
            m_refs[g][...] = mpart.max(axis=0, keepdims=True)
            return None
        return mpart

    def pv_unit(g, ci, m, acc, lpart):
        si, c0, c1, r0 = chunks[ci]
        e = jnp.exp2(s_refs[g][r0:r0 + c1 - c0, :] - m)
        pv = _dot(segs[si][1][_value_rows(kv_cols[g]), c0:c1], e.astype(BF16))
        acc = pv if acc is None else acc + pv
        if not is_diff:
            return acc, None
        part = _fold_rows(e, jnp.sum)
        return acc, (part if lpart is None else lpart + part)

    def emit(g, tile, acc, lpart):
        lsum = lpart.sum(axis=0, keepdims=True) if is_diff else acc[DH:DH + 1]
        ot = acc * (1.0 / lsum)
        o = _combine_heads(is_diff, ot[:, :tq], ot[:, tq:], lam, subln_ref[...])
        out_ref[0, rows(tile), g * LANES:(g + 1) * LANES] = o.astype(BF16)

    mpart = None
    qm = None
    for u in range(lead):
        g, ci = divmod(u, n_chunks)
        if ci == 0:
            qm = masked_q(0, g)
        mpart = score_unit(qm, g, ci, mpart)

    def body(t, carried_mpart):
        nxt = jnp.minimum(t + 1, n_tiles - 1)
        a_state = {}
        a_state[(lead // per_tile, (lead // n_chunks) % n_groups)] = (None, carried_mpart)
        acc = lpart = m = None
        for j in range(per_tile):
            b_g, b_ci = divmod(j, n_chunks)
            if b_ci == 0:
                m = m_refs[b_g][...]
                acc = lpart = None
            acc, lpart = pv_unit(b_g, b_ci, m, acc, lpart)
            if b_ci == n_chunks - 1:
                emit(b_g, t, acc, lpart)

            a_off, a_rem = divmod(j + lead, per_tile)
            a_g, a_ci = divmod(a_rem, n_chunks)
            a_qm, a_mpart = a_state.get((a_off, a_g), (None, None))
            if a_qm is None:
                a_qm = masked_q(t if a_off == 0 else nxt, a_g)
            a_state[(a_off, a_g)] = (a_qm, score_unit(a_qm, a_g, a_ci, a_mpart))

        last = per_tile - 1 + lead
        return a_state[(last // per_tile, (last % per_tile) // n_chunks)][1]

    lax.fori_loop(0, n_tiles, body, mpart)


def _attn_sample_call(q, k, vt, kc, vtc, lamv, subln, *, is_diff, name):
    b, t, _ = q.shape
    tc = kc.shape[1]
    qw = ATTN_COLS_PER_STEP * LANES
    nblk = DQ_DIFF // qw
    if is_diff:
        kw = qw
        q0 = 0
        kv0 = 0
        kv_cols = tuple(range(ATTN_COLS_PER_STEP))
    else:
        assert ATTN_COLS_PER_STEP * (LANES // DH) == G_GQA
        kw = LANES
        q0 = DQ_DIFF // qw
        kv0 = DQ_DIFF // kw
        kv_cols = (0,) * ATTN_COLS_PER_STEP
    in_specs = [
        pl.BlockSpec((1, t, qw), lambda bi, j: (bi, 0, q0 + j)),
        pl.BlockSpec((1, t, kw), lambda bi, j: (bi, 0, kv0 + j)),
        pl.BlockSpec((kw, t), lambda bi, j: (kv0 + j, bi)),
        pl.BlockSpec((1, tc, kw), lambda bi, j: (bi, 0, kv0 + j)),
        pl.BlockSpec((kw, tc), lambda bi, j: (kv0 + j, bi)),
        pl.BlockSpec((4, DH), lambda bi, j: (0, 0)),
        pl.BlockSpec((1, LANES), lambda bi, j: (0, 0)),
    ]
    return pl.pallas_call(
        functools.partial(_attn_loop_kernel, is_diff=is_diff, kv_cols=kv_cols),
        grid=(b, nblk),
        in_specs=in_specs,
        out_specs=pl.BlockSpec((1, t, qw), lambda bi, j: (bi, 0, j)),
        out_shape=jax.ShapeDtypeStruct((b, t, nblk * qw), BF16),
        scratch_shapes=([pltpu.VMEM((t + tc, 2 * ATTN_TILE_Q), F32)] * ATTN_COLS_PER_STEP
                        + [pltpu.VMEM((1, 2 * ATTN_TILE_Q), F32)] * ATTN_COLS_PER_STEP),
        compiler_params=pltpu.CompilerParams(vmem_limit_bytes=VMEM_LIMIT_BYTES),
        name=name,
    )(q, k, vt, kc, vtc, lamv, subln)


def _attn_ctx_call(q, k, vt, lamv, subln, *, name):
    b, t, _ = q.shape
    ncol = DQ_DIFF // LANES
    items = tuple((True, j, j, j) for j in range(ncol)) + tuple(
        (False, ncol + j, ncol + j // (LANES // DH), j) for j in range(ncol))
    whole = lambda w: pl.BlockSpec((1, t, w), lambda bi: (bi, 0, 0))
    return pl.pallas_call(
        functools.partial(_attn_kernel, items=items, maps_per_group=ATTN_CTX_MAPS_PER_GROUP),
        grid=(b,),
        in_specs=[whole(D_MIX), whole(D_KV_ALL),
                  pl.BlockSpec((D_KV_ALL, t), lambda bi: (0, bi)),
                  pl.BlockSpec((4, DH), lambda bi: (0, 0)),
                  pl.BlockSpec((1, LANES), lambda bi: (0, 0))],
        out_specs=[whole(DQ_DIFF), whole(DQ_GQA)],
        out_shape=[jax.ShapeDtypeStruct((b, t, DQ_DIFF), BF16),
                   jax.ShapeDtypeStruct((b, t, DQ_GQA), BF16)],
        compiler_params=pltpu.CompilerParams(vmem_limit_bytes=VMEM_LIMIT_BYTES),
        name=name,
    )(q, k, vt, lamv, subln)


def _rope_tables(n_tokens):
    t = np.arange(n_tokens)
    row = (t // GRID_W).astype(np.float32)
    col = (t % GRID_W).astype(np.float32)
    half = DH // 2
    inv = np.float32(ROPE_THETA) ** (-(np.arange(0, half, 2, dtype=np.float32) / np.float32(half)))
    ang_r = row[:, None] * inv
    ang_c = col[:, None] * inv
    cos = np.concatenate([np.cos(ang_r)] * 2 + [np.cos(ang_c)] * 2, axis=1)
    sin = np.concatenate([-np.sin(ang_r), np.sin(ang_r), -np.sin(ang_c), np.sin(ang_c)], axis=1)
    reps = LANES // DH
    return (jnp.asarray(np.concatenate([cos] * reps, axis=1), F32),
            jnp.asarray(np.concatenate([sin] * reps, axis=1), F32))


def _dup_heads(a):
    parts = []
    for n in range(H_KV):
        head = a[..., n * DH:(n + 1) * DH]
        parts += [head, head]
    return jnp.concatenate(parts, axis=-1)


def _with_sum_feature(a):
    ones = jnp.ones(a.shape[:-1] + (1,), a.dtype)
    zeros = jnp.zeros(a.shape[:-1] + (DH - 1,), a.dtype)
    parts = []
    for n in range(H_KV):
        parts += [a[..., n * DH:(n + 1) * DH], ones, zeros]
    return jnp.concatenate(parts, axis=-1)


def kernel(x_prompt, x_sample, c, cache_diff_k, cache_diff_v, cache_gqa_k, cache_gqa_v, c_ctx, w_ada, b_ada, norm_ff1, w_ff1_gu, w_ff1_down, norm_mix, w_in, q_norm, k_norm, lambda_q1, lambda_k1, lambda_q2, lambda_k2, subln, w_out, norm_ff2, w_ff2_gu, w_ff2_down, final_norm):
    assert w_ada.shape[0] == 1, "single trunk layer"
    bc, tc_, _ = x_prompt.shape
    bs, ts, _ = x_sample.shape
    tpast = cache_diff_k.shape[2]
    assert bs + 1 <= MOD_ROWS
    ctx_row = bs

    cvecs = jnp.concatenate([c, c_ctx[None], jnp.zeros((MOD_ROWS - bs - 1, D_MODEL), F32)], axis=0)
    wgu1 = w_ff1_gu[0].astype(BF16)
    wd1 = w_ff1_down[0].astype(BF16)
    seg128 = (jnp.arange(LANES)[:, None] // DH == jnp.arange(LANES)[None, :] // DH).astype(BF16)
    seg = jnp.concatenate([seg128, seg128], axis=0)
    qkn = jnp.concatenate([jnp.tile(q_norm[0], H_GQA), jnp.tile(k_norm[0], 2 * H_KV)])[None]
    lamv = jnp.stack([lambda_q1[0], lambda_k1[0], lambda_q2[0], lambda_k2[0]])
    g1 = norm_ff1
    gm = norm_mix
    g2 = norm_ff2
    fg = final_norm[None]
    cos, sin = _rope_tables(ts)
    kcache = jnp.concatenate([cache_diff_k[:, 0].reshape(bs, tpast, DQ_DIFF),
                              _dup_heads(cache_gqa_k[:, 0].reshape(bs, tpast, DKV_GQA))], axis=-1).astype(BF16)
    vcache = jnp.concatenate([cache_diff_v[:, 0].reshape(bs, tpast, DQ_DIFF),
                              _with_sum_feature(cache_gqa_v[:, 0].reshape(bs, tpast, DKV_GQA))], axis=-1).astype(BF16)
    vcache_t = vcache.transpose(2, 0, 1).reshape(D_KV_ALL, bs * tpast)

    mod = _mod_call(cvecs, w_ada[0], b_ada).reshape(MOD_ROWS, N_MOD, D_MODEL)

    ctx_map = lambda i: (ctx_row, 0, 0)
    xc = x_prompt.reshape(bc * tc_, D_MODEL)
    xc, wgu2, wd2, wout, wi = _ffn_call(xc, mod, ctx_map, g1, wgu1, wd1, sub=0,
                                        side_casts=(w_ff2_gu[0], w_ff2_down[0], w_out[0], w_in[0]),
                                        name="ffn1_ctx")
    n_qkvq = 3 * DQ_DIFF + DQ_GQA
    w_ext = jnp.concatenate(
        [wi[:, :n_qkvq], _dup_heads(wi[:, n_qkvq:n_qkvq + DKV_GQA]), _dup_heads(wi[:, n_qkvq + DKV_GQA:])], axis=1)
    qc, kc, vtc, dkt32, dv4, gkt32, gvt32 = _proj_call(
        xc, mod, ctx_map, gm, w_ext, seg, qkn, rope_tables=None, cache_seq=tc_, sum_row=False, name="proj_ctx")
    odc, ogc = _attn_ctx_call(qc.reshape(bc, tc_, D_MIX), kc.reshape(bc, tc_, D_KV_ALL), vtc,
                              lamv, subln, name="attn_ctx")
    yc = _ffn_call(xc, mod, ctx_map, g2, wgu2, wd2, sub=2,
                   pre=(odc.reshape(bc * tc_, DQ_DIFF), ogc.reshape(bc * tc_, DQ_GQA), wout),
                   final_gain=fg, name="ffn2_ctx")

    ffn_tiles = ts // FFN_TILE
    proj_tiles = ts // PROJ_TILE
    xs = x_sample.reshape(bs * ts, D_MODEL)
    xs = _ffn_call(xs, mod, lambda i: (i // ffn_tiles, 0, 0), g1, wgu1, wd1, sub=0, name="ffn1_smp")
    qs, ks, vts = _proj_call(xs, mod, lambda i: (i // proj_tiles, 0, 0), gm, w_ext, seg, qkn,
                             rope_tables=(cos, sin), cache_seq=None, sum_row=True, name="proj_smp")
    qs = qs.reshape(bs, ts, D_MIX)
    ks = ks.reshape(bs, ts, D_KV_ALL)
    ods = _attn_sample_call(qs, ks, vts, kcache, vcache_t, lamv, subln, is_diff=True, name="attn_diff_smp")
    ogs = _attn_sample_call(qs, ks, vts, kcache, vcache_t, lamv, subln, is_diff=False, name="attn_gqa_smp")
    ys = _ffn_call(xs, mod, lambda i: (i // ffn_tiles, 0, 0), g2, wgu2, wd2, sub=2,
                   pre=(ods.reshape(bs * ts, DQ_DIFF), ogs.reshape(bs * ts, DQ_GQA), wout),
                   final_gain=fg, name="ffn2_smp")

    return (yc.reshape(bc, tc_, D_MODEL),
            ys.reshape(bs, ts, D_MODEL),
            dkt32.reshape(bc, 1, H_DIFF, 2, DH, tc_).transpose(0, 1, 5, 2, 3, 4),
            dv4.reshape(bc, 1, tc_, H_DIFF, 2 * DH),
            gkt32.reshape(bc, 1, H_KV, DH, tc_).transpose(0, 1, 4, 2, 3),
            gvt32.reshape(bc, 1, H_KV, DH, tc_).transpose(0, 1, 4, 2, 3))
```

```python
import functools
import math

import jax
import jax.numpy as jnp
import numpy as np
from jax import lax
from jax.experimental import pallas as pl
from jax.experimental.pallas import tpu as pltpu

F32 = jnp.float32
BF16 = jnp.bfloat16

D_MODEL = 1024
N_MOD = 9
D_FF = 2816
H_DIFF = 4
DH = 64
H_GQA = 8
H_KV = 2
G_GQA = H_GQA // H_KV
GRID_W = 64
ROPE_THETA = 10000.0
EPS = 1e-6
LAMBDA_INIT = 0.8 - 0.6 * math.exp(-0.3 * 0)
DQ_DIFF = H_DIFF * 2 * DH
DQ_GQA = H_GQA * DH
DKV_GQA = H_KV * DH
ROPE_HALF = DH // 4
D_MIX = DQ_DIFF + DQ_GQA
D_KV_ALL = DQ_DIFF + 2 * DKV_GQA
D_NORMED = DQ_GQA + 2 * DKV_GQA
SCORE_SCALE = DH ** -0.5 * math.log2(math.e)

LANES = 128
SUBLANES = 8
BF16_ROWS = 16
VMEM_LIMIT_BYTES = 60000 * 1024

MOD_ROWS = 8
MOD_TILE_N = 4608
FFN_TILE = 1024
FFN_HIDDEN_SPLIT = 1536
PROJ_TILE = 1024
ATTN_TILE_Q = 256
ATTN_CHUNK_K = 512
ATTN_SKEW = 2
ATTN_COLS_PER_STEP = 2
ATTN_CTX_MAPS_PER_GROUP = 8


def _sigmoid(x):
    return 1.0 / (1.0 + jnp.exp(-x))


def _rms(x):
    return x * lax.rsqrt(jnp.mean(x * x, axis=-1, keepdims=True) + EPS)


def _dot(a, b):
    return jnp.dot(a, b, preferred_element_type=F32)


def _dot_nt(a, b):
    return lax.dot_general(a, b, (((1,), (1,)), ((), ())), preferred_element_type=F32)


def _resident(shape):
    return pl.BlockSpec(shape, lambda *_: (0,) * len(shape), pipeline_mode=pl.Buffered(1))


def _mod_kernel(c_ref, w_ref, b_ref, o_ref):
    c = c_ref[...]
    s = c * _sigmoid(c)
    o_ref[...] = _dot(s.astype(BF16), w_ref[...].astype(BF16)) + b_ref[...]


def _mod_call(cvecs, w_ada, b_ada):
    n = w_ada.shape[1]
    return pl.pallas_call(
        _mod_kernel,
        grid=(n // MOD_TILE_N,),
        in_specs=[
            pl.BlockSpec((MOD_ROWS, D_MODEL), lambda j: (0, 0)),
            pl.BlockSpec((D_MODEL, MOD_TILE_N), lambda j: (0, j)),
            pl.BlockSpec((1, MOD_TILE_N), lambda j: (0, j)),
        ],
        out_specs=pl.BlockSpec((MOD_ROWS, MOD_TILE_N), lambda j: (0, j)),
        out_shape=jax.ShapeDtypeStruct((MOD_ROWS, n), F32),
        name="mod",
    )(cvecs, w_ada, b_ada)


def _ffn_kernel(*refs, sub, pre, final, n_side):
    refs = list(refs)
    x_ref, mod_ref = refs[:2]
    pos = 2
    if pre:
        od_ref, og_ref, wout_ref = refs[pos:pos + 3]
        pos += 3
    g_ref, wgu_ref, wd_ref = refs[pos:pos + 3]
    pos += 3
    if final:
        fg_ref = refs[pos]
        pos += 1
    side_in = refs[pos:pos + n_side]
    o_ref = refs[pos + n_side]
    side_out = refs[pos + n_side + 1:]

    x = x_ref[...]
    mod = mod_ref[0]
    if pre:
        o = jnp.concatenate([od_ref[...], og_ref[...]], axis=1)
        x = x + mod[5:6] * _dot(o, wout_ref[...])
    shift = mod[3 * sub:3 * sub + 1]
    scale = mod[3 * sub + 1:3 * sub + 2]
    gate = mod[3 * sub + 2:3 * sub + 3]
    h = (_rms(x) * g_ref[...]) * (1.0 + scale) + shift
    hb = h.astype(BF16)
    y = None
    for j0, j1 in ((0, FFN_HIDDEN_SPLIT), (FFN_HIDDEN_SPLIT, D_FF)):
        g = _dot(hb, wgu_ref[:, j0:j1])
        u = _dot(hb, wgu_ref[:, D_FF + j0:D_FF + j1])
        act = (g * _sigmoid(g)) * u
        yj = _dot(act.astype(BF16), wd_ref[j0:j1, :])
        y = yj if y is None else y + yj
    x = x + (0.5 * gate) * y
    if final:
        x = _rms(x) * fg_ref[...]
    o_ref[...] = x
    for src, dst in zip(side_in, side_out):
        dst[...] = src[...].astype(BF16)


def _ffn_call(x, mod, row_map, gain, wgu, wd, *, sub, pre=None, final_gain=None, side_casts=(), name):
    t = x.shape[0]
    tm = FFN_TILE
    steps = t // tm
    row_spec = lambda w: pl.BlockSpec((tm, w), lambda i: (i, 0))
    in_specs = [row_spec(D_MODEL), pl.BlockSpec((1, N_MOD, D_MODEL), row_map)]
    args = [x, mod]
    if pre is not None:
        od, og, wout = pre
        in_specs += [row_spec(DQ_DIFF), row_spec(DQ_GQA), _resident((D_MIX, D_MODEL))]
        args += [od, og, wout]
    in_specs += [_resident((1, D_MODEL)), _resident((D_MODEL, 2 * D_FF)), _resident((D_FF, D_MODEL))]
    args += [gain, wgu, wd]
    if final_gain is not None:
        in_specs.append(_resident((1, D_MODEL)))
        args.append(final_gain)
    out_specs = [row_spec(D_MODEL)]
    out_shape = [jax.ShapeDtypeStruct((t, D_MODEL), F32)]
    for w in side_casts:
        rows, cols = w.shape
        assert rows % (steps * BF16_ROWS) == 0
        spec = pl.BlockSpec((rows // steps, cols), lambda i: (i, 0))
        in_specs.append(spec)
        args.append(w)
        out_specs.append(spec)
        out_shape.append(jax.ShapeDtypeStruct((rows, cols), BF16))
    outs = pl.pallas_call(
        functools.partial(_ffn_kernel, sub=sub, pre=pre is not None, final=final_gain is not None,
                          n_side=len(side_casts)),
        grid=(steps,),
        in_specs=in_specs,
        out_specs=out_specs,
        out_shape=out_shape,
        compiler_params=pltpu.CompilerParams(vmem_limit_bytes=VMEM_LIMIT_BYTES),
        name=name,
    )(*args)
    return outs if side_casts else outs[0]


def _rope(x, cos, sin, first_of_pair):
    w = x.shape[1]
    partner = jnp.where(first_of_pair, pltpu.roll(x, w - ROPE_HALF, 1), pltpu.roll(x, ROPE_HALF, 1))
    reps = w // LANES
    cos_w = jnp.concatenate([cos] * reps, axis=1) if reps > 1 else cos
    sin_w = jnp.concatenate([sin] * reps, axis=1) if reps > 1 else sin
    return x * cos_w + partner * sin_w


def _proj_kernel(*refs, rope, emit_f32, sum_row):
    refs = list(refs)
    x_ref, mod_ref, g_ref, w_ref, seg_ref, qkn_ref = refs[:6]
    pos = 6
    if rope:
        cos_ref, sin_ref = refs[pos:pos + 2]
        pos += 2
    q_ref, k_ref, vt_ref = refs[pos:pos + 3]
    pos += 3

    x = x_ref[...]
    mod = mod_ref[0]
    h = (_rms(x) * g_ref[...]) * (1.0 + mod[4:5]) + mod[3:4]
    qkv = _dot(h.astype(BF16), w_ref[...])
    dq = qkv[:, 0:DQ_DIFF]
    dk = qkv[:, DQ_DIFF:2 * DQ_DIFF]
    dv = qkv[:, 2 * DQ_DIFF:3 * DQ_DIFF]
    n0 = 3 * DQ_DIFF
    raw = qkv[:, n0:n0 + D_NORMED]
    gv = qkv[:, n0 + D_NORMED:]

    sq = raw * raw
    hi = sq.astype(BF16)
    lo = (sq - hi.astype(F32)).astype(BF16)
    seg = seg_ref[...]
    ss = jnp.concatenate(
        [_dot(jnp.concatenate([hi[:, c:c + LANES], lo[:, c:c + LANES]], axis=1), seg)
         for c in range(0, D_NORMED, LANES)], axis=1)
    normed = (raw * lax.rsqrt(ss * (1.0 / DH) + EPS)) * qkn_ref[...]
    gq = normed[:, :DQ_GQA]
    gk = normed[:, DQ_GQA:]

    if rope:
        cos = cos_ref[...]
        sin = sin_ref[...]
        lane = lax.broadcasted_iota(jnp.int32, (x.shape[0], LANES), 1)
        first = (lane % (2 * ROPE_HALF)) < ROPE_HALF
        first4 = jnp.concatenate([first] * 4, axis=1)
        first2 = jnp.concatenate([first] * 2, axis=1)
        dq = _rope(dq, cos, sin, first4)
        dk = _rope(dk, cos, sin, first4)
        gq = _rope(gq, cos, sin, first4)
        gk = _rope(gk, cos, sin, first2)

    if sum_row:
        lane = lax.broadcasted_iota(jnp.int32, (x.shape[0], LANES), 1)
        marker = jnp.where(lane == DH, 1.0, 0.0)
        gv = jnp.concatenate([jnp.where(lane < DH, gv[:, c:c + LANES], marker)
                              for c in range(0, gv.shape[1], LANES)], axis=1)

    q_ref[...] = jnp.concatenate([dq * SCORE_SCALE, gq * SCORE_SCALE], axis=1).astype(BF16)
    k_ref[...] = jnp.concatenate([dk, gk], axis=1).astype(BF16)
    vt_ref[...] = jnp.concatenate([dv, gv], axis=1).T.astype(BF16)

    if emit_f32:
        dkt_ref, dv4_ref, gkt_ref, gvt_ref = refs[pos:pos + 4]
        tm = x.shape[0]
        seq = dkt_ref.shape[2]
        lane = lax.broadcasted_iota(jnp.int32, (tm, LANES), 1)
        low = lane < DH
        dkt = dk.T
        gkt = jnp.where(low, gk[:, :LANES], gk[:, LANES:]).T
        gvt = jnp.where(low, gv[:, :LANES], gv[:, LANES:]).T
        for b in range(tm // seq):
            dkt_ref[b] = dkt[:, b * seq:(b + 1) * seq]
            gkt_ref[b] = gkt[:, b * seq:(b + 1) * seq]
            gvt_ref[b] = gvt[:, b * seq:(b + 1) * seq]
        for hd in range(H_DIFF):
            dv4_ref[pl.ds(hd, tm, stride=H_DIFF), :] = dv[:, hd * LANES:(hd + 1) * LANES]


def _proj_call(x, mod, row_map, gain, w_ext, seg, qkn, *, rope_tables, cache_seq, sum_row, name):
    t = x.shape[0]
    emit_f32 = cache_seq is not None
    tm = PROJ_TILE
    row_spec = lambda w: pl.BlockSpec((tm, w), lambda i: (i, 0))
    n_ext = w_ext.shape[1]
    in_specs = [row_spec(D_MODEL), pl.BlockSpec((1, N_MOD, D_MODEL), row_map),
                _resident((1, D_MODEL)), _resident((D_MODEL, n_ext)),
                _resident((2 * LANES, LANES)), _resident((1, D_NORMED))]
    args = [x, mod, gain, w_ext, seg, qkn]
    if rope_tables is not None:
        cos, sin = rope_tables
        tiles_per_seq = cos.shape[0] // tm
        tab_spec = pl.BlockSpec((tm, LANES), lambda i: (i % tiles_per_seq, 0))
        in_specs += [tab_spec, tab_spec]
        args += [cos, sin]
    out_specs = [row_spec(D_MIX), row_spec(D_KV_ALL), pl.BlockSpec((D_KV_ALL, tm), lambda i: (0, i))]
    out_shape = [jax.ShapeDtypeStruct((t, D_MIX), BF16),
                 jax.ShapeDtypeStruct((t, D_KV_ALL), BF16),
                 jax.ShapeDtypeStruct((D_KV_ALL, t), BF16)]
    if emit_f32:
        assert tm % cache_seq == 0
        nb = tm // cache_seq
        slab = lambda rows: pl.BlockSpec((nb, rows, cache_seq), lambda i: (i, 0, 0))
        out_specs += [slab(DQ_DIFF), pl.BlockSpec((tm * H_DIFF, LANES), lambda i: (i, 0)),
                      slab(DKV_GQA), slab(DKV_GQA)]
        out_shape += [jax.ShapeDtypeStruct((t // cache_seq, DQ_DIFF, cache_seq), F32),
                      jax.ShapeDtypeStruct((t * H_DIFF, LANES), F32),
                      jax.ShapeDtypeStruct((t // cache_seq, DKV_GQA, cache_seq), F32),
                      jax.ShapeDtypeStruct((t // cache_seq, DKV_GQA, cache_seq), F32)]
    return pl.pallas_call(
        functools.partial(_proj_kernel, rope=rope_tables is not None, emit_f32=emit_f32, sum_row=sum_row),
        grid=(t // tm,),
        in_specs=in_specs,
        out_specs=out_specs,
        out_shape=out_shape,
        compiler_params=pltpu.CompilerParams(vmem_limit_bytes=VMEM_LIMIT_BYTES),
        name=name,
    )(*args)


def _lambda(lamv_ref):
    lamv = lamv_ref[...]
    return (jnp.exp(jnp.sum(lamv[0:1] * lamv[1:2], axis=-1, keepdims=True))
            - jnp.exp(jnp.sum(lamv[2:3] * lamv[3:4], axis=-1, keepdims=True)) + LAMBDA_INIT)


def _fold_rows(x, op):
    return op(x.reshape(x.shape[0] // SUBLANES, SUBLANES, x.shape[1]), axis=0)


def _value_rows(kvc):
    return slice(kvc * LANES, (kvc + 1) * LANES)


def _combine_heads(is_diff, ot_a, ot_b, lam, subln):
    if is_diff:
        o = (ot_a - lam * ot_b).T
        return (_rms(o) * subln) * (1.0 - LAMBDA_INIT)
    return jnp.concatenate([ot_a[:DH], ot_b[:DH]], axis=0).T


def _attn_kernel(q_ref, k_ref, vt_ref, lamv_ref, subln_ref, od_ref, og_ref, *, items, maps_per_group):
    tq = q_ref.shape[1]
    low = lax.broadcasted_iota(jnp.int32, (tq, LANES), 1) < DH

    class SoftmaxMap:
        def __init__(self, qm, kvc):
            self.qm = qm
            self.cs = slice(kvc * LANES, (kvc + 1) * LANES)
            self.vrows = _value_rows(kvc)

        def score_pass(self):
            self.s = _dot_nt(k_ref[0, :, self.cs], self.qm)
            self.m = _fold_rows(self.s, jnp.max).max(axis=0, keepdims=True)

        def pv_pass(self):
            e = jnp.exp2(self.s - self.m)
            tot = _fold_rows(e, jnp.sum).sum(axis=0, keepdims=True)
            self.ot = _dot(vt_ref[self.vrows, :], e.astype(BF16)) * (1.0 / tot)

    maps = []
    for is_diff, qc, kvc, _ in items:
        q = q_ref[0, :, qc * LANES:(qc + 1) * LANES]
        zero = jnp.zeros_like(q)
        maps += [SoftmaxMap(jnp.where(low, q, zero), kvc), SoftmaxMap(jnp.where(low, zero, q), kvc)]

    groups = [maps[g:g + maps_per_group] for g in range(0, len(maps), maps_per_group)]
    for stage in range(len(groups) + 1):
        if stage > 0:
            for mp in groups[stage - 1]:
                mp.pv_pass()
        if stage < len(groups):
            for mp in groups[stage]:
                mp.score_pass()

    lam = _lambda(lamv_ref)
    for n, (is_diff, _, _, oc) in enumerate(items):
        o = _combine_heads(is_diff, maps[2 * n].ot, maps[2 * n + 1].ot, lam, subln_ref[...])
        out_ref = od_ref if is_diff else og_ref
        out_ref[0, :, oc * LANES:(oc + 1) * LANES] = o.astype(BF16)


def _attn_loop_kernel(q_ref, kn_ref, vtn_ref, kc_ref, vtc_ref, lamv_ref, subln_ref, out_ref, *scratch,
                      is_diff, kv_cols):
    tq = ATTN_TILE_Q
    n_tiles = q_ref.shape[1] // tq
    n_groups = len(kv_cols)
    s_refs = scratch[:n_groups]
    m_refs = scratch[n_groups:]
    t_new = kn_ref.shape[1]
    segs = [(kn_ref, vtn_ref, 0), (kc_ref, vtc_ref, t_new)]
    chunks = []
    for si, (k_ref, _, base) in enumerate(segs):
        n = k_ref.shape[1]
        chunks += [(si, c0, min(n, c0 + ATTN_CHUNK_K), base + c0) for c0 in range(0, n, ATTN_CHUNK_K)]
    n_chunks = len(chunks)
    per_tile = n_groups * n_chunks
    lead = n_chunks + ATTN_SKEW
    assert 0 < ATTN_SKEW < n_chunks and lead < per_tile

    low = lax.broadcasted_iota(jnp.int32, (tq, LANES), 1) < DH
    lam = _lambda(lamv_ref)

    def rows(tile):
        return pl.ds(pl.multiple_of(tile * tq, tq), tq)

    def masked_q(tile, g):
        q = q_ref[0, rows(tile), g * LANES:(g + 1) * LANES]
        zero = jnp.zeros_like(q)
        return jnp.concatenate([jnp.where(low, q, zero), jnp.where(low, zero, q)], axis=0)

    def score_unit(qm, g, ci, mpart):
        si, c0, c1, r0 = chunks[ci]
        cs = slice(kv_cols[g] * LANES, (kv_cols[g] + 1) * LANES)
        s = _dot_nt(segs[si][0][0, c0:c1, cs], qm)
        s_refs[g][r0:r0 + c1 - c0, :] = s
        part = _fold_rows(s, jnp.max)
        mpart = part if mpart is None else jnp.maximum(mpart, part)
        if ci == n_chunks - 1:
            m_refs[g][...] = mpart.max(axis=0, keepdims=True)
            return None
        return mpart

    def pv_unit(g, ci, m, acc, lpart):
        si, c0, c1, r0 = chunks[ci]
        e = jnp.exp2(s_refs[g][r0:r0 + c1 - c0, :] - m)
        pv = _dot(segs[si][1][_value_rows(kv_cols[g]), c0:c1], e.astype(BF16))
        acc = pv if acc is None else acc + pv
        if not is_diff:
            return acc, None
        part = _fold_rows(e, jnp.sum)
        return acc, (part if lpart is None else lpart + part)

    def emit(g, tile, acc, lpart):
        lsum = lpart.sum(axis=0, keepdims=True) if is_diff else acc[DH:DH + 1]
        ot = acc * (1.0 / lsum)
        o = _combine_heads(is_diff, ot[:, :tq], ot[:, tq:], lam, subln_ref[...])
        out_ref[0, rows(tile), g * LANES:(g + 1) * LANES] = o.astype(BF16)

    mpart = None
    qm = None
    for u in range(lead):
        g, ci = divmod(u, n_chunks)
        if ci == 0:
            qm = masked_q(0, g)
        mpart = score_unit(qm, g, ci, mpart)

    def body(t, carried_mpart):
        nxt = jnp.minimum(t + 1, n_tiles - 1)
        a_state = {}
        a_state[(lead // per_tile, (lead // n_chunks) % n_groups)] = (None, carried_mpart)
        acc = lpart = m = None
        for j in range(per_tile):
            b_g, b_ci = divmod(j, n_chunks)
            if b_ci == 0:
                m = m_refs[b_g][...]
                acc = lpart = None
            acc, lpart = pv_unit(b_g, b_ci, m, acc, lpart)
            if b_ci == n_chunks - 1:
                emit(b_g, t, acc, lpart)

            a_off, a_rem = divmod(j + lead, per_tile)
            a_g, a_ci = divmod(a_rem, n_chunks)
            a_qm, a_mpart = a_state.get((a_off, a_g), (None, None))
            if a_qm is None:
                a_qm = masked_q(t if a_off == 0 else nxt, a_g)
            a_state[(a_off, a_g)] = (a_qm, score_unit(a_qm, a_g, a_ci, a_mpart))

        last = per_tile - 1 + lead
        return a_state[(last // per_tile, (last % per_tile) // n_chunks)][1]

    lax.fori_loop(0, n_tiles, body, mpart)


def _attn_sample_call(q, k, vt, kc, vtc, lamv, subln, *, is_diff, name):
    b, t, _ = q.shape
    tc = kc.shape[1]
    qw = ATTN_COLS_PER_STEP * LANES
    nblk = DQ_DIFF // qw
    if is_diff:
        kw = qw
        q0 = 0
        kv0 = 0
        kv_cols = tuple(range(ATTN_COLS_PER_STEP))
    else:
        assert ATTN_COLS_PER_STEP * (LANES // DH) == G_GQA
        kw = LANES
        q0 = DQ_DIFF // qw
        kv0 = DQ_DIFF // kw
        kv_cols = (0,) * ATTN_COLS_PER_STEP
    in_specs = [
        pl.BlockSpec((1, t, qw), lambda bi, j: (bi, 0, q0 + j)),
        pl.BlockSpec((1, t, kw), lambda bi, j: (bi, 0, kv0 + j)),
        pl.BlockSpec((kw, t), lambda bi, j: (kv0 + j, bi)),
        pl.BlockSpec((1, tc, kw), lambda bi, j: (bi, 0, kv0 + j)),
        pl.BlockSpec((kw, tc), lambda bi, j: (kv0 + j, bi)),
        pl.BlockSpec((4, DH), lambda bi, j: (0, 0)),
        pl.BlockSpec((1, LANES), lambda bi, j: (0, 0)),
    ]
    return pl.pallas_call(
        functools.partial(_attn_loop_kernel, is_diff=is_diff, kv_cols=kv_cols),
        grid=(b, nblk),
        in_specs=in_specs,
        out_specs=pl.BlockSpec((1, t, qw), lambda bi, j: (bi, 0, j)),
        out_shape=jax.ShapeDtypeStruct((b, t, nblk * qw), BF16),
        scratch_shapes=([pltpu.VMEM((t + tc, 2 * ATTN_TILE_Q), F32)] * ATTN_COLS_PER_STEP
                        + [pltpu.VMEM((1, 2 * ATTN_TILE_Q), F32)] * ATTN_COLS_PER_STEP),
        compiler_params=pltpu.CompilerParams(vmem_limit_bytes=VMEM_LIMIT_BYTES),
        name=name,
    )(q, k, vt, kc, vtc, lamv, subln)


def _attn_ctx_call(q, k, vt, lamv, subln, *, name):
    b, t, _ = q.shape
    ncol = DQ_DIFF // LANES
    items = tuple((True, j, j, j) for j in range(ncol)) + tuple(
        (False, ncol + j, ncol + j // (LANES // DH), j) for j in range(ncol))
    whole = lambda w: pl.BlockSpec((1, t, w), lambda bi: (bi, 0, 0))
    return pl.pallas_call(
        functools.partial(_attn_kernel, items=items, maps_per_group=ATTN_CTX_MAPS_PER_GROUP),
        grid=(b,),
        in_specs=[whole(D_MIX), whole(D_KV_ALL),
                  pl.BlockSpec((D_KV_ALL, t), lambda bi: (0, bi)),
                  pl.BlockSpec((4, DH), lambda bi: (0, 0)),
                  pl.BlockSpec((1, LANES), lambda bi: (0, 0))],
        out_specs=[whole(DQ_DIFF), whole(DQ_GQA)],
        out_shape=[jax.ShapeDtypeStruct((b, t, DQ_DIFF), BF16),
                   jax.ShapeDtypeStruct((b, t, DQ_GQA), BF16)],
        compiler_params=pltpu.CompilerParams(vmem_limit_bytes=VMEM_LIMIT_BYTES),
        name=name,
    )(q, k, vt, lamv, subln)


def _rope_tables(n_tokens):
    t = np.arange(n_tokens)
    row = (t // GRID_W).astype(np.float32)
    col = (t % GRID_W).astype(np.float32)
    half = DH // 2
    inv = np.float32(ROPE_THETA) ** (-(np.arange(0, half, 2, dtype=np.float32) / np.float32(half)))
    ang_r = row[:, None] * inv
    ang_c = col[:, None] * inv
    cos = np.concatenate([np.cos(ang_r)] * 2 + [np.cos(ang_c)] * 2, axis=1)
    sin = np.concatenate([-np.sin(ang_r), np.sin(ang_r), -np.sin(ang_c), np.sin(ang_c)], axis=1)
    reps = LANES // DH
    return (jnp.asarray(np.concatenate([cos] * reps, axis=1), F32),
            jnp.asarray(np.concatenate([sin] * reps, axis=1), F32))


def _dup_heads(a):
    parts = []
    for n in range(H_KV):
        head = a[..., n * DH:(n + 1) * DH]
        parts += [head, head]
    return jnp.concatenate(parts, axis=-1)


def _with_sum_feature(a):
    ones = jnp.ones(a.shape[:-1] + (1,), a.dtype)
    zeros = jnp.zeros(a.shape[:-1] + (DH - 1,), a.dtype)
    parts = []
    for n in range(H_KV):
        parts += [a[..., n * DH:(n + 1) * DH], ones, zeros]
    return jnp.concatenate(parts, axis=-1)


def kernel(x_prompt, x_sample, c, cache_diff_k, cache_diff_v, cache_gqa_k, cache_gqa_v, c_ctx, w_ada, b_ada, norm_ff1, w_ff1_gu, w_ff1_down, norm_mix, w_in, q_norm, k_norm, lambda_q1, lambda_k1, lambda_q2, lambda_k2, subln, w_out, norm_ff2, w_ff2_gu, w_ff2_down, final_norm):
    assert w_ada.shape[0] == 1, "single trunk layer"
    bc, tc_, _ = x_prompt.shape
    bs, ts, _ = x_sample.shape
    tpast = cache_diff_k.shape[2]
    assert bs + 1 <= MOD_ROWS
    ctx_row = bs

    cvecs = jnp.concatenate([c, c_ctx[None], jnp.zeros((MOD_ROWS - bs - 1, D_MODEL), F32)], axis=0)
    wgu1 = w_ff1_gu[0].astype(BF16)
    wd1 = w_ff1_down[0].astype(BF16)
    wi = w_in[0]
    n_qkvq = 3 * DQ_DIFF + DQ_GQA
    w_ext = jnp.concatenate(
        [wi[:, :n_qkvq], _dup_heads(wi[:, n_qkvq:n_qkvq + DKV_GQA]), _dup_heads(wi[:, n_qkvq + DKV_GQA:])],
        axis=1).astype(BF16)
    seg128 = (jnp.arange(LANES)[:, None] // DH == jnp.arange(LANES)[None, :] // DH).astype(BF16)
    seg = jnp.concatenate([seg128, seg128], axis=0)
    qkn = jnp.concatenate([jnp.tile(q_norm[0], H_GQA), jnp.tile(k_norm[0], 2 * H_KV)])[None]
    lamv = jnp.stack([lambda_q1[0], lambda_k1[0], lambda_q2[0], lambda_k2[0]])
    g1 = norm_ff1
    gm = norm_mix
    g2 = norm_ff2
    fg = final_norm[None]
    cos, sin = _rope_tables(ts)
    kcache = jnp.concatenate([cache_diff_k[:, 0].reshape(bs, tpast, DQ_DIFF),
                              _dup_heads(cache_gqa_k[:, 0].reshape(bs, tpast, DKV_GQA))], axis=-1).astype(BF16)
    vcache = jnp.concatenate([cache_diff_v[:, 0].reshape(bs, tpast, DQ_DIFF),
                              _with_sum_feature(cache_gqa_v[:, 0].reshape(bs, tpast, DKV_GQA))], axis=-1).astype(BF16)
    vcache_t = vcache.transpose(2, 0, 1).reshape(D_KV_ALL, bs * tpast)

    mod = _mod_call(cvecs, w_ada[0], b_ada).reshape(MOD_ROWS, N_MOD, D_MODEL)

    ctx_map = lambda i: (ctx_row, 0, 0)
    xc = x_prompt.reshape(bc * tc_, D_MODEL)
    xc, wgu2, wd2, wout = _ffn_call(xc, mod, ctx_map, g1, wgu1, wd1, sub=0,
                                    side_casts=(w_ff2_gu[0], w_ff2_down[0], w_out[0]), name="ffn1_ctx")
    qc, kc, vtc, dkt32, dv4, gkt32, gvt32 = _proj_call(
        xc, mod, ctx_map, gm, w_ext, seg, qkn, rope_tables=None, cache_seq=tc_, sum_row=False, name="proj_ctx")
    odc, ogc = _attn_ctx_call(qc.reshape(bc, tc_, D_MIX), kc.reshape(bc, tc_, D_KV_ALL), vtc,
                              lamv, subln, name="attn_ctx")
    yc = _ffn_call(xc, mod, ctx_map, g2, wgu2, wd2, sub=2,
                   pre=(odc.reshape(bc * tc_, DQ_DIFF), ogc.reshape(bc * tc_, DQ_GQA), wout),
                   final_gain=fg, name="ffn2_ctx")

    ffn_tiles = ts // FFN_TILE
    proj_tiles = ts // PROJ_TILE
    xs = x_sample.reshape(bs * ts, D_MODEL)
    xs = _ffn_call(xs, mod, lambda i: (i // ffn_tiles, 0, 0), g1, wgu1, wd1, sub=0, name="ffn1_smp")
    qs, ks, vts = _proj_call(xs, mod, lambda i: (i // proj_tiles, 0, 0), gm, w_ext, seg, qkn,
                             rope_tables=(cos, sin), cache_seq=None, sum_row=True, name="proj_smp")
    qs = qs.reshape(bs, ts, D_MIX)
    ks = ks.reshape(bs, ts, D_KV_ALL)
    ods = _attn_sample_call(qs, ks, vts, kcache, vcache_t, lamv, subln, is_diff=True, name="attn_diff_smp")
    ogs = _attn_sample_call(qs, ks, vts, kcache, vcache_t, lamv, subln, is_diff=False, name="attn_gqa_smp")
    ys = _ffn_call(xs, mod, lambda i: (i // ffn_tiles, 0, 0), g2, wgu2, wd2, sub=2,
                   pre=(ods.reshape(bs * ts, DQ_DIFF), ogs.reshape(bs * ts, DQ_GQA), wout),
                   final_gain=fg, name="ffn2_smp")

    return (yc.reshape(bc, tc_, D_MODEL),
            ys.reshape(bs, ts, D_MODEL),
            dkt32.reshape(bc, 1, H_DIFF, 2, DH, tc_).transpose(0, 1, 5, 2, 3, 4),
            dv4.reshape(bc, 1, tc_, H_DIFF, 2 * DH),
            gkt32.reshape(bc, 1, H_KV, DH, tc_).transpose(0, 1, 4, 2, 3),
            gvt32.reshape(bc, 1, H_KV, DH, tc_).transpose(0, 1, 4, 2, 3))
```

```python
import functools
import math

import jax
import jax.numpy as jnp
import numpy as np
from jax import lax
from jax.experimental import pallas as pl
from jax.experimental.pallas import tpu as pltpu

F32 = jnp.float32
BF16 = jnp.bfloat16

D_MODEL = 1024
N_MOD = 9
D_FF = 2816
H_DIFF = 4
DH = 64
H_GQA = 8
H_KV = 2
G_GQA = H_GQA // H_KV
GRID_W = 64
ROPE_THETA = 10000.0
EPS = 1e-6
LAMBDA_INIT = 0.8 - 0.6 * math.exp(-0.3 * 0)
DQ_DIFF = H_DIFF * 2 * DH
DQ_GQA = H_GQA * DH
DKV_GQA = H_KV * DH
ROPE_HALF = DH // 4
D_MIX = DQ_DIFF + DQ_GQA
D_KV_ALL = DQ_DIFF + 2 * DKV_GQA
D_NORMED = DQ_GQA + 2 * DKV_GQA
SCORE_SCALE = DH ** -0.5 * math.log2(math.e)

LANES = 128
SUBLANES = 8
BF16_ROWS = 16
VMEM_LIMIT_BYTES = 60000 * 1024

MOD_ROWS = 8
MOD_TILE_N = 4608
FFN_TILE = 1024
FFN_HIDDEN_SPLIT = 1536
PROJ_TILE = 1024
ATTN_TILE_Q = 256
ATTN_CHUNK_K = 512
ATTN_SKEW = 2
ATTN_COLS_PER_STEP = 2
ATTN_CTX_MAPS_PER_GROUP = 8


def _sigmoid(x):
    return 1.0 / (1.0 + jnp.exp(-x))


def _rms(x):
    return x * lax.rsqrt(jnp.mean(x * x, axis=-1, keepdims=True) + EPS)


def _dot(a, b):
    return jnp.dot(a, b, preferred_element_type=F32)


def _dot_nt(a, b):
    return lax.dot_general(a, b, (((1,), (1,)), ((), ())), preferred_element_type=F32)


def _resident(shape):
    return pl.BlockSpec(shape, lambda *_: (0,) * len(shape), pipeline_mode=pl.Buffered(1))


def _mod_kernel(c_ref, w_ref, b_ref, o_ref):
    c = c_ref[...]
    s = c * _sigmoid(c)
    o_ref[...] = _dot(s.astype(BF16), w_ref[...].astype(BF16)) + b_ref[...]


def _mod_call(cvecs, w_ada, b_ada):
    n = w_ada.shape[1]
    return pl.pallas_call(
        _mod_kernel,
        grid=(n // MOD_TILE_N,),
        in_specs=[
            pl.BlockSpec((MOD_ROWS, D_MODEL), lambda j: (0, 0)),
            pl.BlockSpec((D_MODEL, MOD_TILE_N), lambda j: (0, j)),
            pl.BlockSpec((1, MOD_TILE_N), lambda j: (0, j)),
        ],
        out_specs=pl.BlockSpec((MOD_ROWS, MOD_TILE_N), lambda j: (0, j)),
        out_shape=jax.ShapeDtypeStruct((MOD_ROWS, n), F32),
        name="mod",
    )(cvecs, w_ada, b_ada)


def _ffn_kernel(*refs, sub, pre, final, n_side):
    refs = list(refs)
    x_ref, mod_ref = refs[:2]
    pos = 2
    if pre:
        od_ref, og_ref, wout_ref = refs[pos:pos + 3]
        pos += 3
    g_ref, wgu_ref, wd_ref = refs[pos:pos + 3]
    pos += 3
    if final:
        fg_ref = refs[pos]
        pos += 1
    side_in = refs[pos:pos + n_side]
    o_ref = refs[pos + n_side]
    side_out = refs[pos + n_side + 1:]

    x = x_ref[...]
    mod = mod_ref[0]
    if pre:
        o = jnp.concatenate([od_ref[...], og_ref[...]], axis=1)
        x = x + mod[5:6] * _dot(o, wout_ref[...])
    shift = mod[3 * sub:3 * sub + 1]
    scale = mod[3 * sub + 1:3 * sub + 2]
    gate = mod[3 * sub + 2:3 * sub + 3]
    h = (_rms(x) * g_ref[...]) * (1.0 + scale) + shift
    hb = h.astype(BF16)
    y = None
    for j0, j1 in ((0, FFN_HIDDEN_SPLIT), (FFN_HIDDEN_SPLIT, D_FF)):
        g = _dot(hb, wgu_ref[:, j0:j1])
        u = _dot(hb, wgu_ref[:, D_FF + j0:D_FF + j1])
        act = (g * _sigmoid(g)) * u
        yj = _dot(act.astype(BF16), wd_ref[j0:j1, :])
        y = yj if y is None else y + yj
    x = x + (0.5 * gate) * y
    if final:
        x = _rms(x) * fg_ref[...]
    o_ref[...] = x
    for src, dst in zip(side_in, side_out):
        dst[...] = src[...].astype(BF16)


def _ffn_call(x, mod, row_map, gain, wgu, wd, *, sub, pre=None, final_gain=None, side_casts=(), name):
    t = x.shape[0]
    tm = FFN_TILE
    steps = t // tm
    row_spec = lambda w: pl.BlockSpec((tm, w), lambda i: (i, 0))
    in_specs = [row_spec(D_MODEL), pl.BlockSpec((1, N_MOD, D_MODEL), row_map)]
    args = [x, mod]
    if pre is not None:
        od, og, wout = pre
        in_specs += [row_spec(DQ_DIFF), row_spec(DQ_GQA), _resident((D_MIX, D_MODEL))]
        args += [od, og, wout]
    in_specs += [_resident((1, D_MODEL)), _resident((D_MODEL, 2 * D_FF)), _resident((D_FF, D_MODEL))]
    args += [gain, wgu, wd]
    if final_gain is not None:
        in_specs.append(_resident((1, D_MODEL)))
        args.append(final_gain)
    out_specs = [row_spec(D_MODEL)]
    out_shape = [jax.ShapeDtypeStruct((t, D_MODEL), F32)]
    for w in side_casts:
        rows, cols = w.shape
        assert rows % (steps * BF16_ROWS) == 0
        spec = pl.BlockSpec((rows // steps, cols), lambda i: (i, 0))
        in_specs.append(spec)
        args.append(w)
        out_specs.append(spec)
        out_shape.append(jax.ShapeDtypeStruct((rows, cols), BF16))
    outs = pl.pallas_call(
        functools.partial(_ffn_kernel, sub=sub, pre=pre is not None, final=final_gain is not None,
                          n_side=len(side_casts)),
        grid=(steps,),
        in_specs=in_specs,
        out_specs=out_specs,
        out_shape=out_shape,
        compiler_params=pltpu.CompilerParams(vmem_limit_bytes=VMEM_LIMIT_BYTES),
        name=name,
    )(*args)
    return outs if side_casts else outs[0]


def _rope(x, cos, sin, first_of_pair):
    w = x.shape[1]
    partner = jnp.where(first_of_pair, pltpu.roll(x, w - ROPE_HALF, 1), pltpu.roll(x, ROPE_HALF, 1))
    reps = w // LANES
    cos_w = jnp.concatenate([cos] * reps, axis=1) if reps > 1 else cos
    sin_w = jnp.concatenate([sin] * reps, axis=1) if reps > 1 else sin
    return x * cos_w + partner * sin_w


def _proj_kernel(*refs, rope, emit_f32, sum_row):
    refs = list(refs)
    x_ref, mod_ref, g_ref, w_ref, seg_ref, qkn_ref = refs[:6]
    pos = 6
    if rope:
        cos_ref, sin_ref = refs[pos:pos + 2]
        pos += 2
    q_ref, k_ref, vt_ref = refs[pos:pos + 3]
    pos += 3

    x = x_ref[...]
    mod = mod_ref[0]
    h = (_rms(x) * g_ref[...]) * (1.0 + mod[4:5]) + mod[3:4]
    qkv = _dot(h.astype(BF16), w_ref[...])
    dq = qkv[:, 0:DQ_DIFF]
    dk = qkv[:, DQ_DIFF:2 * DQ_DIFF]
    dv = qkv[:, 2 * DQ_DIFF:3 * DQ_DIFF]
    n0 = 3 * DQ_DIFF
    raw = qkv[:, n0:n0 + D_NORMED]
    gv = qkv[:, n0 + D_NORMED:]

    sq = raw * raw
    hi = sq.astype(BF16)
    lo = (sq - hi.astype(F32)).astype(BF16)
    seg = seg_ref[...]
    ss = jnp.concatenate(
        [_dot(jnp.concatenate([hi[:, c:c + LANES], lo[:, c:c + LANES]], axis=1), seg)
         for c in range(0, D_NORMED, LANES)], axis=1)
    normed = (raw * lax.rsqrt(ss * (1.0 / DH) + EPS)) * qkn_ref[...]
    gq = normed[:, :DQ_GQA]
    gk = normed[:, DQ_GQA:]

    if rope:
        cos = cos_ref[...]
        sin = sin_ref[...]
        lane = lax.broadcasted_iota(jnp.int32, (x.shape[0], LANES), 1)
        first = (lane % (2 * ROPE_HALF)) < ROPE_HALF
        first4 = jnp.concatenate([first] * 4, axis=1)
        first2 = jnp.concatenate([first] * 2, axis=1)
        dq = _rope(dq, cos, sin, first4)
        dk = _rope(dk, cos, sin, first4)
        gq = _rope(gq, cos, sin, first4)
        gk = _rope(gk, cos, sin, first2)

    if sum_row:
        lane = lax.broadcasted_iota(jnp.int32, (x.shape[0], LANES), 1)
        marker = jnp.where(lane == DH, 1.0, 0.0)
        gv = jnp.concatenate([jnp.where(lane < DH, gv[:, c:c + LANES], marker)
                              for c in range(0, gv.shape[1], LANES)], axis=1)

    q_ref[...] = jnp.concatenate([dq * SCORE_SCALE, gq * SCORE_SCALE], axis=1).astype(BF16)
    k_ref[...] = jnp.concatenate([dk, gk], axis=1).astype(BF16)
    vt_ref[...] = jnp.concatenate([dv, gv], axis=1).T.astype(BF16)

    if emit_f32:
        dkt_ref, dv4_ref, gkt_ref, gvt_ref = refs[pos:pos + 4]
        tm = x.shape[0]
        seq = dkt_ref.shape[2]
        lane = lax.broadcasted_iota(jnp.int32, (tm, LANES), 1)
        low = lane < DH
        dkt = dk.T
        gkt = jnp.where(low, gk[:, :LANES], gk[:, LANES:]).T
        gvt = jnp.where(low, gv[:, :LANES], gv[:, LANES:]).T
        for b in range(tm // seq):
            dkt_ref[b] = dkt[:, b * seq:(b + 1) * seq]
            gkt_ref[b] = gkt[:, b * seq:(b + 1) * seq]
            gvt_ref[b] = gvt[:, b * seq:(b + 1) * seq]
        for hd in range(H_DIFF):
            dv4_ref[pl.ds(hd, tm, stride=H_DIFF), :] = dv[:, hd * LANES:(hd + 1) * LANES]


def _proj_call(x, mod, row_map, gain, w_ext, seg, qkn, *, rope_tables, cache_seq, sum_row, name):
    t = x.shape[0]
    emit_f32 = cache_seq is not None
    tm = PROJ_TILE
    row_spec = lambda w: pl.BlockSpec((tm, w), lambda i: (i, 0))
    n_ext = w_ext.shape[1]
    in_specs = [row_spec(D_MODEL), pl.BlockSpec((1, N_MOD, D_MODEL), row_map),
                _resident((1, D_MODEL)), _resident((D_MODEL, n_ext)),
                _resident((2 * LANES, LANES)), _resident((1, D_NORMED))]
    args = [x, mod, gain, w_ext, seg, qkn]
    if rope_tables is not None:
        cos, sin = rope_tables
        tiles_per_seq = cos.shape[0] // tm
        tab_spec = pl.BlockSpec((tm, LANES), lambda i: (i % tiles_per_seq, 0))
        in_specs += [tab_spec, tab_spec]
        args += [cos, sin]
    out_specs = [row_spec(D_MIX), row_spec(D_KV_ALL), pl.BlockSpec((D_KV_ALL, tm), lambda i: (0, i))]
    out_shape = [jax.ShapeDtypeStruct((t, D_MIX), BF16),
                 jax.ShapeDtypeStruct((t, D_KV_ALL), BF16),
                 jax.ShapeDtypeStruct((D_KV_ALL, t), BF16)]
    if emit_f32:
        assert tm % cache_seq == 0
        nb = tm // cache_seq
        slab = lambda rows: pl.BlockSpec((nb, rows, cache_seq), lambda i: (i, 0, 0))
        out_specs += [slab(DQ_DIFF), pl.BlockSpec((tm * H_DIFF, LANES), lambda i: (i, 0)),
                      slab(DKV_GQA), slab(DKV_GQA)]
        out_shape += [jax.ShapeDtypeStruct((t // cache_seq, DQ_DIFF, cache_seq), F32),
                      jax.ShapeDtypeStruct((t * H_DIFF, LANES), F32),
                      jax.ShapeDtypeStruct((t // cache_seq, DKV_GQA, cache_seq), F32),
                      jax.ShapeDtypeStruct((t // cache_seq, DKV_GQA, cache_seq), F32)]
    return pl.pallas_call(
        functools.partial(_proj_kernel, rope=rope_tables is not None, emit_f32=emit_f32, sum_row=sum_row),
        grid=(t // tm,),
        in_specs=in_specs,
        out_specs=out_specs,
        out_shape=out_shape,
        compiler_params=pltpu.CompilerParams(vmem_limit_bytes=VMEM_LIMIT_BYTES),
        name=name,
    )(*args)


def _lambda(lamv_ref):
    lamv = lamv_ref[...]
    return (jnp.exp(jnp.sum(lamv[0:1] * lamv[1:2], axis=-1, keepdims=True))
            - jnp.exp(jnp.sum(lamv[2:3] * lamv[3:4], axis=-1, keepdims=True)) + LAMBDA_INIT)


def _fold_rows(x, op):
    return op(x.reshape(x.shape[0] // SUBLANES, SUBLANES, x.shape[1]), axis=0)


def _value_rows(kvc):
    return slice(kvc * LANES, (kvc + 1) * LANES)


def _combine_heads(is_diff, ot_a, ot_b, lam, subln):
    if is_diff:
        o = (ot_a - lam * ot_b).T
        return (_rms(o) * subln) * (1.0 - LAMBDA_INIT)
    return jnp.concatenate([ot_a[:DH], ot_b[:DH]], axis=0).T


def _attn_kernel(q_ref, k_ref, vt_ref, lamv_ref, subln_ref, od_ref, og_ref, *, items, maps_per_group):
    tq = q_ref.shape[1]
    low = lax.broadcasted_iota(jnp.int32, (tq, LANES), 1) < DH

    class SoftmaxMap:
        def __init__(self, qm, kvc):
            self.qm = qm
            self.cs = slice(kvc * LANES, (kvc + 1) * LANES)
            self.vrows = _value_rows(kvc)

        def score_pass(self):
            self.s = _dot_nt(k_ref[0, :, self.cs], self.qm)
            self.m = _fold_rows(self.s, jnp.max).max(axis=0, keepdims=True)

        def pv_pass(self):
            e = jnp.exp2(self.s - self.m)
            tot = _fold_rows(e, jnp.sum).sum(axis=0, keepdims=True)
            self.ot = _dot(vt_ref[self.vrows, :], e.astype(BF16)) * (1.0 / tot)

    maps = []
    for is_diff, qc, kvc, _ in items:
        q = q_ref[0, :, qc * LANES:(qc + 1) * LANES]
        zero = jnp.zeros_like(q)
        maps += [SoftmaxMap(jnp.where(low, q, zero), kvc), SoftmaxMap(jnp.where(low, zero, q), kvc)]

    groups = [maps[g:g + maps_per_group] for g in range(0, len(maps), maps_per_group)]
    for stage in range(len(groups) + 1):
        if stage > 0:
            for mp in groups[stage - 1]:
                mp.pv_pass()
        if stage < len(groups):
            for mp in groups[stage]:
                mp.score_pass()

    lam = _lambda(lamv_ref)
    for n, (is_diff, _, _, oc) in enumerate(items):
        o = _combine_heads(is_diff, maps[2 * n].ot, maps[2 * n + 1].ot, lam, subln_ref[...])
        out_ref = od_ref if is_diff else og_ref
        out_ref[0, :, oc * LANES:(oc + 1) * LANES] = o.astype(BF16)


def _attn_loop_kernel(q_ref, kn_ref, vtn_ref, kc_ref, vtc_ref, lamv_ref, subln_ref, out_ref, *scratch,
                      is_diff, kv_cols):
    tq = ATTN_TILE_Q
    n_tiles = q_ref.shape[1] // tq
    n_groups = len(kv_cols)
    s_refs = scratch[:n_groups]
    m_refs = scratch[n_groups:]
    t_new = kn_ref.shape[1]
    segs = [(kn_ref, vtn_ref, 0), (kc_ref, vtc_ref, t_new)]
    chunks = []
    for si, (k_ref, _, base) in enumerate(segs):
        n = k_ref.shape[1]
        chunks += [(si, c0, min(n, c0 + ATTN_CHUNK_K), base + c0) for c0 in range(0, n, ATTN_CHUNK_K)]
    n_chunks = len(chunks)
    per_tile = n_groups * n_chunks
    lead = n_chunks + ATTN_SKEW
    assert 0 < ATTN_SKEW < n_chunks and lead < per_tile

    low = lax.broadcasted_iota(jnp.int32, (tq, LANES), 1) < DH
    lam = _lambda(lamv_ref)

    def rows(tile):
        return pl.ds(pl.multiple_of(tile * tq, tq), tq)

    def masked_q(tile, g):
        q = q_ref[0, rows(tile), g * LANES:(g + 1) * LANES]
        zero = jnp.zeros_like(q)
        return jnp.concatenate([jnp.where(low, q, zero), jnp.where(low, zero, q)], axis=0)

    def score_unit(qm, g, ci, mpart):
        si, c0, c1, r0 = chunks[ci]
        cs = slice(kv_cols[g] * LANES, (kv_cols[g] + 1) * LANES)
        s = _dot_nt(segs[si][0][0, c0:c1, cs], qm)
        s_refs[g][r0:r0 + c1 - c0, :] = s
        part = _fold_rows(s, jnp.max)
        mpart = part if mpart is None else jnp.maximum(mpart, part)
        if ci == n_chunks - 1:
            m_refs[g][...] = mpart.max(axis=0, keepdims=True)
            return None
        return mpart

    def pv_unit(g, ci, m, acc, lpart):
        si, c0, c1, r0 = chunks[ci]
        e = jnp.exp2(s_refs[g][r0:r0 + c1 - c0, :] - m)
        pv = _dot(segs[si][1][_value_rows(kv_cols[g]), c0:c1], e.astype(BF16))
        acc = pv if acc is None else acc + pv
        if not is_diff:
            return acc, None
        part = _fold_rows(e, jnp.sum)
        return acc, (part if lpart is None else lpart + part)

    def emit(g, tile, acc, lpart):
        lsum = lpart.sum(axis=0, keepdims=True) if is_diff else acc[DH:DH + 1]
        ot = acc * (1.0 / lsum)
        o = _combine_heads(is_diff, ot[:, :tq], ot[:, tq:], lam, subln_ref[...])
        out_ref[0, rows(tile), g * LANES:(g + 1) * LANES] = o.astype(BF16)

    mpart = None
    qm = None
    for u in range(lead):
        g, ci = divmod(u, n_chunks)
        if ci == 0:
            qm = masked_q(0, g)
        mpart = score_unit(qm, g, ci, mpart)

    def body(t, carried_mpart, final_tile=False):
        a_state = {}
        a_state[(lead // per_tile, (lead // n_chunks) % n_groups)] = (None, carried_mpart)
        acc = lpart = m = None
        for j in range(per_tile):
            b_g, b_ci = divmod(j, n_chunks)
            if b_ci == 0:
                m = m_refs[b_g][...]
                acc = lpart = None
            acc, lpart = pv_unit(b_g, b_ci, m, acc, lpart)
            if b_ci == n_chunks - 1:
                emit(b_g, t, acc, lpart)

            a_off, a_rem = divmod(j + lead, per_tile)
            if final_tile and a_off > 0:
                continue
            a_g, a_ci = divmod(a_rem, n_chunks)
            a_qm, a_mpart = a_state.get((a_off, a_g), (None, None))
            if a_qm is None:
                a_qm = masked_q(t + a_off, a_g)
            a_state[(a_off, a_g)] = (a_qm, score_unit(a_qm, a_g, a_ci, a_mpart))

        last = per_tile - 1 + lead
        return a_state.get((last // per_tile, (last % per_tile) // n_chunks), (None, None))[1]

    mpart = lax.fori_loop(0, n_tiles - 1, body, mpart)
    body(jnp.int32(n_tiles - 1), mpart, final_tile=True)


def _attn_sample_call(q, k, vt, kc, vtc, lamv, subln, *, is_diff, name):
    b, t, _ = q.shape
    tc = kc.shape[1]
    qw = ATTN_COLS_PER_STEP * LANES
    nblk = DQ_DIFF // qw
    if is_diff:
        kw = qw
        q0 = 0
        kv0 = 0
        kv_cols = tuple(range(ATTN_COLS_PER_STEP))
    else:
        assert ATTN_COLS_PER_STEP * (LANES // DH) == G_GQA
        kw = LANES
        q0 = DQ_DIFF // qw
        kv0 = DQ_DIFF // kw
        kv_cols = (0,) * ATTN_COLS_PER_STEP
    in_specs = [
        pl.BlockSpec((1, t, qw), lambda bi, j: (bi, 0, q0 + j)),
        pl.BlockSpec((1, t, kw), lambda bi, j: (bi, 0, kv0 + j)),
        pl.BlockSpec((kw, t), lambda bi, j: (kv0 + j, bi)),
        pl.BlockSpec((1, tc, kw), lambda bi, j: (bi, 0, kv0 + j)),
        pl.BlockSpec((kw, tc), lambda bi, j: (kv0 + j, bi)),
        pl.BlockSpec((4, DH), lambda bi, j: (0, 0)),
        pl.BlockSpec((1, LANES), lambda bi, j: (0, 0)),
    ]
    return pl.pallas_call(
        functools.partial(_attn_loop_kernel, is_diff=is_diff, kv_cols=kv_cols),
        grid=(b, nblk),
        in_specs=in_specs,
        out_specs=pl.BlockSpec((1, t, qw), lambda bi, j: (bi, 0, j)),
        out_shape=jax.ShapeDtypeStruct((b, t, nblk * qw), BF16),
        scratch_shapes=([pltpu.VMEM((t + tc, 2 * ATTN_TILE_Q), F32)] * ATTN_COLS_PER_STEP
                        + [pltpu.VMEM((1, 2 * ATTN_TILE_Q), F32)] * ATTN_COLS_PER_STEP),
        compiler_params=pltpu.CompilerParams(vmem_limit_bytes=VMEM_LIMIT_BYTES),
        name=name,
    )(q, k, vt, kc, vtc, lamv, subln)


def _attn_ctx_call(q, k, vt, lamv, subln, *, name):
    b, t, _ = q.shape
    ncol = DQ_DIFF // LANES
    items = tuple((True, j, j, j) for j in range(ncol)) + tuple(
        (False, ncol + j, ncol + j // (LANES // DH), j) for j in range(ncol))
    whole = lambda w: pl.BlockSpec((1, t, w), lambda bi: (bi, 0, 0))
    return pl.pallas_call(
        functools.partial(_attn_kernel, items=items, maps_per_group=ATTN_CTX_MAPS_PER_GROUP),
        grid=(b,),
        in_specs=[whole(D_MIX), whole(D_KV_ALL),
                  pl.BlockSpec((D_KV_ALL, t), lambda bi: (0, bi)),
                  pl.BlockSpec((4, DH), lambda bi: (0, 0)),
                  pl.BlockSpec((1, LANES), lambda bi: (0, 0))],
        out_specs=[whole(DQ_DIFF), whole(DQ_GQA)],
        out_shape=[jax.ShapeDtypeStruct((b, t, DQ_DIFF), BF16),
                   jax.ShapeDtypeStruct((b, t, DQ_GQA), BF16)],
        compiler_params=pltpu.CompilerParams(vmem_limit_bytes=VMEM_LIMIT_BYTES),
        name=name,
    )(q, k, vt, lamv, subln)


def _rope_tables(n_tokens):
    t = np.arange(n_tokens)
    row = (t // GRID_W).astype(np.float32)
    col = (t % GRID_W).astype(np.float32)
    half = DH // 2
    inv = np.float32(ROPE_THETA) ** (-(np.arange(0, half, 2, dtype=np.float32) / np.float32(half)))
    ang_r = row[:, None] * inv
    ang_c = col[:, None] * inv
    cos = np.concatenate([np.cos(ang_r)] * 2 + [np.cos(ang_c)] * 2, axis=1)
    sin = np.concatenate([-np.sin(ang_r), np.sin(ang_r), -np.sin(ang_c), np.sin(ang_c)], axis=1)
    reps = LANES // DH
    return (jnp.asarray(np.concatenate([cos] * reps, axis=1), F32),
            jnp.asarray(np.concatenate([sin] * reps, axis=1), F32))


def _dup_heads(a):
    parts = []
    for n in range(H_KV):
        head = a[..., n * DH:(n + 1) * DH]
        parts += [head, head]
    return jnp.concatenate(parts, axis=-1)


def _with_sum_feature(a):
    ones = jnp.ones(a.shape[:-1] + (1,), a.dtype)
    zeros = jnp.zeros(a.shape[:-1] + (DH - 1,), a.dtype)
    parts = []
    for n in range(H_KV):
        parts += [a[..., n * DH:(n + 1) * DH], ones, zeros]
    return jnp.concatenate(parts, axis=-1)


def kernel(x_prompt, x_sample, c, cache_diff_k, cache_diff_v, cache_gqa_k, cache_gqa_v, c_ctx, w_ada, b_ada, norm_ff1, w_ff1_gu, w_ff1_down, norm_mix, w_in, q_norm, k_norm, lambda_q1, lambda_k1, lambda_q2, lambda_k2, subln, w_out, norm_ff2, w_ff2_gu, w_ff2_down, final_norm):
    assert w_ada.shape[0] == 1, "single trunk layer"
    bc, tc_, _ = x_prompt.shape
    bs, ts, _ = x_sample.shape
    tpast = cache_diff_k.shape[2]
    assert bs + 1 <= MOD_ROWS
    ctx_row = bs

    cvecs = jnp.concatenate([c, c_ctx[None], jnp.zeros((MOD_ROWS - bs - 1, D_MODEL), F32)], axis=0)
    wgu1 = w_ff1_gu[0].astype(BF16)
    wd1 = w_ff1_down[0].astype(BF16)
    wi = w_in[0]
    n_qkvq = 3 * DQ_DIFF + DQ_GQA
    w_ext = jnp.concatenate(
        [wi[:, :n_qkvq], _dup_heads(wi[:, n_qkvq:n_qkvq + DKV_GQA]), _dup_heads(wi[:, n_qkvq + DKV_GQA:])],
        axis=1).astype(BF16)
    seg128 = (jnp.arange(LANES)[:, None] // DH == jnp.arange(LANES)[None, :] // DH).astype(BF16)
    seg = jnp.concatenate([seg128, seg128], axis=0)
    qkn = jnp.concatenate([jnp.tile(q_norm[0], H_GQA), jnp.tile(k_norm[0], 2 * H_KV)])[None]
    lamv = jnp.stack([lambda_q1[0], lambda_k1[0], lambda_q2[0], lambda_k2[0]])
    g1 = norm_ff1
    gm = norm_mix
    g2 = norm_ff2
    fg = final_norm[None]
    cos, sin = _rope_tables(ts)
    kcache = jnp.concatenate([cache_diff_k[:, 0].reshape(bs, tpast, DQ_DIFF),
                              _dup_heads(cache_gqa_k[:, 0].reshape(bs, tpast, DKV_GQA))], axis=-1).astype(BF16)
    vcache = jnp.concatenate([cache_diff_v[:, 0].reshape(bs, tpast, DQ_DIFF),
                              _with_sum_feature(cache_gqa_v[:, 0].reshape(bs, tpast, DKV_GQA))], axis=-1).astype(BF16)
    vcache_t = vcache.transpose(2, 0, 1).reshape(D_KV_ALL, bs * tpast)

    mod = _mod_call(cvecs, w_ada[0], b_ada).reshape(MOD_ROWS, N_MOD, D_MODEL)

    ctx_map = lambda i: (ctx_row, 0, 0)
    xc = x_prompt.reshape(bc * tc_, D_MODEL)
    xc, wgu2, wd2, wout = _ffn_call(xc, mod, ctx_map, g1, wgu1, wd1, sub=0,
                                    side_casts=(w_ff2_gu[0], w_ff2_down[0], w_out[0]), name="ffn1_ctx")
    qc, kc, vtc, dkt32, dv4, gkt32, gvt32 = _proj_call(
        xc, mod, ctx_map, gm, w_ext, seg, qkn, rope_tables=None, cache_seq=tc_, sum_row=False, name="proj_ctx")
    odc, ogc = _attn_ctx_call(qc.reshape(bc, tc_, D_MIX), kc.reshape(bc, tc_, D_KV_ALL), vtc,
                              lamv, subln, name="attn_ctx")
    yc = _ffn_call(xc, mod, ctx_map, g2, wgu2, wd2, sub=2,
                   pre=(odc.reshape(bc * tc_, DQ_DIFF), ogc.reshape(bc * tc_, DQ_GQA), wout),
                   final_gain=fg, name="ffn2_ctx")

    ffn_tiles = ts // FFN_TILE
    proj_tiles = ts // PROJ_TILE
    xs = x_sample.reshape(bs * ts, D_MODEL)
    xs = _ffn_call(xs, mod, lambda i: (i // ffn_tiles, 0, 0), g1, wgu1, wd1, sub=0, name="ffn1_smp")
    qs, ks, vts = _proj_call(xs, mod, lambda i: (i // proj_tiles, 0, 0), gm, w_ext, seg, qkn,
                             rope_tables=(cos, sin), cache_seq=None, sum_row=True, name="proj_smp")
    qs = qs.reshape(bs, ts, D_MIX)
    ks = ks.reshape(bs, ts, D_KV_ALL)
    ods = _attn_sample_call(qs, ks, vts, kcache, vcache_t, lamv, subln, is_diff=True, name="attn_diff_smp")
    ogs = _attn_sample_call(qs, ks, vts, kcache, vcache_t, lamv, subln, is_diff=False, name="attn_gqa_smp")
    ys = _ffn_call(xs, mod, lambda i: (i // ffn_tiles, 0, 0), g2, wgu2, wd2, sub=2,
                   pre=(ods.reshape(bs * ts, DQ_DIFF), ogs.reshape(bs * ts, DQ_GQA), wout),
                   final_gain=fg, name="ffn2_smp")

    return (yc.reshape(bc, tc_, D_MODEL),
            ys.reshape(bs, ts, D_MODEL),
            dkt32.reshape(bc, 1, H_DIFF, 2, DH, tc_).transpose(0, 1, 5, 2, 3, 4),
            dv4.reshape(bc, 1, tc_, H_DIFF, 2 * DH),
            gkt32.reshape(bc, 1, H_KV, DH, tc_).transpose(0, 1, 4, 2, 3),
            gvt32.reshape(bc, 1, H_KV, DH, tc_).transpose(0, 1, 4, 2, 3))
```

```python
import functools
import math

import jax
import jax.numpy as jnp
import numpy as np
from jax import lax
from jax.experimental import pallas as pl
from jax.experimental.pallas import tpu as pltpu

F32 = jnp.float32
BF16 = jnp.bfloat16

D_MODEL = 1024
N_MOD = 9
D_FF = 2816
H_DIFF = 4
DH = 64
H_GQA = 8
H_KV = 2
G_GQA = H_GQA // H_KV
GRID_W = 64
ROPE_THETA = 10000.0
EPS = 1e-6
LAMBDA_INIT = 0.8 - 0.6 * math.exp(-0.3 * 0)
DQ_DIFF = H_DIFF * 2 * DH
DQ_GQA = H_GQA * DH
DKV_GQA = H_KV * DH
ROPE_HALF = DH // 4
D_MIX = DQ_DIFF + DQ_GQA
D_KV_ALL = DQ_DIFF + 2 * DKV_GQA
D_NORMED = DQ_GQA + 2 * DKV_GQA
SCORE_SCALE = DH ** -0.5 * math.log2(math.e)

LANES = 128
SUBLANES = 8
BF16_ROWS = 16
VMEM_LIMIT_BYTES = 60000 * 1024

MOD_ROWS = 8
MOD_TILE_N = 4608
FFN_TILE = 1024
FFN_HIDDEN_SPLIT = 1536
PROJ_TILE = 1024
ATTN_TILE_Q = 256
ATTN_CHUNK_K = 512
ATTN_SKEW = 2
ATTN_SUPERS_PER_STEP = 2
ATTN_COLS_PER_STEP = 2
ATTN_CTX_MAPS_PER_GROUP = 8


def _sigmoid(x):
    return 1.0 / (1.0 + jnp.exp(-x))


def _rms(x):
    return x * lax.rsqrt(jnp.mean(x * x, axis=-1, keepdims=True) + EPS)


def _dot(a, b):
    return jnp.dot(a, b, preferred_element_type=F32)


def _dot_nt(a, b):
    return lax.dot_general(a, b, (((1,), (1,)), ((), ())), preferred_element_type=F32)


def _resident(shape):
    return pl.BlockSpec(shape, lambda *_: (0,) * len(shape), pipeline_mode=pl.Buffered(1))


def _mod_kernel(c_ref, w_ref, b_ref, o_ref):
    c = c_ref[...]
    s = c * _sigmoid(c)
    o_ref[...] = _dot(s.astype(BF16), w_ref[...].astype(BF16)) + b_ref[...]


def _mod_call(cvecs, w_ada, b_ada):
    n = w_ada.shape[1]
    return pl.pallas_call(
        _mod_kernel,
        grid=(n // MOD_TILE_N,),
        in_specs=[
            pl.BlockSpec((MOD_ROWS, D_MODEL), lambda j: (0, 0)),
            pl.BlockSpec((D_MODEL, MOD_TILE_N), lambda j: (0, j)),
            pl.BlockSpec((1, MOD_TILE_N), lambda j: (0, j)),
        ],
        out_specs=pl.BlockSpec((MOD_ROWS, MOD_TILE_N), lambda j: (0, j)),
        out_shape=jax.ShapeDtypeStruct((MOD_ROWS, n), F32),
        name="mod",
    )(cvecs, w_ada, b_ada)


def _ffn_kernel(*refs, sub, pre, final, n_side):
    refs = list(refs)
    x_ref, mod_ref = refs[:2]
    pos = 2
    if pre:
        od_ref, og_ref, wout_ref = refs[pos:pos + 3]
        pos += 3
    g_ref, wgu_ref, wd_ref = refs[pos:pos + 3]
    pos += 3
    if final:
        fg_ref = refs[pos]
        pos += 1
    side_in = refs[pos:pos + n_side]
    o_ref = refs[pos + n_side]
    side_out = refs[pos + n_side + 1:]

    x = x_ref[...]
    mod = mod_ref[0]
    if pre:
        o = jnp.concatenate([od_ref[...], og_ref[...]], axis=1)
        x = x + mod[5:6] * _dot(o, wout_ref[...])
    shift = mod[3 * sub:3 * sub + 1]
    scale = mod[3 * sub + 1:3 * sub + 2]
    gate = mod[3 * sub + 2:3 * sub + 3]
    h = (_rms(x) * g_ref[...]) * (1.0 + scale) + shift
    hb = h.astype(BF16)
    y = None
    for j0, j1 in ((0, FFN_HIDDEN_SPLIT), (FFN_HIDDEN_SPLIT, D_FF)):
        g = _dot(hb, wgu_ref[:, j0:j1])
        u = _dot(hb, wgu_ref[:, D_FF + j0:D_FF + j1])
        act = (g * _sigmoid(g)) * u
        yj = _dot(act.astype(BF16), wd_ref[j0:j1, :])
        y = yj if y is None else y + yj
    x = x + (0.5 * gate) * y
    if final:
        x = _rms(x) * fg_ref[...]
    o_ref[...] = x
    for src, dst in zip(side_in, side_out):
        dst[...] = src[...].astype(BF16)


def _ffn_call(x, mod, row_map, gain, wgu, wd, *, sub, pre=None, final_gain=None, side_casts=(), name):
    t = x.shape[0]
    tm = FFN_TILE
    steps = t // tm
    row_spec = lambda w: pl.BlockSpec((tm, w), lambda i: (i, 0))
    in_specs = [row_spec(D_MODEL), pl.BlockSpec((1, N_MOD, D_MODEL), row_map)]
    args = [x, mod]
    if pre is not None:
        od, og, wout = pre
        in_specs += [row_spec(DQ_DIFF), row_spec(DQ_GQA), _resident((D_MIX, D_MODEL))]
        args += [od, og, wout]
    in_specs += [_resident((1, D_MODEL)), _resident((D_MODEL, 2 * D_FF)), _resident((D_FF, D_MODEL))]
    args += [gain, wgu, wd]
    if final_gain is not None:
        in_specs.append(_resident((1, D_MODEL)))
        args.append(final_gain)
    out_specs = [row_spec(D_MODEL)]
    out_shape = [jax.ShapeDtypeStruct((t, D_MODEL), F32)]
    for w in side_casts:
        rows, cols = w.shape
        assert rows % (steps * BF16_ROWS) == 0
        spec = pl.BlockSpec((rows // steps, cols), lambda i: (i, 0))
        in_specs.append(spec)
        args.append(w)
        out_specs.append(spec)
        out_shape.append(jax.ShapeDtypeStruct((rows, cols), BF16))
    outs = pl.pallas_call(
        functools.partial(_ffn_kernel, sub=sub, pre=pre is not None, final=final_gain is not None,
                          n_side=len(side_casts)),
        grid=(steps,),
        in_specs=in_specs,
        out_specs=out_specs,
        out_shape=out_shape,
        compiler_params=pltpu.CompilerParams(vmem_limit_bytes=VMEM_LIMIT_BYTES),
        name=name,
    )(*args)
    return outs if side_casts else outs[0]


def _rope(x, cos, sin, first_of_pair):
    w = x.shape[1]
    partner = jnp.where(first_of_pair, pltpu.roll(x, w - ROPE_HALF, 1), pltpu.roll(x, ROPE_HALF, 1))
    reps = w // LANES
    cos_w = jnp.concatenate([cos] * reps, axis=1) if reps > 1 else cos
    sin_w = jnp.concatenate([sin] * reps, axis=1) if reps > 1 else sin
    return x * cos_w + partner * sin_w


def _proj_kernel(*refs, rope, emit_f32, sum_row):
    refs = list(refs)
    x_ref, mod_ref, g_ref, w_ref, seg_ref, qkn_ref = refs[:6]
    pos = 6
    if rope:
        cos_ref, sin_ref = refs[pos:pos + 2]
        pos += 2
    q_ref, k_ref, vt_ref = refs[pos:pos + 3]
    pos += 3

    x = x_ref[...]
    mod = mod_ref[0]
    h = (_rms(x) * g_ref[...]) * (1.0 + mod[4:5]) + mod[3:4]
    qkv = _dot(h.astype(BF16), w_ref[...])
    dq = qkv[:, 0:DQ_DIFF]
    dk = qkv[:, DQ_DIFF:2 * DQ_DIFF]
    dv = qkv[:, 2 * DQ_DIFF:3 * DQ_DIFF]
    n0 = 3 * DQ_DIFF
    raw = qkv[:, n0:n0 + D_NORMED]
    gv = qkv[:, n0 + D_NORMED:]

    sq = raw * raw
    hi = sq.astype(BF16)
    lo = (sq - hi.astype(F32)).astype(BF16)
    seg = seg_ref[...]
    ss = jnp.concatenate(
        [_dot(jnp.concatenate([hi[:, c:c + LANES], lo[:, c:c + LANES]], axis=1), seg)
         for c in range(0, D_NORMED, LANES)], axis=1)
    normed = (raw * lax.rsqrt(ss * (1.0 / DH) + EPS)) * qkn_ref[...]
    gq = normed[:, :DQ_GQA]
    gk = normed[:, DQ_GQA:]

    if rope:
        cos = cos_ref[...]
        sin = sin_ref[...]
        lane = lax.broadcasted_iota(jnp.int32, (x.shape[0], LANES), 1)
        first = (lane % (2 * ROPE_HALF)) < ROPE_HALF
        first4 = jnp.concatenate([first] * 4, axis=1)
        first2 = jnp.concatenate([first] * 2, axis=1)
        dq = _rope(dq, cos, sin, first4)
        dk = _rope(dk, cos, sin, first4)
        gq = _rope(gq, cos, sin, first4)
        gk = _rope(gk, cos, sin, first2)

    if sum_row:
        lane = lax.broadcasted_iota(jnp.int32, (x.shape[0], LANES), 1)
        marker = jnp.where(lane == DH, 1.0, 0.0)
        gv = jnp.concatenate([jnp.where(lane < DH, gv[:, c:c + LANES], marker)
                              for c in range(0, gv.shape[1], LANES)], axis=1)

    q_ref[...] = jnp.concatenate([dq * SCORE_SCALE, gq * SCORE_SCALE], axis=1).astype(BF16)
    k_ref[...] = jnp.concatenate([dk, gk], axis=1).astype(BF16)
    vt_ref[...] = jnp.concatenate([dv, gv], axis=1).T.astype(BF16)

    if emit_f32:
        dkt_ref, dv4_ref, gkt_ref, gvt_ref = refs[pos:pos + 4]
        tm = x.shape[0]
        seq = dkt_ref.shape[2]
        lane = lax.broadcasted_iota(jnp.int32, (tm, LANES), 1)
        low = lane < DH
        dkt = dk.T
        gkt = jnp.where(low, gk[:, :LANES], gk[:, LANES:]).T
        gvt = jnp.where(low, gv[:, :LANES], gv[:, LANES:]).T
        for b in range(tm // seq):
            dkt_ref[b] = dkt[:, b * seq:(b + 1) * seq]
            gkt_ref[b] = gkt[:, b * seq:(b + 1) * seq]
            gvt_ref[b] = gvt[:, b * seq:(b + 1) * seq]
        for hd in range(H_DIFF):
            dv4_ref[pl.ds(hd, tm, stride=H_DIFF), :] = dv[:, hd * LANES:(hd + 1) * LANES]


def _proj_call(x, mod, row_map, gain, w_ext, seg, qkn, *, rope_tables, cache_seq, sum_row, name):
    t = x.shape[0]
    emit_f32 = cache_seq is not None
    tm = PROJ_TILE
    row_spec = lambda w: pl.BlockSpec((tm, w), lambda i: (i, 0))
    n_ext = w_ext.shape[1]
    in_specs = [row_spec(D_MODEL), pl.BlockSpec((1, N_MOD, D_MODEL), row_map),
                _resident((1, D_MODEL)), _resident((D_MODEL, n_ext)),
                _resident((2 * LANES, LANES)), _resident((1, D_NORMED))]
    args = [x, mod, gain, w_ext, seg, qkn]
    if rope_tables is not None:
        cos, sin = rope_tables
        tiles_per_seq = cos.shape[0] // tm
        tab_spec = pl.BlockSpec((tm, LANES), lambda i: (i % tiles_per_seq, 0))
        in_specs += [tab_spec, tab_spec]
        args += [cos, sin]
    out_specs = [row_spec(D_MIX), row_spec(D_KV_ALL), pl.BlockSpec((D_KV_ALL, tm), lambda i: (0, i))]
    out_shape = [jax.ShapeDtypeStruct((t, D_MIX), BF16),
                 jax.ShapeDtypeStruct((t, D_KV_ALL), BF16),
                 jax.ShapeDtypeStruct((D_KV_ALL, t), BF16)]
    if emit_f32:
        assert tm % cache_seq == 0
        nb = tm // cache_seq
        slab = lambda rows: pl.BlockSpec((nb, rows, cache_seq), lambda i: (i, 0, 0))
        out_specs += [slab(DQ_DIFF), pl.BlockSpec((tm * H_DIFF, LANES), lambda i: (i, 0)),
                      slab(DKV_GQA), slab(DKV_GQA)]
        out_shape += [jax.ShapeDtypeStruct((t // cache_seq, DQ_DIFF, cache_seq), F32),
                      jax.ShapeDtypeStruct((t * H_DIFF, LANES), F32),
                      jax.ShapeDtypeStruct((t // cache_seq, DKV_GQA, cache_seq), F32),
                      jax.ShapeDtypeStruct((t // cache_seq, DKV_GQA, cache_seq), F32)]
    return pl.pallas_call(
        functools.partial(_proj_kernel, rope=rope_tables is not None, emit_f32=emit_f32, sum_row=sum_row),
        grid=(t // tm,),
        in_specs=in_specs,
        out_specs=out_specs,
        out_shape=out_shape,
        compiler_params=pltpu.CompilerParams(vmem_limit_bytes=VMEM_LIMIT_BYTES),
        name=name,
    )(*args)


def _lambda(lamv_ref):
    lamv = lamv_ref[...]
    return (jnp.exp(jnp.sum(lamv[0:1] * lamv[1:2], axis=-1, keepdims=True))
            - jnp.exp(jnp.sum(lamv[2:3] * lamv[3:4], axis=-1, keepdims=True)) + LAMBDA_INIT)


def _fold_rows(x, op):
    return op(x.reshape(x.shape[0] // SUBLANES, SUBLANES, x.shape[1]), axis=0)


def _value_rows(kvc):
    return slice(kvc * LANES, (kvc + 1) * LANES)


def _combine_heads(is_diff, ot_a, ot_b, lam, subln):
    if is_diff:
        o = (ot_a - lam * ot_b).T
        return (_rms(o) * subln) * (1.0 - LAMBDA_INIT)
    return jnp.concatenate([ot_a[:DH], ot_b[:DH]], axis=0).T


def _attn_kernel(q_ref, k_ref, vt_ref, lamv_ref, subln_ref, od_ref, og_ref, *, items, maps_per_group):
    tq = q_ref.shape[1]
    low = lax.broadcasted_iota(jnp.int32, (tq, LANES), 1) < DH

    class SoftmaxMap:
        def __init__(self, qm, kvc):
            self.qm = qm
            self.cs = slice(kvc * LANES, (kvc + 1) * LANES)
            self.vrows = _value_rows(kvc)

        def score_pass(self):
            self.s = _dot_nt(k_ref[0, :, self.cs], self.qm)
            self.m = _fold_rows(self.s, jnp.max).max(axis=0, keepdims=True)

        def pv_pass(self):
            e = jnp.exp2(self.s - self.m)
            tot = _fold_rows(e, jnp.sum).sum(axis=0, keepdims=True)
            self.ot = _dot(vt_ref[self.vrows, :], e.astype(BF16)) * (1.0 / tot)

    maps = []
    for is_diff, qc, kvc, _ in items:
        q = q_ref[0, :, qc * LANES:(qc + 1) * LANES]
        zero = jnp.zeros_like(q)
        maps += [SoftmaxMap(jnp.where(low, q, zero), kvc), SoftmaxMap(jnp.where(low, zero, q), kvc)]

    groups = [maps[g:g + maps_per_group] for g in range(0, len(maps), maps_per_group)]
    for stage in range(len(groups) + 1):
        if stage > 0:
            for mp in groups[stage - 1]:
                mp.pv_pass()
        if stage < len(groups):
            for mp in groups[stage]:
                mp.score_pass()

    lam = _lambda(lamv_ref)
    for n, (is_diff, _, _, oc) in enumerate(items):
        o = _combine_heads(is_diff, maps[2 * n].ot, maps[2 * n + 1].ot, lam, subln_ref[...])
        out_ref = od_ref if is_diff else og_ref
        out_ref[0, :, oc * LANES:(oc + 1) * LANES] = o.astype(BF16)


def _attn_loop_kernel(q_ref, kn_ref, vtn_ref, kc_ref, vtc_ref, lamv_ref, subln_ref, out_ref, *scratch,
                      is_diff, kv_cols):
    tq = ATTN_TILE_Q
    n_tiles = q_ref.shape[1] // tq
    n_super = len(kv_cols)
    n_groups = len(kv_cols[0])
    s_refs = scratch[:n_groups]
    m_refs = scratch[n_groups:]
    t_new = kn_ref.shape[1]
    segs = [(kn_ref, vtn_ref, 0), (kc_ref, vtc_ref, t_new)]
    chunks = []
    for si, (k_ref, _, base) in enumerate(segs):
        n = k_ref.shape[1]
        chunks += [(si, c0, min(n, c0 + ATTN_CHUNK_K), base + c0) for c0 in range(0, n, ATTN_CHUNK_K)]
    n_chunks = len(chunks)
    per_tile = n_groups * n_chunks
    lead = n_chunks + ATTN_SKEW
    assert 0 < ATTN_SKEW < n_chunks and lead < per_tile

    low = lax.broadcasted_iota(jnp.int32, (tq, LANES), 1) < DH
    lam = _lambda(lamv_ref)

    def rows(tile):
        return pl.ds(pl.multiple_of(tile * tq, tq), tq)

    def lanes(block):
        return slice(block * LANES, (block + 1) * LANES)

    def masked_q(sb, tile, g):
        q = q_ref[0, rows(tile), lanes(sb * n_groups + g)]
        zero = jnp.zeros_like(q)
        return jnp.concatenate([jnp.where(low, q, zero), jnp.where(low, zero, q)], axis=0)

    def score_unit(qm, sb, g, ci, mpart):
        si, c0, c1, r0 = chunks[ci]
        s = _dot_nt(segs[si][0][0, c0:c1, lanes(kv_cols[sb][g])], qm)
        s_refs[g][r0:r0 + c1 - c0, :] = s
        part = _fold_rows(s, jnp.max)
        mpart = part if mpart is None else jnp.maximum(mpart, part)
        if ci == n_chunks - 1:
            m_refs[g][...] = mpart.max(axis=0, keepdims=True)
            return None
        return mpart

    def pv_unit(sb, g, ci, m, acc, lpart):
        si, c0, c1, r0 = chunks[ci]
        e = jnp.exp2(s_refs[g][r0:r0 + c1 - c0, :] - m)
        pv = _dot(segs[si][1][_value_rows(kv_cols[sb][g]), c0:c1], e.astype(BF16))
        acc = pv if acc is None else acc + pv
        if not is_diff:
            return acc, None
        part = _fold_rows(e, jnp.sum)
        return acc, (part if lpart is None else lpart + part)

    def emit(sb, g, tile, acc, lpart):
        lsum = lpart.sum(axis=0, keepdims=True) if is_diff else acc[DH:DH + 1]
        ot = acc * (1.0 / lsum)
        o = _combine_heads(is_diff, ot[:, :tq], ot[:, tq:], lam, subln_ref[...])
        out_ref[0, rows(tile), lanes(sb * n_groups + g)] = o.astype(BF16)

    mpart = None
    qm = None
    for u in range(lead):
        g, ci = divmod(u, n_chunks)
        if ci == 0:
            qm = masked_q(0, jnp.int32(0), g)
        mpart = score_unit(qm, 0, g, ci, mpart)

    def body(t, carried_mpart, sb, after):
        a_state = {}
        a_state[(lead // per_tile, (lead // n_chunks) % n_groups)] = (None, carried_mpart)
        acc = lpart = m = None
        for j in range(per_tile):
            b_g, b_ci = divmod(j, n_chunks)
            if b_ci == 0:
                m = m_refs[b_g][...]
                acc = lpart = None
            acc, lpart = pv_unit(sb, b_g, b_ci, m, acc, lpart)
            if b_ci == n_chunks - 1:
                emit(sb, b_g, t, acc, lpart)

            a_off, a_rem = divmod(j + lead, per_tile)
            if a_off > 0 and after is None:
                continue
            a_sb, a_tile = (sb, t + a_off) if (a_off == 0 or after == "tile") else (sb + 1, jnp.int32(0))
            a_g, a_ci = divmod(a_rem, n_chunks)
            a_qm, a_mpart = a_state.get((a_off, a_g), (None, None))
            if a_qm is None:
                a_qm = masked_q(a_sb, a_tile, a_g)
            a_state[(a_off, a_g)] = (a_qm, score_unit(a_qm, a_sb, a_g, a_ci, a_mpart))

        last = per_tile - 1 + lead
        return a_state.get((last // per_tile, (last % per_tile) // n_chunks), (None, None))[1]

    for sb in range(n_super):
        mpart = lax.fori_loop(0, n_tiles - 1, functools.partial(body, sb=sb, after="tile"), mpart)
        mpart = body(jnp.int32(n_tiles - 1), mpart, sb, "set" if sb + 1 < n_super else None)


def _attn_sample_call(q, k, vt, kc, vtc, lamv, subln, *, is_diff, name):
    b, t, _ = q.shape
    tc = kc.shape[1]
    n_super = ATTN_SUPERS_PER_STEP
    qw = n_super * ATTN_COLS_PER_STEP * LANES
    nblk = DQ_DIFF // qw
    if is_diff:
        kw = qw
        q0 = 0
        kv0 = 0
        kv_cols = tuple(tuple(sb * ATTN_COLS_PER_STEP + c for c in range(ATTN_COLS_PER_STEP))
                        for sb in range(n_super))
    else:
        assert ATTN_COLS_PER_STEP * (LANES // DH) == G_GQA
        kw = n_super * LANES
        q0 = DQ_DIFF // qw
        kv0 = DQ_DIFF // kw
        kv_cols = tuple((sb,) * ATTN_COLS_PER_STEP for sb in range(n_super))
    in_specs = [
        pl.BlockSpec((1, t, qw), lambda bi, j: (bi, 0, q0 + j)),
        pl.BlockSpec((1, t, kw), lambda bi, j: (bi, 0, kv0 + j)),
        pl.BlockSpec((kw, t), lambda bi, j: (kv0 + j, bi)),
        pl.BlockSpec((1, tc, kw), lambda bi, j: (bi, 0, kv0 + j)),
        pl.BlockSpec((kw, tc), lambda bi, j: (kv0 + j, bi)),
        pl.BlockSpec((4, DH), lambda bi, j: (0, 0)),
        pl.BlockSpec((1, LANES), lambda bi, j: (0, 0)),
    ]
    return pl.pallas_call(
        functools.partial(_attn_loop_kernel, is_diff=is_diff, kv_cols=kv_cols),
        grid=(b, nblk),
        in_specs=in_specs,
        out_specs=pl.BlockSpec((1, t, qw), lambda bi, j: (bi, 0, j)),
        out_shape=jax.ShapeDtypeStruct((b, t, nblk * qw), BF16),
        scratch_shapes=([pltpu.VMEM((t + tc, 2 * ATTN_TILE_Q), F32)] * ATTN_COLS_PER_STEP
                        + [pltpu.VMEM((1, 2 * ATTN_TILE_Q), F32)] * ATTN_COLS_PER_STEP),
        compiler_params=pltpu.CompilerParams(vmem_limit_bytes=VMEM_LIMIT_BYTES),
        name=name,
    )(q, k, vt, kc, vtc, lamv, subln)


def _attn_ctx_call(q, k, vt, lamv, subln, *, name):
    b, t, _ = q.shape
    ncol = DQ_DIFF // LANES
    items = tuple((True, j, j, j) for j in range(ncol)) + tuple(
        (False, ncol + j, ncol + j // (LANES // DH), j) for j in range(ncol))
    whole = lambda w: pl.BlockSpec((1, t, w), lambda bi: (bi, 0, 0))
    return pl.pallas_call(
        functools.partial(_attn_kernel, items=items, maps_per_group=ATTN_CTX_MAPS_PER_GROUP),
        grid=(b,),
        in_specs=[whole(D_MIX), whole(D_KV_ALL),
                  pl.BlockSpec((D_KV_ALL, t), lambda bi: (0, bi)),
                  pl.BlockSpec((4, DH), lambda bi: (0, 0)),
                  pl.BlockSpec((1, LANES), lambda bi: (0, 0))],
        out_specs=[whole(DQ_DIFF), whole(DQ_GQA)],
        out_shape=[jax.ShapeDtypeStruct((b, t, DQ_DIFF), BF16),
                   jax.ShapeDtypeStruct((b, t, DQ_GQA), BF16)],
        compiler_params=pltpu.CompilerParams(vmem_limit_bytes=VMEM_LIMIT_BYTES),
        name=name,
    )(q, k, vt, lamv, subln)


def _rope_tables(n_tokens):
    t = np.arange(n_tokens)
    row = (t // GRID_W).astype(np.float32)
    col = (t % GRID_W).astype(np.float32)
    half = DH // 2
    inv = np.float32(ROPE_THETA) ** (-(np.arange(0, half, 2, dtype=np.float32) / np.float32(half)))
    ang_r = row[:, None] * inv
    ang_c = col[:, None] * inv
    cos = np.concatenate([np.cos(ang_r)] * 2 + [np.cos(ang_c)] * 2, axis=1)
    sin = np.concatenate([-np.sin(ang_r), np.sin(ang_r), -np.sin(ang_c), np.sin(ang_c)], axis=1)
    reps = LANES // DH
    return (jnp.asarray(np.concatenate([cos] * reps, axis=1), F32),
            jnp.asarray(np.concatenate([sin] * reps, axis=1), F32))


def _dup_heads(a):
    parts = []
    for n in range(H_KV):
        head = a[..., n * DH:(n + 1) * DH]
        parts += [head, head]
    return jnp.concatenate(parts, axis=-1)


def _with_sum_feature(a):
    ones = jnp.ones(a.shape[:-1] + (1,), a.dtype)
    zeros = jnp.zeros(a.shape[:-1] + (DH - 1,), a.dtype)
    parts = []
    for n in range(H_KV):
        parts += [a[..., n * DH:(n + 1) * DH], ones, zeros]
    return jnp.concatenate(parts, axis=-1)


def kernel(x_prompt, x_sample, c, cache_diff_k, cache_diff_v, cache_gqa_k, cache_gqa_v, c_ctx, w_ada, b_ada, norm_ff1, w_ff1_gu, w_ff1_down, norm_mix, w_in, q_norm, k_norm, lambda_q1, lambda_k1, lambda_q2, lambda_k2, subln, w_out, norm_ff2, w_ff2_gu, w_ff2_down, final_norm):
    assert w_ada.shape[0] == 1, "single trunk layer"
    bc, tc_, _ = x_prompt.shape
    bs, ts, _ = x_sample.shape
    tpast = cache_diff_k.shape[2]
    assert bs + 1 <= MOD_ROWS
    ctx_row = bs

    cvecs = jnp.concatenate([c, c_ctx[None], jnp.zeros((MOD_ROWS - bs - 1, D_MODEL), F32)], axis=0)
    wgu1 = w_ff1_gu[0].astype(BF16)
    wd1 = w_ff1_down[0].astype(BF16)
    wi = w_in[0]
    n_qkvq = 3 * DQ_DIFF + DQ_GQA
    w_ext = jnp.concatenate(
        [wi[:, :n_qkvq], _dup_heads(wi[:, n_qkvq:n_qkvq + DKV_GQA]), _dup_heads(wi[:, n_qkvq + DKV_GQA:])],
        axis=1).astype(BF16)
    seg128 = (jnp.arange(LANES)[:, None] // DH == jnp.arange(LANES)[None, :] // DH).astype(BF16)
    seg = jnp.concatenate([seg128, seg128], axis=0)
    qkn = jnp.concatenate([jnp.tile(q_norm[0], H_GQA), jnp.tile(k_norm[0], 2 * H_KV)])[None]
    lamv = jnp.stack([lambda_q1[0], lambda_k1[0], lambda_q2[0], lambda_k2[0]])
    g1 = norm_ff1
    gm = norm_mix
    g2 = norm_ff2
    fg = final_norm[None]
    cos, sin = _rope_tables(ts)
    kcache = jnp.concatenate([cache_diff_k[:, 0].reshape(bs, tpast, DQ_DIFF),
                              _dup_heads(cache_gqa_k[:, 0].reshape(bs, tpast, DKV_GQA))], axis=-1).astype(BF16)
    vcache = jnp.concatenate([cache_diff_v[:, 0].reshape(bs, tpast, DQ_DIFF),
                              _with_sum_feature(cache_gqa_v[:, 0].reshape(bs, tpast, DKV_GQA))], axis=-1).astype(BF16)
    vcache_t = vcache.transpose(2, 0, 1).reshape(D_KV_ALL, bs * tpast)

    mod = _mod_call(cvecs, w_ada[0], b_ada).reshape(MOD_ROWS, N_MOD, D_MODEL)

    ctx_map = lambda i: (ctx_row, 0, 0)
    xc = x_prompt.reshape(bc * tc_, D_MODEL)
    xc, wgu2, wd2, wout = _ffn_call(xc, mod, ctx_map, g1, wgu1, wd1, sub=0,
                                    side_casts=(w_ff2_gu[0], w_ff2_down[0], w_out[0]), name="ffn1_ctx")
    qc, kc, vtc, dkt32, dv4, gkt32, gvt32 = _proj_call(
        xc, mod, ctx_map, gm, w_ext, seg, qkn, rope_tables=None, cache_seq=tc_, sum_row=False, name="proj_ctx")
    odc, ogc = _attn_ctx_call(qc.reshape(bc, tc_, D_MIX), kc.reshape(bc, tc_, D_KV_ALL), vtc,
                              lamv, subln, name="attn_ctx")
    yc = _ffn_call(xc, mod, ctx_map, g2, wgu2, wd2, sub=2,
                   pre=(odc.reshape(bc * tc_, DQ_DIFF), ogc.reshape(bc * tc_, DQ_GQA), wout),
                   final_gain=fg, name="ffn2_ctx")

    ffn_tiles = ts // FFN_TILE
    proj_tiles = ts // PROJ_TILE
    xs = x_sample.reshape(bs * ts, D_MODEL)
    xs = _ffn_call(xs, mod, lambda i: (i // ffn_tiles, 0, 0), g1, wgu1, wd1, sub=0, name="ffn1_smp")
    qs, ks, vts = _proj_call(xs, mod, lambda i: (i // proj_tiles, 0, 0), gm, w_ext, seg, qkn,
                             rope_tables=(cos, sin), cache_seq=None, sum_row=True, name="proj_smp")
    qs = qs.reshape(bs, ts, D_MIX)
    ks = ks.reshape(bs, ts, D_KV_ALL)
    ods = _attn_sample_call(qs, ks, vts, kcache, vcache_t, lamv, subln, is_diff=True, name="attn_diff_smp")
    ogs = _attn_sample_call(qs, ks, vts, kcache, vcache_t, lamv, subln, is_diff=False, name="attn_gqa_smp")
    ys = _ffn_call(xs, mod, lambda i: (i // ffn_tiles, 0, 0), g2, wgu2, wd2, sub=2,
                   pre=(ods.reshape(bs * ts, DQ_DIFF), ogs.reshape(bs * ts, DQ_GQA), wout),
                   final_gain=fg, name="ffn2_smp")

    return (yc.reshape(bc, tc_, D_MODEL),
            ys.reshape(bs, ts, D_MODEL),
            dkt32.reshape(bc, 1, H_DIFF, 2, DH, tc_).transpose(0, 1, 5, 2, 3, 4),
            dv4.reshape(bc, 1, tc_, H_DIFF, 2 * DH),
            gkt32.reshape(bc, 1, H_KV, DH, tc_).transpose(0, 1, 4, 2, 3),
            gvt32.reshape(bc, 1, H_KV, DH, tc_).transpose(0, 1, 4, 2, 3))
```

```python
import functools
import math

import jax
import jax.numpy as jnp
import numpy as np
from jax import lax
from jax.experimental import pallas as pl
from jax.experimental.pallas import tpu as pltpu

F32 = jnp.float32
BF16 = jnp.bfloat16

D_MODEL = 1024
N_MOD = 9
D_FF = 2816
H_DIFF = 4
DH = 64
H_GQA = 8
H_KV = 2
G_GQA = H_GQA // H_KV
GRID_W = 64
ROPE_THETA = 10000.0
EPS = 1e-6
LAMBDA_INIT = 0.8 - 0.6 * math.exp(-0.3 * 0)
DQ_DIFF = H_DIFF * 2 * DH
DQ_GQA = H_GQA * DH
DKV_GQA = H_KV * DH
ROPE_HALF = DH // 4
D_MIX = DQ_DIFF + DQ_GQA
D_KV_ALL = DQ_DIFF + 2 * DKV_GQA
D_NORMED = DQ_GQA + 2 * DKV_GQA
SCORE_SCALE = DH ** -0.5 * math.log2(math.e)

LANES = 128
SUBLANES = 8
BF16_ROWS = 16
VMEM_LIMIT_BYTES = 60000 * 1024

MOD_ROWS = 8
MOD_TILE_N = 4608
FFN_TILE = 1024
FFN_HIDDEN_SPLIT = 1536
PROJ_TILE = 1024
ATTN_TILE_Q = 256
ATTN_CHUNK_K = 512
ATTN_SKEW = 2
ATTN_COLS_PER_STEP = 2
ATTN_CTX_MAPS_PER_GROUP = 8


def _sigmoid(x):
    return 1.0 / (1.0 + jnp.exp(-x))


def _rms(x):
    return x * lax.rsqrt(jnp.mean(x * x, axis=-1, keepdims=True) + EPS)


def _dot(a, b):
    return jnp.dot(a, b, preferred_element_type=F32)


def _dot_nt(a, b):
    return lax.dot_general(a, b, (((1,), (1,)), ((), ())), preferred_element_type=F32)


def _resident(shape):
    return pl.BlockSpec(shape, lambda *_: (0,) * len(shape), pipeline_mode=pl.Buffered(1))


def _mod_kernel(c_ref, w_ref, b_ref, o_ref):
    c = c_ref[...]
    s = c * _sigmoid(c)
    o_ref[...] = _dot(s.astype(BF16), w_ref[...].astype(BF16)) + b_ref[...]


def _mod_call(cvecs, w_ada, b_ada):
    n = w_ada.shape[1]
    return pl.pallas_call(
        _mod_kernel,
        grid=(n // MOD_TILE_N,),
        in_specs=[
            pl.BlockSpec((MOD_ROWS, D_MODEL), lambda j: (0, 0)),
            pl.BlockSpec((D_MODEL, MOD_TILE_N), lambda j: (0, j)),
            pl.BlockSpec((1, MOD_TILE_N), lambda j: (0, j)),
        ],
        out_specs=pl.BlockSpec((MOD_ROWS, MOD_TILE_N), lambda j: (0, j)),
        out_shape=jax.ShapeDtypeStruct((MOD_ROWS, n), F32),
        name="mod",
    )(cvecs, w_ada, b_ada)


def _ffn_kernel(*refs, sub, pre, final, n_side):
    refs = list(refs)
    x_ref, mod_ref = refs[:2]
    pos = 2
    if pre:
        od_ref, og_ref, wout_ref = refs[pos:pos + 3]
        pos += 3
    g_ref, wgu_ref, wd_ref = refs[pos:pos + 3]
    pos += 3
    if final:
        fg_ref = refs[pos]
        pos += 1
    side_in = refs[pos:pos + n_side]
    o_ref = refs[pos + n_side]
    side_out = refs[pos + n_side + 1:]

    x = x_ref[...]
    mod = mod_ref[0]
    if pre:
        o = jnp.concatenate([od_ref[...], og_ref[...]], axis=1)
        x = x + mod[5:6] * _dot(o, wout_ref[...])
    shift = mod[3 * sub:3 * sub + 1]
    scale = mod[3 * sub + 1:3 * sub + 2]
    gate = mod[3 * sub + 2:3 * sub + 3]
    h = (_rms(x) * g_ref[...]) * (1.0 + scale) + shift
    hb = h.astype(BF16)
    y = None
    for j0, j1 in ((0, FFN_HIDDEN_SPLIT), (FFN_HIDDEN_SPLIT, D_FF)):
        g = _dot(hb, wgu_ref[:, j0:j1])
        u = _dot(hb, wgu_ref[:, D_FF + j0:D_FF + j1])
        act = (g * _sigmoid(g)) * u
        yj = _dot(act.astype(BF16), wd_ref[j0:j1, :])
        y = yj if y is None else y + yj
    x = x + (0.5 * gate) * y
    if final:
        x = _rms(x) * fg_ref[...]
    o_ref[...] = x
    for src, dst in zip(side_in, side_out):
        dst[...] = src[...].astype(BF16)


def _ffn_call(x, mod, row_map, gain, wgu, wd, *, sub, pre=None, final_gain=None, side_casts=(), name):
    t = x.shape[0]
    tm = FFN_TILE
    steps = t // tm
    row_spec = lambda w: pl.BlockSpec((tm, w), lambda i: (i, 0))
    in_specs = [row_spec(D_MODEL), pl.BlockSpec((1, N_MOD, D_MODEL), row_map)]
    args = [x, mod]
    if pre is not None:
        od, og, wout = pre
        in_specs += [row_spec(DQ_DIFF), row_spec(DQ_GQA), _resident((D_MIX, D_MODEL))]
        args += [od, og, wout]
    in_specs += [_resident((1, D_MODEL)), _resident((D_MODEL, 2 * D_FF)), _resident((D_FF, D_MODEL))]
    args += [gain, wgu, wd]
    if final_gain is not None:
        in_specs.append(_resident((1, D_MODEL)))
        args.append(final_gain)
    out_specs = [row_spec(D_MODEL)]
    out_shape = [jax.ShapeDtypeStruct((t, D_MODEL), F32)]
    for w in side_casts:
        rows, cols = w.shape
        assert rows % (steps * BF16_ROWS) == 0
        spec = pl.BlockSpec((rows // steps, cols), lambda i: (i, 0))
        in_specs.append(spec)
        args.append(w)
        out_specs.append(spec)
        out_shape.append(jax.ShapeDtypeStruct((rows, cols), BF16))
    outs = pl.pallas_call(
        functools.partial(_ffn_kernel, sub=sub, pre=pre is not None, final=final_gain is not None,
                          n_side=len(side_casts)),
        grid=(steps,),
        in_specs=in_specs,
        out_specs=out_specs,
        out_shape=out_shape,
        compiler_params=pltpu.CompilerParams(vmem_limit_bytes=VMEM_LIMIT_BYTES),
        name=name,
    )(*args)
    return outs if side_casts else outs[0]


def _rope(x, cos, sin, first_of_pair):
    w = x.shape[1]
    xb = x.astype(BF16)
    partner = jnp.where(first_of_pair, pltpu.roll(xb, w - ROPE_HALF, 1), pltpu.roll(xb, ROPE_HALF, 1)).astype(F32)
    reps = w // LANES
    cos_w = jnp.concatenate([cos] * reps, axis=1) if reps > 1 else cos
    sin_w = jnp.concatenate([sin] * reps, axis=1) if reps > 1 else sin
    return x * cos_w + partner * sin_w


def _proj_kernel(*refs, rope, emit_f32, sum_row):
    refs = list(refs)
    x_ref, mod_ref, g_ref, w_ref, seg_ref, qkn_ref = refs[:6]
    pos = 6
    if rope:
        cos_ref, sin_ref = refs[pos:pos + 2]
        pos += 2
    q_ref, k_ref, vt_ref = refs[pos:pos + 3]
    pos += 3

    x = x_ref[...]
    mod = mod_ref[0]
    h = (_rms(x) * g_ref[...]) * (1.0 + mod[4:5]) + mod[3:4]
    qkv = _dot(h.astype(BF16), w_ref[...])
    dq = qkv[:, 0:DQ_DIFF]
    dk = qkv[:, DQ_DIFF:2 * DQ_DIFF]
    dv = qkv[:, 2 * DQ_DIFF:3 * DQ_DIFF]
    n0 = 3 * DQ_DIFF
    raw = qkv[:, n0:n0 + D_NORMED]
    gv = qkv[:, n0 + D_NORMED:]

    sq = raw * raw
    hi = sq.astype(BF16)
    lo = (sq - hi.astype(F32)).astype(BF16)
    seg = seg_ref[...]
    ss = jnp.concatenate(
        [_dot(jnp.concatenate([hi[:, c:c + LANES], lo[:, c:c + LANES]], axis=1), seg)
         for c in range(0, D_NORMED, LANES)], axis=1)
    normed = (raw * lax.rsqrt(ss * (1.0 / DH) + EPS)) * qkn_ref[...]
    gq = normed[:, :DQ_GQA]
    gk = normed[:, DQ_GQA:]

    if rope:
        cos = cos_ref[...]
        sin = sin_ref[...]
        lane = lax.broadcasted_iota(jnp.int32, (x.shape[0], LANES), 1)
        first = (lane % (2 * ROPE_HALF)) < ROPE_HALF
        first4 = jnp.concatenate([first] * 4, axis=1)
        first2 = jnp.concatenate([first] * 2, axis=1)
        dq = _rope(dq, cos, sin, first4)
        dk = _rope(dk, cos, sin, first4)
        gq = _rope(gq, cos, sin, first4)
        gk = _rope(gk, cos, sin, first2)

    if sum_row:
        lane = lax.broadcasted_iota(jnp.int32, (x.shape[0], LANES), 1)
        marker = jnp.where(lane == DH, 1.0, 0.0)
        gv = jnp.concatenate([jnp.where(lane < DH, gv[:, c:c + LANES], marker)
                              for c in range(0, gv.shape[1], LANES)], axis=1)

    q_ref[...] = jnp.concatenate([dq * SCORE_SCALE, gq * SCORE_SCALE], axis=1).astype(BF16)
    k_ref[...] = jnp.concatenate([dk, gk], axis=1).astype(BF16)
    vt_ref[...] = jnp.concatenate([dv, gv], axis=1).T.astype(BF16)

    if emit_f32:
        dkt_ref, dv4_ref, gkt_ref, gvt_ref = refs[pos:pos + 4]
        tm = x.shape[0]
        seq = dkt_ref.shape[2]
        lane = lax.broadcasted_iota(jnp.int32, (tm, LANES), 1)
        low = lane < DH
        dkt = dk.T
        gkt = jnp.where(low, gk[:, :LANES], gk[:, LANES:]).T
        gvt = jnp.where(low, gv[:, :LANES], gv[:, LANES:]).T
        for b in range(tm // seq):
            dkt_ref[b] = dkt[:, b * seq:(b + 1) * seq]
            gkt_ref[b] = gkt[:, b * seq:(b + 1) * seq]
            gvt_ref[b] = gvt[:, b * seq:(b + 1) * seq]
        for hd in range(H_DIFF):
            dv4_ref[pl.ds(hd, tm, stride=H_DIFF), :] = dv[:, hd * LANES:(hd + 1) * LANES]


def _proj_call(x, mod, row_map, gain, w_ext, seg, qkn, *, rope_tables, cache_seq, sum_row, name):
    t = x.shape[0]
    emit_f32 = cache_seq is not None
    tm = PROJ_TILE
    row_spec = lambda w: pl.BlockSpec((tm, w), lambda i: (i, 0))
    n_ext = w_ext.shape[1]
    in_specs = [row_spec(D_MODEL), pl.BlockSpec((1, N_MOD, D_MODEL), row_map),
                _resident((1, D_MODEL)), _resident((D_MODEL, n_ext)),
                _resident((2 * LANES, LANES)), _resident((1, D_NORMED))]
    args = [x, mod, gain, w_ext, seg, qkn]
    if rope_tables is not None:
        cos, sin = rope_tables
        tiles_per_seq = cos.shape[0] // tm
        tab_spec = pl.BlockSpec((tm, LANES), lambda i: (i % tiles_per_seq, 0))
        in_specs += [tab_spec, tab_spec]
        args += [cos, sin]
    out_specs = [row_spec(D_MIX), row_spec(D_KV_ALL), pl.BlockSpec((D_KV_ALL, tm), lambda i: (0, i))]
    out_shape = [jax.ShapeDtypeStruct((t, D_MIX), BF16),
                 jax.ShapeDtypeStruct((t, D_KV_ALL), BF16),
                 jax.ShapeDtypeStruct((D_KV_ALL, t), BF16)]
    if emit_f32:
        assert tm % cache_seq == 0
        nb = tm // cache_seq
        slab = lambda rows: pl.BlockSpec((nb, rows, cache_seq), lambda i: (i, 0, 0))
        out_specs += [slab(DQ_DIFF), pl.BlockSpec((tm * H_DIFF, LANES), lambda i: (i, 0)),
                      slab(DKV_GQA), slab(DKV_GQA)]
        out_shape += [jax.ShapeDtypeStruct((t // cache_seq, DQ_DIFF, cache_seq), F32),
                      jax.ShapeDtypeStruct((t * H_DIFF, LANES), F32),
                      jax.ShapeDtypeStruct((t // cache_seq, DKV_GQA, cache_seq), F32),
                      jax.ShapeDtypeStruct((t // cache_seq, DKV_GQA, cache_seq), F32)]
    return pl.pallas_call(
        functools.partial(_proj_kernel, rope=rope_tables is not None, emit_f32=emit_f32, sum_row=sum_row),
        grid=(t // tm,),
        in_specs=in_specs,
        out_specs=out_specs,
        out_shape=out_shape,
        compiler_params=pltpu.CompilerParams(vmem_limit_bytes=VMEM_LIMIT_BYTES),
        name=name,
    )(*args)


def _lambda(lamv_ref):
    lamv = lamv_ref[...]
    return (jnp.exp(jnp.sum(lamv[0:1] * lamv[1:2], axis=-1, keepdims=True))
            - jnp.exp(jnp.sum(lamv[2:3] * lamv[3:4], axis=-1, keepdims=True)) + LAMBDA_INIT)


def _fold_rows(x, op):
    return op(x.reshape(x.shape[0] // SUBLANES, SUBLANES, x.shape[1]), axis=0)


def _value_rows(kvc):
    return slice(kvc * LANES, (kvc + 1) * LANES)


def _combine_heads(is_diff, ot_a, ot_b, lam, subln):
    if is_diff:
        o = (ot_a - lam * ot_b).T
        return (_rms(o) * subln) * (1.0 - LAMBDA_INIT)
    return jnp.concatenate([ot_a[:DH], ot_b[:DH]], axis=0).T


def _attn_kernel(q_ref, k_ref, vt_ref, lamv_ref, subln_ref, od_ref, og_ref, *, items, maps_per_group):
    tq = q_ref.shape[1]
    low = lax.broadcasted_iota(jnp.int32, (tq, LANES), 1) < DH

    class SoftmaxMap:
        def __init__(self, qm, kvc):
            self.qm = qm
            self.cs = slice(kvc * LANES, (kvc + 1) * LANES)
            self.vrows = _value_rows(kvc)

        def score_pass(self):
            self.s = _dot_nt(k_ref[0, :, self.cs], self.qm)
            self.m = _fold_rows(self.s, jnp.max).max(axis=0, keepdims=True)

        def pv_pass(self):
            e = jnp.exp2(self.s - self.m)
            tot = _fold_rows(e, jnp.sum).sum(axis=0, keepdims=True)
            self.ot = _dot(vt_ref[self.vrows, :], e.astype(BF16)) * (1.0 / tot)

    maps = []
    for is_diff, qc, kvc, _ in items:
        q = q_ref[0, :, qc * LANES:(qc + 1) * LANES]
        zero = jnp.zeros_like(q)
        maps += [SoftmaxMap(jnp.where(low, q, zero), kvc), SoftmaxMap(jnp.where(low, zero, q), kvc)]

    groups = [maps[g:g + maps_per_group] for g in range(0, len(maps), maps_per_group)]
    for stage in range(len(groups) + 1):
        if stage > 0:
            for mp in groups[stage - 1]:
                mp.pv_pass()
        if stage < len(groups):
            for mp in groups[stage]:
                mp.score_pass()

    lam = _lambda(lamv_ref)
    for n, (is_diff, _, _, oc) in enumerate(items):
        o = _combine_heads(is_diff, maps[2 * n].ot, maps[2 * n + 1].ot, lam, subln_ref[...])
        out_ref = od_ref if is_diff else og_ref
        out_ref[0, :, oc * LANES:(oc + 1) * LANES] = o.astype(BF16)


def _attn_loop_kernel(q_ref, kn_ref, vtn_ref, kc_ref, vtc_ref, lamv_ref, subln_ref, out_ref, *scratch,
                      is_diff, kv_cols):
    tq = ATTN_TILE_Q
    n_tiles = q_ref.shape[1] // tq
    n_groups = len(kv_cols)
    s_refs = scratch[:n_groups]
    m_refs = scratch[n_groups:]
    t_new = kn_ref.shape[1]
    segs = [(kn_ref, vtn_ref, 0), (kc_ref, vtc_ref, t_new)]
    chunks = []
    for si, (k_ref, _, base) in enumerate(segs):
        n = k_ref.shape[1]
        chunks += [(si, c0, min(n, c0 + ATTN_CHUNK_K), base + c0) for c0 in range(0, n, ATTN_CHUNK_K)]
    n_chunks = len(chunks)
    per_tile = n_groups * n_chunks
    lead = n_chunks + ATTN_SKEW
    assert 0 < ATTN_SKEW < n_chunks and lead < per_tile

    low = lax.broadcasted_iota(jnp.int32, (tq, LANES), 1) < DH
    lam = _lambda(lamv_ref)

    def rows(tile):
        return pl.ds(pl.multiple_of(tile * tq, tq), tq)

    def masked_q(tile, g):
        q = q_ref[0, rows(tile), g * LANES:(g + 1) * LANES]
        zero = jnp.zeros_like(q)
        return jnp.concatenate([jnp.where(low, q, zero), jnp.where(low, zero, q)], axis=0)

    def score_unit(qm, g, ci, mpart):
        si, c0, c1, r0 = chunks[ci]
        cs = slice(kv_cols[g] * LANES, (kv_cols[g] + 1) * LANES)
        s = _dot_nt(segs[si][0][0, c0:c1, cs], qm)
        s_refs[g][r0:r0 + c1 - c0, :] = s
        part = _fold_rows(s, jnp.max)
        mpart = part if mpart is None else jnp.maximum(mpart, part)
        if ci == n_chunks - 1:
            m_refs[g][...] = mpart.max(axis=0, keepdims=True)
            return None
        return mpart

    def pv_unit(g, ci, m, acc, lpart):
        si, c0, c1, r0 = chunks[ci]
        e = jnp.exp2(s_refs[g][r0:r0 + c1 - c0, :] - m)
        pv = _dot(segs[si][1][_value_rows(kv_cols[g]), c0:c1], e.astype(BF16))
        acc = pv if acc is None else acc + pv
        if not is_diff:
            return acc, None
        part = _fold_rows(e, jnp.sum)
        return acc, (part if lpart is None else lpart + part)

    def emit(g, tile, acc, lpart):
        lsum = lpart.sum(axis=0, keepdims=True) if is_diff else acc[DH:DH + 1]
        ot = acc * (1.0 / lsum)
        o = _combine_heads(is_diff, ot[:, :tq], ot[:, tq:], lam, subln_ref[...])
        out_ref[0, rows(tile), g * LANES:(g + 1) * LANES] = o.astype(BF16)

    mpart = None
    qm = None
    for u in range(lead):
        g, ci = divmod(u, n_chunks)
        if ci == 0:
            qm = masked_q(0, g)
        mpart = score_unit(qm, g, ci, mpart)

    def body(t, carried_mpart, final_tile=False):
        a_state = {}
        a_state[(lead // per_tile, (lead // n_chunks) % n_groups)] = (None, carried_mpart)
        acc = lpart = m = None
        for j in range(per_tile):
            b_g, b_ci = divmod(j, n_chunks)
            if b_ci == 0:
                m = m_refs[b_g][...]
                acc = lpart = None
            acc, lpart = pv_unit(b_g, b_ci, m, acc, lpart)
            if b_ci == n_chunks - 1:
                emit(b_g, t, acc, lpart)

            a_off, a_rem = divmod(j + lead, per_tile)
            if final_tile and a_off > 0:
                continue
            a_g, a_ci = divmod(a_rem, n_chunks)
            a_qm, a_mpart = a_state.get((a_off, a_g), (None, None))
            if a_qm is None:
                a_qm = masked_q(t + a_off, a_g)
            a_state[(a_off, a_g)] = (a_qm, score_unit(a_qm, a_g, a_ci, a_mpart))

        last = per_tile - 1 + lead
        return a_state.get((last // per_tile, (last % per_tile) // n_chunks), (None, None))[1]

    mpart = lax.fori_loop(0, n_tiles - 1, body, mpart)
    body(jnp.int32(n_tiles - 1), mpart, final_tile=True)


def _attn_sample_call(q, k, vt, kc, vtc, lamv, subln, *, is_diff, name):
    b, t, _ = q.shape
    tc = kc.shape[1]
    qw = ATTN_COLS_PER_STEP * LANES
    nblk = DQ_DIFF // qw
    if is_diff:
        kw = qw
        q0 = 0
        kv0 = 0
        kv_cols = tuple(range(ATTN_COLS_PER_STEP))
    else:
        assert ATTN_COLS_PER_STEP * (LANES // DH) == G_GQA
        kw = LANES
        q0 = DQ_DIFF // qw
        kv0 = DQ_DIFF // kw
        kv_cols = (0,) * ATTN_COLS_PER_STEP
    in_specs = [
        pl.BlockSpec((1, t, qw), lambda bi, j: (bi, 0, q0 + j)),
        pl.BlockSpec((1, t, kw), lambda bi, j: (bi, 0, kv0 + j)),
        pl.BlockSpec((kw, t), lambda bi, j: (kv0 + j, bi)),
        pl.BlockSpec((1, tc, kw), lambda bi, j: (bi, 0, kv0 + j)),
        pl.BlockSpec((kw, tc), lambda bi, j: (kv0 + j, bi)),
        pl.BlockSpec((4, DH), lambda bi, j: (0, 0)),
        pl.BlockSpec((1, LANES), lambda bi, j: (0, 0)),
    ]
    return pl.pallas_call(
        functools.partial(_attn_loop_kernel, is_diff=is_diff, kv_cols=kv_cols),
        grid=(b, nblk),
        in_specs=in_specs,
        out_specs=pl.BlockSpec((1, t, qw), lambda bi, j: (bi, 0, j)),
        out_shape=jax.ShapeDtypeStruct((b, t, nblk * qw), BF16),
        scratch_shapes=([pltpu.VMEM((t + tc, 2 * ATTN_TILE_Q), F32)] * ATTN_COLS_PER_STEP
                        + [pltpu.VMEM((1, 2 * ATTN_TILE_Q), F32)] * ATTN_COLS_PER_STEP),
        compiler_params=pltpu.CompilerParams(vmem_limit_bytes=VMEM_LIMIT_BYTES),
        name=name,
    )(q, k, vt, kc, vtc, lamv, subln)


def _attn_ctx_call(q, k, vt, lamv, subln, *, name):
    b, t, _ = q.shape
    ncol = DQ_DIFF // LANES
    items = tuple((True, j, j, j) for j in range(ncol)) + tuple(
        (False, ncol + j, ncol + j // (LANES // DH), j) for j in range(ncol))
    whole = lambda w: pl.BlockSpec((1, t, w), lambda bi: (bi, 0, 0))
    return pl.pallas_call(
        functools.partial(_attn_kernel, items=items, maps_per_group=ATTN_CTX_MAPS_PER_GROUP),
        grid=(b,),
        in_specs=[whole(D_MIX), whole(D_KV_ALL),
                  pl.BlockSpec((D_KV_ALL, t), lambda bi: (0, bi)),
                  pl.BlockSpec((4, DH), lambda bi: (0, 0)),
                  pl.BlockSpec((1, LANES), lambda bi: (0, 0))],
        out_specs=[whole(DQ_DIFF), whole(DQ_GQA)],
        out_shape=[jax.ShapeDtypeStruct((b, t, DQ_DIFF), BF16),
                   jax.ShapeDtypeStruct((b, t, DQ_GQA), BF16)],
        compiler_params=pltpu.CompilerParams(vmem_limit_bytes=VMEM_LIMIT_BYTES),
        name=name,
    )(q, k, vt, lamv, subln)


def _rope_tables(n_tokens):
    t = np.arange(n_tokens)
    row = (t // GRID_W).astype(np.float32)
    col = (t % GRID_W).astype(np.float32)
    half = DH // 2
    inv = np.float32(ROPE_THETA) ** (-(np.arange(0, half, 2, dtype=np.float32) / np.float32(half)))
    ang_r = row[:, None] * inv
    ang_c = col[:, None] * inv
    cos = np.concatenate([np.cos(ang_r)] * 2 + [np.cos(ang_c)] * 2, axis=1)
    sin = np.concatenate([-np.sin(ang_r), np.sin(ang_r), -np.sin(ang_c), np.sin(ang_c)], axis=1)
    reps = LANES // DH
    return (jnp.asarray(np.concatenate([cos] * reps, axis=1), F32),
            jnp.asarray(np.concatenate([sin] * reps, axis=1), F32))


def _dup_heads(a):
    parts = []
    for n in range(H_KV):
        head = a[..., n * DH:(n + 1) * DH]
        parts += [head, head]
    return jnp.concatenate(parts, axis=-1)


def _with_sum_feature(a):
    ones = jnp.ones(a.shape[:-1] + (1,), a.dtype)
    zeros = jnp.zeros(a.shape[:-1] + (DH - 1,), a.dtype)
    parts = []
    for n in range(H_KV):
        parts += [a[..., n * DH:(n + 1) * DH], ones, zeros]
    return jnp.concatenate(parts, axis=-1)


def kernel(x_prompt, x_sample, c, cache_diff_k, cache_diff_v, cache_gqa_k, cache_gqa_v, c_ctx, w_ada, b_ada, norm_ff1, w_ff1_gu, w_ff1_down, norm_mix, w_in, q_norm, k_norm, lambda_q1, lambda_k1, lambda_q2, lambda_k2, subln, w_out, norm_ff2, w_ff2_gu, w_ff2_down, final_norm):
    assert w_ada.shape[0] == 1, "single trunk layer"
    bc, tc_, _ = x_prompt.shape
    bs, ts, _ = x_sample.shape
    tpast = cache_diff_k.shape[2]
    assert bs + 1 <= MOD_ROWS
    ctx_row = bs

    cvecs = jnp.concatenate([c, c_ctx[None], jnp.zeros((MOD_ROWS - bs - 1, D_MODEL), F32)], axis=0)
    wgu1 = w_ff1_gu[0].astype(BF16)
    wd1 = w_ff1_down[0].astype(BF16)
    wi = w_in[0]
    n_qkvq = 3 * DQ_DIFF + DQ_GQA
    w_ext = jnp.concatenate(
        [wi[:, :n_qkvq], _dup_heads(wi[:, n_qkvq:n_qkvq + DKV_GQA]), _dup_heads(wi[:, n_qkvq + DKV_GQA:])],
        axis=1).astype(BF16)
    seg128 = (jnp.arange(LANES)[:, None] // DH == jnp.arange(LANES)[None, :] // DH).astype(BF16)
    seg = jnp.concatenate([seg128, seg128], axis=0)
    qkn = jnp.concatenate([jnp.tile(q_norm[0], H_GQA), jnp.tile(k_norm[0], 2 * H_KV)])[None]
    lamv = jnp.stack([lambda_q1[0], lambda_k1[0], lambda_q2[0], lambda_k2[0]])
    g1 = norm_ff1
    gm = norm_mix
    g2 = norm_ff2
    fg = final_norm[None]
    cos, sin = _rope_tables(ts)
    kcache = jnp.concatenate([cache_diff_k[:, 0].reshape(bs, tpast, DQ_DIFF),
                              _dup_heads(cache_gqa_k[:, 0].reshape(bs, tpast, DKV_GQA))], axis=-1).astype(BF16)
    vcache = jnp.concatenate([cache_diff_v[:, 0].reshape(bs, tpast, DQ_DIFF),
                              _with_sum_feature(cache_gqa_v[:, 0].reshape(bs, tpast, DKV_GQA))], axis=-1).astype(BF16)
    vcache_t = vcache.transpose(2, 0, 1).reshape(D_KV_ALL, bs * tpast)

    mod = _mod_call(cvecs, w_ada[0], b_ada).reshape(MOD_ROWS, N_MOD, D_MODEL)

    ctx_map = lambda i: (ctx_row, 0, 0)
    xc = x_prompt.reshape(bc * tc_, D_MODEL)
    xc, wgu2, wd2, wout = _ffn_call(xc, mod, ctx_map, g1, wgu1, wd1, sub=0,
                                    side_casts=(w_ff2_gu[0], w_ff2_down[0], w_out[0]), name="ffn1_ctx")
    qc, kc, vtc, dkt32, dv4, gkt32, gvt32 = _proj_call(
        xc, mod, ctx_map, gm, w_ext, seg, qkn, rope_tables=None, cache_seq=tc_, sum_row=False, name="proj_ctx")
    odc, ogc = _attn_ctx_call(qc.reshape(bc, tc_, D_MIX), kc.reshape(bc, tc_, D_KV_ALL), vtc,
                              lamv, subln, name="attn_ctx")
    yc = _ffn_call(xc, mod, ctx_map, g2, wgu2, wd2, sub=2,
                   pre=(odc.reshape(bc * tc_, DQ_DIFF), ogc.reshape(bc * tc_, DQ_GQA), wout),
                   final_gain=fg, name="ffn2_ctx")

    ffn_tiles = ts // FFN_TILE
    proj_tiles = ts // PROJ_TILE
    xs = x_sample.reshape(bs * ts, D_MODEL)
    xs = _ffn_call(xs, mod, lambda i: (i // ffn_tiles, 0, 0), g1, wgu1, wd1, sub=0, name="ffn1_smp")
    qs, ks, vts = _proj_call(xs, mod, lambda i: (i // proj_tiles, 0, 0), gm, w_ext, seg, qkn,
                             rope_tables=(cos, sin), cache_seq=None, sum_row=True, name="proj_smp")
    qs = qs.reshape(bs, ts, D_MIX)
    ks = ks.reshape(bs, ts, D_KV_ALL)
    ods = _attn_sample_call(qs, ks, vts, kcache, vcache_t, lamv, subln, is_diff=True, name="attn_diff_smp")
    ogs = _attn_sample_call(qs, ks, vts, kcache, vcache_t, lamv, subln, is_diff=False, name="attn_gqa_smp")
    ys = _ffn_call(xs, mod, lambda i: (i // ffn_tiles, 0, 0), g2, wgu2, wd2, sub=2,
                   pre=(ods.reshape(bs * ts, DQ_DIFF), ogs.reshape(bs * ts, DQ_GQA), wout),
                   final_gain=fg, name="ffn2_smp")

    return (yc.reshape(bc, tc_, D_MODEL),
            ys.reshape(bs, ts, D_MODEL),
            dkt32.reshape(bc, 1, H_DIFF, 2, DH, tc_).transpose(0, 1, 5, 2, 3, 4),
            dv4.reshape(bc, 1, tc_, H_DIFF, 2 * DH),
            gkt32.reshape(bc, 1, H_KV, DH, tc_).transpose(0, 1, 4, 2, 3),
            gvt32.reshape(bc, 1, H_KV, DH, tc_).transpose(0, 1, 4, 2, 3))
```
